```python
import jax, jax.numpy as jnp
from jax import lax
import numpy as np

D_MODEL = 2048
BATCH = 8
SEQ = 2048
DEPTH = 4

N_META = 16
BLOCK = 128
EPS = 1e-6
NEG_INF = -1e30
ROPE_THETA = 10000.0
MLA_HEADS = 8
MLA_Q_RANK = 512
MLA_KV_RANK = 512
MLA_NOPE = 128
MLA_ROPE = 64
MLA_V = 128
CONV_WIDTH = 1024
CONV_K = 3
FOX_HEADS = 8
FOX_HEAD_DIM = 128
FORGET_BIAS_MEAN = 2.0
N_BRANCH = 3
BRANCH_WIDTH = 1024
D_FF = -(-8 * D_MODEL // (3 * 256)) * 256
FOX_WIDTH = FOX_HEADS * FOX_HEAD_DIM
IN_SPLITS = (MLA_Q_RANK, MLA_KV_RANK, MLA_ROPE,
             CONV_WIDTH, CONV_WIDTH, CONV_WIDTH,
             FOX_WIDTH, FOX_WIDTH, FOX_WIDTH, FOX_HEADS,
             N_BRANCH * D_MODEL)
D_IN = sum(IN_SPLITS)

kernel_name = "hybrid_mla_conv_fox_gated_block"


def _split_points():
    return [int(v) for v in np.cumsum(IN_SPLITS)[:-1]]


def rms_norm(x, g):
    xf = x.astype(jnp.float32)
    y = xf * lax.rsqrt(jnp.mean(xf * xf, axis=-1, keepdims=True) + EPS) * g.astype(jnp.float32)
    return y.astype(x.dtype)


def rope_tables(length):
    inv_freq = 1.0 / (ROPE_THETA ** (jnp.arange(0, MLA_ROPE, 2, dtype=jnp.float32) / MLA_ROPE))
    ang = jnp.arange(length, dtype=jnp.float32)[:, None] * inv_freq[None, :]
    return jnp.cos(ang)[:, None, :], jnp.sin(ang)[:, None, :]


def apply_rope(x, cos, sin):
    xf = x.astype(jnp.float32)
    x1, x2 = xf[..., : MLA_ROPE // 2], xf[..., MLA_ROPE // 2:]
    return jnp.concatenate([x1 * cos - x2 * sin, x1 * sin + x2 * cos], axis=-1).astype(x.dtype)


def to_heads(t, n_heads):
    b, l, _ = t.shape
    return t.reshape(b, l, n_heads, -1).transpose(0, 2, 1, 3)


def blocked_causal_attention(q, k, v, scale, decay=None):
    b, h, l, _ = q.shape
    pad = (-l) % BLOCK
    padw = ((0, 0), (0, 0), (pad, 0), (0, 0))
    qp, kp, vp = jnp.pad(q, padw), jnp.pad(k, padw), jnp.pad(v, padw)
    lp = l + pad
    nb = lp // BLOCK
    kpos = jnp.arange(lp)
    key_ok = kpos >= pad
    q_blocks = qp.reshape(b, h, nb, BLOCK, -1).transpose(2, 0, 1, 3, 4)
    dp = None if decay is None else jnp.pad(decay, ((0, 0), (0, 0), (pad, 0)))

    def attend(qi, i, di):
        s = jnp.einsum('bhqd,bhkd->bhqk', qi, kp, preferred_element_type=jnp.float32) * scale
        if di is not None:
            s = s + (di[..., :, None] - dp[:, :, None, :])
        qpos = i * BLOCK + jnp.arange(BLOCK)
        mask = (kpos[None, :] <= qpos[:, None]) & key_ok[None, :]
        p = jax.nn.softmax(jnp.where(mask, s, NEG_INF), axis=-1).astype(vp.dtype)
        return jnp.einsum('bhqk,bhkd->bhqd', p, vp)

    idx = jnp.arange(nb)
    if decay is None:
        out = lax.map(lambda a: attend(a[0], a[1], None), (q_blocks, idx))
    else:
        d_blocks = dp.reshape(b, h, nb, BLOCK).transpose(2, 0, 1, 3)
        out = lax.map(lambda a: attend(a[0], a[1], a[2]), (q_blocks, idx, d_blocks))
    out = out.transpose(1, 2, 0, 3, 4).reshape(b, h, lp, -1)
    return out[:, :, pad:]


def hybrid_mixer(h, w_in, b_forget, g_q_lat, g_kv_lat, w_uq, w_ukv, conv_w, w_branch, w_out, cos, sin):
    b, l, _ = h.shape
    proj = h @ w_in
    (c_q, c_kv, k_pe, conv_b, conv_c, conv_x,
     f_q, f_k, f_v, f_logit, gate_logit) = jnp.split(proj, _split_points(), axis=-1)

    q = (rms_norm(c_q, g_q_lat) @ w_uq).reshape(b, l, MLA_HEADS, MLA_NOPE + MLA_ROPE)
    q_nope, q_pe = q[..., :MLA_NOPE], apply_rope(q[..., MLA_NOPE:], cos, sin)
    kv = (rms_norm(c_kv, g_kv_lat) @ w_ukv).reshape(b, l, MLA_HEADS, MLA_NOPE + MLA_V)
    k_nope, v_a = kv[..., :MLA_NOPE], kv[..., MLA_NOPE:]
    k_pe = apply_rope(k_pe[:, :, None, :], cos, sin)
    q_a = jnp.concatenate([q_nope, q_pe], axis=-1)
    k_a = jnp.concatenate([k_nope, jnp.broadcast_to(k_pe, (b, l, MLA_HEADS, MLA_ROPE))], axis=-1)
    o_a = blocked_causal_attention(q_a.transpose(0, 2, 1, 3), k_a.transpose(0, 2, 1, 3),
                                   v_a.transpose(0, 2, 1, 3), (MLA_NOPE + MLA_ROPE) ** -0.5)
    o_a = o_a.transpose(0, 2, 1, 3).reshape(b, l, MLA_HEADS * MLA_V)

    u = conv_c * conv_x
    u = lax.conv_general_dilated(u, conv_w[:, None, :].astype(u.dtype), window_strides=(1,),
                                 padding=[(CONV_K - 1, 0)], dimension_numbers=('NWC', 'WIO', 'NWC'),
                                 feature_group_count=CONV_WIDTH)
    o_b = conv_b * u

    log_f = jax.nn.log_sigmoid(f_logit.astype(jnp.float32) + b_forget.astype(jnp.float32))
    c = jnp.cumsum(log_f, axis=1).transpose(0, 2, 1)
    o_c = blocked_causal_attention(to_heads(f_q, FOX_HEADS), to_heads(f_k, FOX_HEADS),
                                   to_heads(f_v, FOX_HEADS), FOX_HEAD_DIM ** -0.5, decay=c)
    o_c = o_c.transpose(0, 2, 1, 3).reshape(b, l, FOX_WIDTH)

    o = jnp.stack([o_a, o_b, o_c], axis=2)
    y = jnp.einsum('blnw,nwd->blnd', o, w_branch)
    gates = jax.nn.sigmoid(gate_logit.astype(jnp.float32)).astype(h.dtype).reshape(b, l, N_BRANCH, D_MODEL)
    merged = jnp.sum(gates * y, axis=2)
    return merged @ w_out


def swiglu(h, w_ffn_in, w_ffn_out):
    g, u = jnp.split(h @ w_ffn_in, 2, axis=-1)
    return (jax.nn.silu(g) * u) @ w_ffn_out


def _fwd_setup_inputs(seed: int = 0) -> dict:
    key = jax.random.key(seed)
    ks = jax.random.split(key, 18)
    f32 = jnp.float32

    def dense(k, shape, fan_in):
        return jax.random.normal(k, shape, f32) * fan_in ** -0.5

    def gain(k, shape):
        return 1.0 + 0.05 * jax.random.normal(k, shape, f32)

    return {
        "x": jax.random.normal(ks[0], (BATCH, SEQ, D_MODEL), f32),
        "meta": jax.random.normal(ks[1], (N_META, D_MODEL), f32),
        "w_in": dense(ks[2], (DEPTH, D_MODEL, D_IN), D_MODEL),
        "b_forget": FORGET_BIAS_MEAN + 0.1 * jax.random.normal(ks[3], (DEPTH, FOX_HEADS), f32),
        "g_q_lat": gain(ks[4], (DEPTH, MLA_Q_RANK)),
        "g_kv_lat": gain(ks[5], (DEPTH, MLA_KV_RANK)),
        "w_uq": dense(ks[6], (DEPTH, MLA_Q_RANK, MLA_HEADS * (MLA_NOPE + MLA_ROPE)), MLA_Q_RANK),
        "w_ukv": dense(ks[7], (DEPTH, MLA_KV_RANK, MLA_HEADS * (MLA_NOPE + MLA_V)), MLA_KV_RANK),
        "conv_w": dense(ks[8], (DEPTH, CONV_K, CONV_WIDTH), CONV_K),
        "w_branch": dense(ks[9], (DEPTH, N_BRANCH, BRANCH_WIDTH, D_MODEL), BRANCH_WIDTH),
        "w_out": dense(ks[10], (DEPTH, D_MODEL, D_MODEL), D_MODEL),
        "w_ffn_in": dense(ks[11], (DEPTH, D_MODEL, 2 * D_FF), D_MODEL),
        "w_ffn_out": dense(ks[12], (DEPTH, D_FF, D_MODEL), D_FF),
        "g_mix_pre": gain(ks[13], (DEPTH, D_MODEL)),
        "g_mix_post": gain(ks[14], (DEPTH, D_MODEL)),
        "g_ffn_pre": gain(ks[15], (DEPTH, D_MODEL)),
        "g_ffn_post": gain(ks[16], (DEPTH, D_MODEL)),
    }


def _fwd_reference(x, meta, w_in, b_forget, g_q_lat, g_kv_lat, w_uq, w_ukv, conv_w, w_branch, w_out,
              w_ffn_in, w_ffn_out, g_mix_pre, g_mix_post, g_ffn_pre, g_ffn_post):
    b, s, _ = x.shape
    length = N_META + s
    h = jnp.concatenate([jnp.broadcast_to(meta[None].astype(x.dtype), (b, N_META, D_MODEL)), x], axis=1)
    cos, sin = rope_tables(length)
    for layer in range(DEPTH):
        hn = rms_norm(h, g_mix_pre[layer])
        mix = hybrid_mixer(hn, w_in[layer], b_forget[layer], g_q_lat[layer], g_kv_lat[layer],
                           w_uq[layer], w_ukv[layer], conv_w[layer], w_branch[layer], w_out[layer], cos, sin)
        h = h + rms_norm(mix, g_mix_post[layer])
        hn = rms_norm(h, g_ffn_pre[layer])
        h = h + rms_norm(swiglu(hn, w_ffn_in[layer], w_ffn_out[layer]), g_ffn_post[layer])
    return h[:, N_META:]


import jax as _jax
import jax.numpy as _jnp

TWIN_FORMAT = 'train_step'
FWD_PARAMS = ['x', 'meta', 'w_in', 'b_forget', 'g_q_lat', 'g_kv_lat', 'w_uq', 'w_ukv', 'conv_w', 'w_branch', 'w_out', 'w_ffn_in', 'w_ffn_out', 'g_mix_pre', 'g_mix_post', 'g_ffn_pre', 'g_ffn_post']
TWIN_WEIGHTS = ['meta', 'w_in', 'b_forget', 'g_q_lat', 'g_kv_lat', 'w_uq', 'w_ukv', 'conv_w', 'w_branch', 'w_out', 'w_ffn_in', 'w_ffn_out', 'g_mix_pre', 'g_mix_post', 'g_ffn_pre', 'g_ffn_post']
TWIN_DIFF_INPUT = 'x'
TWIN_INPUTS = ['x', 'meta', 'w_in', 'b_forget', 'g_q_lat', 'g_kv_lat', 'w_uq', 'w_ukv', 'conv_w', 'w_branch', 'w_out', 'w_ffn_in', 'w_ffn_out', 'g_mix_pre', 'g_mix_post', 'g_ffn_pre', 'g_ffn_post', 'loss_target', 'm_meta', 'm_w_in', 'm_b_forget', 'm_g_q_lat', 'm_g_kv_lat', 'm_w_uq', 'm_w_ukv', 'm_conv_w', 'm_w_branch', 'm_w_out', 'm_w_ffn_in', 'm_w_ffn_out', 'm_g_mix_pre', 'm_g_mix_post', 'm_g_ffn_pre', 'm_g_ffn_post', 'v_meta', 'v_w_in', 'v_b_forget', 'v_g_q_lat', 'v_g_kv_lat', 'v_w_uq', 'v_w_ukv', 'v_conv_w', 'v_w_branch', 'v_w_out', 'v_w_ffn_in', 'v_w_ffn_out', 'v_g_mix_pre', 'v_g_mix_post', 'v_g_ffn_pre', 'v_g_ffn_post']
TWIN_OUTPUTS = ['loss', 'grad_x', 'grad_meta', 'grad_w_in', 'grad_b_forget', 'grad_g_q_lat', 'grad_g_kv_lat', 'grad_w_uq', 'grad_w_ukv', 'grad_conv_w', 'grad_w_branch', 'grad_w_out', 'grad_w_ffn_in', 'grad_w_ffn_out', 'grad_g_mix_pre', 'grad_g_mix_post', 'grad_g_ffn_pre', 'grad_g_ffn_post', 'delta_meta', 'delta_w_in', 'delta_b_forget', 'delta_g_q_lat', 'delta_g_kv_lat', 'delta_w_uq', 'delta_w_ukv', 'delta_conv_w', 'delta_w_branch', 'delta_w_out', 'delta_w_ffn_in', 'delta_w_ffn_out', 'delta_g_mix_pre', 'delta_g_mix_post', 'delta_g_ffn_pre', 'delta_g_ffn_post', 'new_m_meta', 'new_m_w_in', 'new_m_b_forget', 'new_m_g_q_lat', 'new_m_g_kv_lat', 'new_m_w_uq', 'new_m_w_ukv', 'new_m_conv_w', 'new_m_w_branch', 'new_m_w_out', 'new_m_w_ffn_in', 'new_m_w_ffn_out', 'new_m_g_mix_pre', 'new_m_g_mix_post', 'new_m_g_ffn_pre', 'new_m_g_ffn_post', 'new_v_meta', 'new_v_w_in', 'new_v_b_forget', 'new_v_g_q_lat', 'new_v_g_kv_lat', 'new_v_w_uq', 'new_v_w_ukv', 'new_v_conv_w', 'new_v_w_branch', 'new_v_w_out', 'new_v_w_ffn_in', 'new_v_w_ffn_out', 'new_v_g_mix_pre', 'new_v_g_mix_post', 'new_v_g_ffn_pre', 'new_v_g_ffn_post']
TWIN_LEAF_KINDS = {'loss': 'loss', 'grad_x': 'grad_x', 'grad_meta': 'grad_w', 'grad_w_in': 'grad_w', 'grad_b_forget': 'grad_w', 'grad_g_q_lat': 'grad_w', 'grad_g_kv_lat': 'grad_w', 'grad_w_uq': 'grad_w', 'grad_w_ukv': 'grad_w', 'grad_conv_w': 'grad_w', 'grad_w_branch': 'grad_w', 'grad_w_out': 'grad_w', 'grad_w_ffn_in': 'grad_w', 'grad_w_ffn_out': 'grad_w', 'grad_g_mix_pre': 'grad_w', 'grad_g_mix_post': 'grad_w', 'grad_g_ffn_pre': 'grad_w', 'grad_g_ffn_post': 'grad_w', 'delta_meta': 'delta_w', 'delta_w_in': 'delta_w', 'delta_b_forget': 'delta_w', 'delta_g_q_lat': 'delta_w', 'delta_g_kv_lat': 'delta_w', 'delta_w_uq': 'delta_w', 'delta_w_ukv': 'delta_w', 'delta_conv_w': 'delta_w', 'delta_w_branch': 'delta_w', 'delta_w_out': 'delta_w', 'delta_w_ffn_in': 'delta_w', 'delta_w_ffn_out': 'delta_w', 'delta_g_mix_pre': 'delta_w', 'delta_g_mix_post': 'delta_w', 'delta_g_ffn_pre': 'delta_w', 'delta_g_ffn_post': 'delta_w', 'new_m_meta': 'new_m', 'new_m_w_in': 'new_m', 'new_m_b_forget': 'new_m', 'new_m_g_q_lat': 'new_m', 'new_m_g_kv_lat': 'new_m', 'new_m_w_uq': 'new_m', 'new_m_w_ukv': 'new_m', 'new_m_conv_w': 'new_m', 'new_m_w_branch': 'new_m', 'new_m_w_out': 'new_m', 'new_m_w_ffn_in': 'new_m', 'new_m_w_ffn_out': 'new_m', 'new_m_g_mix_pre': 'new_m', 'new_m_g_mix_post': 'new_m', 'new_m_g_ffn_pre': 'new_m', 'new_m_g_ffn_post': 'new_m', 'new_v_meta': 'new_v', 'new_v_w_in': 'new_v', 'new_v_b_forget': 'new_v', 'new_v_g_q_lat': 'new_v', 'new_v_g_kv_lat': 'new_v', 'new_v_w_uq': 'new_v', 'new_v_w_ukv': 'new_v', 'new_v_conv_w': 'new_v', 'new_v_w_branch': 'new_v', 'new_v_w_out': 'new_v', 'new_v_w_ffn_in': 'new_v', 'new_v_w_ffn_out': 'new_v', 'new_v_g_mix_pre': 'new_v', 'new_v_g_mix_post': 'new_v', 'new_v_g_ffn_pre': 'new_v', 'new_v_g_ffn_post': 'new_v'}


def _forward(args):
    return _fwd_reference(*[args[k] for k in FWD_PARAMS])


def _output_shape():
    out = _jax.eval_shape(lambda: _forward(_fwd_setup_inputs(0)))
    return out.shape, out.dtype

N_MICROBATCH = 1
ADAM_LR = 0.001
ADAM_B1 = 0.9
ADAM_B2 = 0.999
ADAM_EPS = 1e-08
ADAM_WD = 0.01
ADAM_STEP = 10
PER_EXAMPLE_BATCH_AXIS = {'x': 0, 'loss_target': 0}
SHARED_INPUTS = []
_WEIGHT_DTYPES = {'meta': _jnp.float32, 'w_in': _jnp.float32, 'b_forget': _jnp.float32, 'g_q_lat': _jnp.float32, 'g_kv_lat': _jnp.float32, 'w_uq': _jnp.float32, 'w_ukv': _jnp.float32, 'conv_w': _jnp.float32, 'w_branch': _jnp.float32, 'w_out': _jnp.float32, 'w_ffn_in': _jnp.float32, 'w_ffn_out': _jnp.float32, 'g_mix_pre': _jnp.float32, 'g_mix_post': _jnp.float32, 'g_ffn_pre': _jnp.float32, 'g_ffn_post': _jnp.float32}
MOMENT_SCALE = {'meta': 5.849260e-02, 'w_in': 3.749756e-01, 'b_forget': 2.710178e+00, 'g_q_lat': 1.674747e-01, 'g_kv_lat': 2.510821e-01, 'w_uq': 9.339864e-02, 'w_ukv': 1.231087e-01, 'conv_w': 7.442504e-01, 'w_branch': 3.203442e-01, 'w_out': 5.583004e-01, 'w_ffn_in': 2.545131e-01, 'w_ffn_out': 4.226912e-01, 'g_mix_pre': 9.741247e-01, 'g_mix_post': 7.892376e+00, 'g_ffn_pre': 5.980209e-01, 'g_ffn_post': 7.931708e+00}


def _to_microbatches(a, axis):
    t = _jnp.moveaxis(a, axis, 0)
    t = t.reshape((N_MICROBATCH, t.shape[0] // N_MICROBATCH) + t.shape[1:])
    return _jnp.moveaxis(t, 1, axis + 1)


def setup_inputs(seed: int = 0) -> dict:
    inp = _fwd_setup_inputs(seed)
    key = _jax.random.fold_in(_jax.random.key(seed), 7919)
    shape, _ = _output_shape()
    out = dict(inp)
    out["loss_target"] = _jax.random.normal(_jax.random.fold_in(key, 0), shape, _jnp.float32)
    for i, name in enumerate(TWIN_WEIGHTS):
        w = inp[name].astype(_jnp.float32)
        if MOMENT_SCALE is None:
            s = _jnp.sqrt(_jnp.mean(_jnp.square(w)) + 1e-30)
        else:
            s = MOMENT_SCALE[name]
        km, kv = _jax.random.split(_jax.random.fold_in(key, i + 1))
        out[name] = w
        out["m_" + name] = s * _jax.random.normal(km, w.shape, _jnp.float32)
        out["v_" + name] = (s * s) * _jax.random.uniform(kv, w.shape, _jnp.float32, 0.5, 1.5)
    if N_MICROBATCH > 1:
        for name, axis in PER_EXAMPLE_BATCH_AXIS.items():
            out[name] = _to_microbatches(out[name], axis)
    return {'x': out['x'], 'meta': out['meta'], 'w_in': out['w_in'], 'b_forget': out['b_forget'], 'g_q_lat': out['g_q_lat'], 'g_kv_lat': out['g_kv_lat'], 'w_uq': out['w_uq'], 'w_ukv': out['w_ukv'], 'conv_w': out['conv_w'], 'w_branch': out['w_branch'], 'w_out': out['w_out'], 'w_ffn_in': out['w_ffn_in'], 'w_ffn_out': out['w_ffn_out'], 'g_mix_pre': out['g_mix_pre'], 'g_mix_post': out['g_mix_post'], 'g_ffn_pre': out['g_ffn_pre'], 'g_ffn_post': out['g_ffn_post'], 'loss_target': out['loss_target'], 'm_meta': out['m_meta'], 'm_w_in': out['m_w_in'], 'm_b_forget': out['m_b_forget'], 'm_g_q_lat': out['m_g_q_lat'], 'm_g_kv_lat': out['m_g_kv_lat'], 'm_w_uq': out['m_w_uq'], 'm_w_ukv': out['m_w_ukv'], 'm_conv_w': out['m_conv_w'], 'm_w_branch': out['m_w_branch'], 'm_w_out': out['m_w_out'], 'm_w_ffn_in': out['m_w_ffn_in'], 'm_w_ffn_out': out['m_w_ffn_out'], 'm_g_mix_pre': out['m_g_mix_pre'], 'm_g_mix_post': out['m_g_mix_post'], 'm_g_ffn_pre': out['m_g_ffn_pre'], 'm_g_ffn_post': out['m_g_ffn_post'], 'v_meta': out['v_meta'], 'v_w_in': out['v_w_in'], 'v_b_forget': out['v_b_forget'], 'v_g_q_lat': out['v_g_q_lat'], 'v_g_kv_lat': out['v_g_kv_lat'], 'v_w_uq': out['v_w_uq'], 'v_w_ukv': out['v_w_ukv'], 'v_conv_w': out['v_conv_w'], 'v_w_branch': out['v_w_branch'], 'v_w_out': out['v_w_out'], 'v_w_ffn_in': out['v_w_ffn_in'], 'v_w_ffn_out': out['v_w_ffn_out'], 'v_g_mix_pre': out['v_g_mix_pre'], 'v_g_mix_post': out['v_g_mix_post'], 'v_g_ffn_pre': out['v_g_ffn_pre'], 'v_g_ffn_post': out['v_g_ffn_post']}


def _loss(weights, diff, rest, loss_target):
    with _jax.named_scope("forward"):
        args = {**rest, TWIN_DIFF_INPUT: diff, **{k: w.astype(_WEIGHT_DTYPES[k]) for k, w in weights.items()}}
        y = _forward(args)
    with _jax.named_scope("loss_head"):
        err = _jnp.square(y.astype(_jnp.float32) - loss_target)
        return 0.5 * _jnp.sum(_jnp.mean(err, axis=-1)) if err.ndim else 0.5 * err


def _adamw(w, g, m, v):
    m = ADAM_B1 * m + (1.0 - ADAM_B1) * g
    v = ADAM_B2 * v + (1.0 - ADAM_B2) * _jnp.square(g)
    m_hat = m / (1.0 - ADAM_B1 ** ADAM_STEP)
    v_hat = v / (1.0 - ADAM_B2 ** ADAM_STEP)
    delta = -ADAM_LR * (m_hat / (_jnp.sqrt(v_hat) + ADAM_EPS) + ADAM_WD * w)
    return delta, m, v


def reference(x, meta, w_in, b_forget, g_q_lat, g_kv_lat, w_uq, w_ukv, conv_w, w_branch, w_out, w_ffn_in, w_ffn_out, g_mix_pre, g_mix_post, g_ffn_pre, g_ffn_post, loss_target, m_meta, m_w_in, m_b_forget, m_g_q_lat, m_g_kv_lat, m_w_uq, m_w_ukv, m_conv_w, m_w_branch, m_w_out, m_w_ffn_in, m_w_ffn_out, m_g_mix_pre, m_g_mix_post, m_g_ffn_pre, m_g_ffn_post, v_meta, v_w_in, v_b_forget, v_g_q_lat, v_g_kv_lat, v_w_uq, v_w_ukv, v_conv_w, v_w_branch, v_w_out, v_w_ffn_in, v_w_ffn_out, v_g_mix_pre, v_g_mix_post, v_g_ffn_pre, v_g_ffn_post):
    given = dict(x=x, meta=meta, w_in=w_in, b_forget=b_forget, g_q_lat=g_q_lat, g_kv_lat=g_kv_lat, w_uq=w_uq, w_ukv=w_ukv, conv_w=conv_w, w_branch=w_branch, w_out=w_out, w_ffn_in=w_ffn_in, w_ffn_out=w_ffn_out, g_mix_pre=g_mix_pre, g_mix_post=g_mix_post, g_ffn_pre=g_ffn_pre, g_ffn_post=g_ffn_post, loss_target=loss_target, m_meta=m_meta, m_w_in=m_w_in, m_b_forget=m_b_forget, m_g_q_lat=m_g_q_lat, m_g_kv_lat=m_g_kv_lat, m_w_uq=m_w_uq, m_w_ukv=m_w_ukv, m_conv_w=m_conv_w, m_w_branch=m_w_branch, m_w_out=m_w_out, m_w_ffn_in=m_w_ffn_in, m_w_ffn_out=m_w_ffn_out, m_g_mix_pre=m_g_mix_pre, m_g_mix_post=m_g_mix_post, m_g_ffn_pre=m_g_ffn_pre, m_g_ffn_post=m_g_ffn_post, v_meta=v_meta, v_w_in=v_w_in, v_b_forget=v_b_forget, v_g_q_lat=v_g_q_lat, v_g_kv_lat=v_g_kv_lat, v_w_uq=v_w_uq, v_w_ukv=v_w_ukv, v_conv_w=v_conv_w, v_w_branch=v_w_branch, v_w_out=v_w_out, v_w_ffn_in=v_w_ffn_in, v_w_ffn_out=v_w_ffn_out, v_g_mix_pre=v_g_mix_pre, v_g_mix_post=v_g_mix_post, v_g_ffn_pre=v_g_ffn_pre, v_g_ffn_post=v_g_ffn_post)
    weights = {n: given[n] for n in TWIN_WEIGHTS}
    shared = {n: given[n] for n in SHARED_INPUTS}
    per_example = {n: given[n] for n in ['x']}
    grad_fn = _jax.value_and_grad(_loss, argnums=(0, 1))

    def one_microbatch(ex, loss_target):
        ex = dict(ex)
        diff = ex.pop(TWIN_DIFF_INPUT)
        return grad_fn(weights, diff, {**shared, **ex}, loss_target)

    if N_MICROBATCH == 1:
        loss, (grad_w, grad_x) = one_microbatch(per_example, given["loss_target"])
    else:
        def body(carry, xs):
            loss_sum, grad_sum = carry
            l_k, (gw_k, gx_k) = one_microbatch(xs[0], xs[1])
            with _jax.named_scope("update"):
                return (loss_sum + l_k, _jax.tree.map(_jnp.add, grad_sum, gw_k)), gx_k

        init = (_jnp.zeros((), _jnp.float32), _jax.tree.map(_jnp.zeros_like, weights))
        (loss, grad_w), grad_x = _jax.lax.scan(body, init, (per_example, given["loss_target"]))
    with _jax.named_scope("update"):
        delta_w, new_m, new_v = {}, {}, {}
        for n in TWIN_WEIGHTS:
            delta_w[n], new_m[n], new_v[n] = _adamw(weights[n], grad_w[n], given["m_" + n], given["v_" + n])
    return (loss, grad_x, *[grad_w[n] for n in TWIN_WEIGHTS], *[delta_w[n] for n in TWIN_WEIGHTS],
            *[new_m[n] for n in TWIN_WEIGHTS], *[new_v[n] for n in TWIN_WEIGHTS])
```

```python
import functools

import numpy as np
import jax
import jax.numpy as jnp
from jax import lax
from jax.experimental import pallas as pl
from jax.experimental.pallas import tpu as pltpu

D_MODEL = 2048
SEQ = 2048
DEPTH = 4
Q_RANK = 512
KV_RANK = 512
D_FF = 5632
N_META = 16
HEADS = 8
HD = 128
ROPE = 64
BW = HEADS * HD
EPS = 1e-6
NEG_INF = -1e30
ROPE_THETA = 10000.0
N_DEV = 8
LANE = 128
L_TOK = N_META + SEQ
LP = -(-L_TOK // LANE) * LANE
D_IN = Q_RANK + KV_RANK + ROPE + 6 * BW + HEADS + 3 * D_MODEL
MLA_SCALE = (HD + ROPE) ** -0.5
FOX_SCALE = HD ** -0.5
ADAM_LR, ADAM_B1, ADAM_B2, ADAM_EPS, ADAM_WD, ADAM_STEP = 0.001, 0.9, 0.999, 1e-08, 0.01, 10
VMEM_LIMIT = 48 * 1024 * 1024

F32 = jnp.float32
BF16 = jnp.bfloat16
MESH = pl.DeviceIdType.MESH

OFF_GATE = 0
OFF_CONV = 3 * D_MODEL
OFF_FOX = OFF_CONV + 3 * BW
OFF_CQ = OFF_FOX + 3 * BW
OFF_CKV = OFF_CQ + Q_RANK
OFF_KPE = OFF_CKV + KV_RANK
OFF_KROT = OFF_KPE + LANE
OFF_FL = OFF_KROT + LANE
W_ALL = OFF_FL + 2 * LANE
GW = 512 if D_MODEL % 512 == 0 else 256
TW = 512 if D_FF % 512 == 0 else 256
CW = 128
QG = 3 * LANE


def _pick(n, prefs):
    for p in prefs:
        if n % p == 0:
            return p
    return n


TM_ROW = _pick(LP, (136, 128))
TQ = _pick(LP, (272, 128))


def _cb(off, w):
    assert off % w == 0, (off, w)
    return off // w


def _pcall(body, **kw):
    return pl.pallas_call(body, **kw)


def _params(sem):
    return pltpu.CompilerParams(dimension_semantics=sem, vmem_limit_bytes=VMEM_LIMIT)


def _bf(x):
    return x.astype(BF16)


def _dot(a, b, dims, **kw):
    return lax.dot_general(_bf(a), _bf(b), (dims, ((), ())), preferred_element_type=F32, **kw)


_NN = ((1,), (0,))
_NT = ((1,), (1,))
_TN = ((0,), (0,))


def _mm(a, b, *, ta=False, tb=False, name):
    if ta:
        kd, m = a.shape
    else:
        m, kd = a.shape
    if tb:
        n, kd2 = b.shape
    else:
        kd2, n = b.shape
    assert kd == kd2, (a.shape, b.shape, ta, tb)
    tm = _pick(m, (1088, 1024, 544, 512, 272, 256, 128))
    tn = _pick(n, (1024, 768, 512, 384, 256, 128))
    tk = _pick(kd, (1024, 544, 512, 384, 272, 256, 128))
    nk = kd // tk
    dims = _TN if ta else (_NT if tb else _NN)

    def body(a_ref, b_ref, o_ref, acc_ref):
        k = pl.program_id(2)

        @pl.when(k == 0)
        def _():
            acc_ref[...] = jnp.zeros_like(acc_ref)

        acc_ref[...] += _dot(a_ref[...], b_ref[...], dims)

        @pl.when(k == nk - 1)
        def _():
            o_ref[...] = acc_ref[...]

    a_spec = pl.BlockSpec((tk, tm), lambda i, j, k: (k, i)) if ta else pl.BlockSpec((tm, tk), lambda i, j, k: (i, k))
    b_spec = pl.BlockSpec((tn, tk), lambda i, j, k: (j, k)) if tb else pl.BlockSpec((tk, tn), lambda i, j, k: (k, j))
    return _pcall(
        body, name=name, grid=(m // tm, n // tn, nk), in_specs=[a_spec, b_spec],
        out_specs=pl.BlockSpec((tm, tn), lambda i, j, k: (i, j)),
        out_shape=jax.ShapeDtypeStruct((m, n), F32), scratch_shapes=[pltpu.VMEM((tm, tn), F32)],
        compiler_params=_params(("parallel", "parallel", "arbitrary")))(a, b)


class R:
    def __init__(self, arr, w, cb0=0, cstep=0):
        self.arr, self.w, self.cb0, self.cstep = arr, w, cb0, cstep


class Pm:
    def __init__(self, arr):
        self.arr = arr


class OR:
    def __init__(self, total, w, cstep=0):
        self.total, self.w, self.cstep = total, w, cstep


class OA:
    def __init__(self, shape):
        self.shape = shape


def _rw(fn, ins, outs, *, name, ncol=1, tm=None):
    tm = TM_ROW if tm is None else tm
    nrow = LP // tm
    n_in = len(ins)

    def body(*refs):
        j, i = pl.program_id(0), pl.program_id(1)
        res = fn(*[r[...] for r in refs[:n_in]])
        for o, ref, val in zip(outs, refs[n_in:], res):
            if isinstance(o, OR):
                ref[...] = val.astype(ref.dtype)
            else:
                @pl.when((i == 0) & (j == 0))
                def _(ref=ref):
                    ref[...] = jnp.zeros_like(ref)

                ref[...] += val

    in_specs = []
    for s in ins:
        if isinstance(s, R):
            in_specs.append(pl.BlockSpec((tm, s.w), lambda j, i, s=s: (i, s.cb0 + j * s.cstep)))
        else:
            in_specs.append(pl.BlockSpec(s.arr.shape, lambda j, i, nd=s.arr.ndim: (0,) * nd))
    out_specs, out_shape = [], []
    for o in outs:
        if isinstance(o, OR):
            out_specs.append(pl.BlockSpec((tm, o.w), lambda j, i, o=o: (i, j * o.cstep)))
            out_shape.append(jax.ShapeDtypeStruct((LP, o.total), F32))
        else:
            out_specs.append(pl.BlockSpec(o.shape, lambda j, i: (0, 0)))
            out_shape.append(jax.ShapeDtypeStruct(o.shape, F32))
    return _pcall(body, name=name, grid=(ncol, nrow), in_specs=in_specs, out_specs=out_specs, out_shape=out_shape,
                  compiler_params=_params(("arbitrary", "arbitrary")))(*[s.arr for s in ins])


def _rms(x, g):
    return x * lax.rsqrt(jnp.mean(x * x, axis=-1, keepdims=True) + EPS) * g


def _resid_norm(h, z, ga, gb):
    h2 = h + _rms(z, ga)
    return h2, _rms(h2, gb)


def _resid_norm_bwd(h, z, ga, gb, dh2, dhn2):
    _, vjp = jax.vjp(_resid_norm, h, z, ga, gb)
    return vjp((dh2, dhn2))


def _rms_bwd(h, g, dhn, dh_in):
    _, vjp = jax.vjp(_rms, h, g)
    dh, dg = vjp(dhn)
    return dh + dh_in, dg


def _loss_fn(h2, f, g4, lt, rmask):
    h3 = h2 + _rms(f, g4)
    err = jnp.square(h3 - lt)
    return 0.5 * jnp.sum(jnp.mean(err, axis=-1, keepdims=True) * rmask)


def _loss_bwd(h2, f, g4, lt, rmask):
    val, (dh2, df, dg4) = jax.value_and_grad(_loss_fn, argnums=(0, 1, 2))(h2, f, g4, lt, rmask)
    return dh2, df, dg4, jnp.broadcast_to(val, (1, LANE))


def _mla_prep(cq, ckv, kpe, krot, gq, gkv, cosp, sinp):
    return _rms(cq, gq), _rms(ckv, gkv), kpe * cosp + krot * sinp


def _mla_prep_bwd(cq, ckv, kpe, krot, gq, gkv, cosp, sinp, dcqn, dckvn, dkper):
    _, vjp = jax.vjp(lambda a, b, c, d, e, f: _mla_prep(a, b, c, d, e, f, cosp, sinp), cq, ckv, kpe, krot, gq, gkv)
    return vjp((dcqn, dckvn, dkper))


def _merge(g0, g1, g2, y0, y1, y2):
    return (jax.nn.sigmoid(g0) * y0 + jax.nn.sigmoid(g1) * y1 + jax.nn.sigmoid(g2) * y2,)


def _merge_bwd(g0, g1, g2, y0, y1, y2, dm):
    _, vjp = jax.vjp(_merge, g0, g1, g2, y0, y1, y2)
    return vjp((dm,))


def _swiglu(gu):
    g, u = gu[:, :TW], gu[:, TW:]
    return (g * jax.nn.sigmoid(g) * u,)


def _swiglu_bwd(gu, dact):
    _, vjp = jax.vjp(_swiglu, gu)
    return vjp((dact,))


def _causal_probs(s, iq):
    qpos = iq * TQ + lax.broadcasted_iota(jnp.int32, s.shape, 0)
    kpos = lax.broadcasted_iota(jnp.int32, s.shape, 1)
    s = jnp.where(kpos <= qpos, s, NEG_INF)
    e = jnp.exp(s - jnp.max(s, axis=-1, keepdims=True))
    return e / jnp.sum(e, axis=-1, keepdims=True)


def _softmax_vjp(p, dp):
    return p * (dp - jnp.sum(p * dp, axis=-1, keepdims=True))


def _mla_specs():
    return [pl.BlockSpec((TQ, QG), lambda h, i: (i, h)),
            pl.BlockSpec((LP, 2 * HD), lambda h, i: (0, h)),
            pl.BlockSpec((LP, LANE), lambda h, i: (0, 0)),
            pl.BlockSpec((TQ, LANE), lambda h, i: (i, 0)),
            pl.BlockSpec((TQ, LANE), lambda h, i: (i, 0))]


def _mla_parts(q_ref, kv_ref, cos_ref, sin_ref):
    q, kv = q_ref[...], kv_ref[...]
    qn = q[:, :HD]
    qp = q[:, HD:2 * HD] * cos_ref[...] + q[:, 2 * HD:] * sin_ref[...]
    return qn, qp, kv[:, :HD], kv[:, HD:]


def _mla_fwd(q, kv, kper, cosp, sinp):
    def body(q_ref, kv_ref, kp_ref, cos_ref, sin_ref, o_ref):
        qn, qp, kn, v = _mla_parts(q_ref, kv_ref, cos_ref, sin_ref)
        s = (_dot(qn, kn, _NT) + _dot(qp, kp_ref[...], _NT)) * MLA_SCALE
        o_ref[...] = _dot(_causal_probs(s, pl.program_id(1)), v, _NN)

    return _pcall(body, name="mla_fwd", grid=(HEADS, LP // TQ), in_specs=_mla_specs(),
                  out_specs=pl.BlockSpec((TQ, HD), lambda h, i: (i, h)),
                  out_shape=jax.ShapeDtypeStruct((LP, BW), F32),
                  compiler_params=_params(("parallel", "parallel")))(q, kv, kper, cosp, sinp)


def _mla_bwd(q, kv, kper, cosp, sinp, do):
    def body(q_ref, kv_ref, kp_ref, cos_ref, sin_ref, do_ref, dq_ref, dkv_ref, dkp_ref):
        h, iq = pl.program_id(0), pl.program_id(1)
        qn, qp, kn, v = _mla_parts(q_ref, kv_ref, cos_ref, sin_ref)
        kp, dout = kp_ref[...], do_ref[...]
        p = _causal_probs((_dot(qn, kn, _NT) + _dot(qp, kp, _NT)) * MLA_SCALE, iq)
        ds = _softmax_vjp(p, _dot(dout, v, _NT)) * MLA_SCALE
        dqp = _dot(ds, kp, _NN)
        dq_ref[...] = jnp.concatenate([_dot(ds, kn, _NN), dqp * cos_ref[...], dqp * sin_ref[...]], axis=1)

        @pl.when(iq == 0)
        def _():
            dkv_ref[...] = jnp.zeros_like(dkv_ref)

        dkv_ref[...] += jnp.concatenate([_dot(ds, qn, _TN), _dot(p, dout, _TN)], axis=1)

        @pl.when((iq == 0) & (h == 0))
        def _():
            dkp_ref[...] = jnp.zeros_like(dkp_ref)

        dkp_ref[...] += _dot(ds, qp, _TN)

    return _pcall(
        body, name="mla_bwd", grid=(HEADS, LP // TQ),
        in_specs=_mla_specs() + [pl.BlockSpec((TQ, HD), lambda h, i: (i, h))],
        out_specs=[pl.BlockSpec((TQ, QG), lambda h, i: (i, h)), pl.BlockSpec((LP, 2 * HD), lambda h, i: (0, h)),
                   pl.BlockSpec((LP, LANE), lambda h, i: (0, 0))],
        out_shape=[jax.ShapeDtypeStruct((LP, HEADS * QG), F32), jax.ShapeDtypeStruct((LP, 2 * BW), F32),
                   jax.ShapeDtypeStruct((LP, LANE), F32)],
        compiler_params=_params(("arbitrary", "arbitrary")))(q, kv, kper, cosp, sinp, do)


def _fox_specs():
    cq, ck, cv = _cb(OFF_FOX, HD), _cb(OFF_FOX + BW, HD), _cb(OFF_FOX + 2 * BW, HD)
    return [pl.BlockSpec((TQ, HD), lambda h, i: (i, cq + h)),
            pl.BlockSpec((LP, HD), lambda h, i: (0, ck + h)),
            pl.BlockSpec((LP, HD), lambda h, i: (0, cv + h)),
            pl.BlockSpec((1, TQ, 1), lambda h, i: (h, i, 0)),
            pl.BlockSpec((1, 1, LP), lambda h, i: (h, 0, 0))]


def _fox_probs(q_ref, k_ref, cq_ref, ck_ref, iq):
    s = _dot(q_ref[...], k_ref[...], _NT) * FOX_SCALE + (cq_ref[0] - ck_ref[0])
    return _causal_probs(s, iq)


def _fox_fwd(p_all, cq3, ck3):
    def body(q_ref, k_ref, v_ref, cq_ref, ck_ref, o_ref):
        o_ref[...] = _dot(_fox_probs(q_ref, k_ref, cq_ref, ck_ref, pl.program_id(1)), v_ref[...], _NN)

    return _pcall(body, name="fox_fwd", grid=(HEADS, LP // TQ), in_specs=_fox_specs(),
                  out_specs=pl.BlockSpec((TQ, HD), lambda h, i: (i, h)),
                  out_shape=jax.ShapeDtypeStruct((LP, BW), F32),
                  compiler_params=_params(("parallel", "parallel")))(p_all, p_all, p_all, cq3, ck3)


def _fox_bwd(p_all, cq3, ck3, do):
    def body(q_ref, k_ref, v_ref, cq_ref, ck_ref, do_ref, dq_ref, dk_ref, dv_ref, dcq_ref, dck_ref):
        iq = pl.program_id(1)
        p = _fox_probs(q_ref, k_ref, cq_ref, ck_ref, iq)
        dout = do_ref[...]
        ds = _softmax_vjp(p, _dot(dout, v_ref[...], _NT))
        dss = ds * FOX_SCALE
        dq_ref[...] = _dot(dss, k_ref[...], _NN)
        dcq_ref[0] = jnp.sum(ds, axis=1, keepdims=True)

        @pl.when(iq == 0)
        def _():
            dk_ref[...] = jnp.zeros_like(dk_ref)
            dv_ref[...] = jnp.zeros_like(dv_ref)
            dck_ref[...] = jnp.zeros_like(dck_ref)

        dk_ref[...] += _dot(dss, q_ref[...], _TN)
        dv_ref[...] += _dot(p, dout, _TN)
        dck_ref[0] -= jnp.sum(ds, axis=0, keepdims=True)

    head_rows = pl.BlockSpec((TQ, HD), lambda h, i: (i, h))
    head_all = pl.BlockSpec((LP, HD), lambda h, i: (0, h))
    return _pcall(
        body, name="fox_bwd", grid=(HEADS, LP // TQ), in_specs=_fox_specs() + [head_rows],
        out_specs=[head_rows, head_all, head_all, pl.BlockSpec((1, TQ, 1), lambda h, i: (h, i, 0)),
                   pl.BlockSpec((1, 1, LP), lambda h, i: (h, 0, 0))],
        out_shape=[jax.ShapeDtypeStruct((LP, BW), F32)] * 3
        + [jax.ShapeDtypeStruct((HEADS, LP, 1), F32), jax.ShapeDtypeStruct((HEADS, 1, LP), F32)],
        compiler_params=_params(("arbitrary", "arbitrary")))(p_all, p_all, p_all, cq3, ck3, do)


def _log_sigmoid(x):
    return jnp.minimum(x, 0.0) - jnp.log(1.0 + jnp.exp(-jnp.abs(x)))


def _decay_fwd(p_all, b_pad):
    tc = TM_ROW

    def body(fl_ref, b_ref, c_ref):
        lf = _log_sigmoid(fl_ref[...] + b_ref[...])
        r = pl.program_id(0) * tc + lax.broadcasted_iota(jnp.int32, (tc, LP), 0)
        s = lax.broadcasted_iota(jnp.int32, (tc, LP), 1)
        c_ref[...] = jnp.dot((s <= r).astype(F32), lf, precision=lax.Precision.HIGHEST, preferred_element_type=F32)

    return _pcall(body, name="decay_fwd", grid=(LP // tc,),
                  in_specs=[pl.BlockSpec((LP, LANE), lambda i: (0, _cb(OFF_FL, LANE))),
                            pl.BlockSpec((1, LANE), lambda i: (0, 0))],
                  out_specs=pl.BlockSpec((tc, LANE), lambda i: (i, 0)),
                  out_shape=jax.ShapeDtypeStruct((LP, LANE), F32), compiler_params=_params(("parallel",)))(p_all, b_pad)


def _decay_bwd(p_all, b_pad, dc):
    tc = TM_ROW

    def body(fl_ref, b_ref, dc_ref, dfl_ref, db_ref):
        i = pl.program_id(0)
        r = i * tc + lax.broadcasted_iota(jnp.int32, (tc, LP), 0)
        t = lax.broadcasted_iota(jnp.int32, (tc, LP), 1)
        dlf = jnp.dot((t >= r).astype(F32), dc_ref[...], precision=lax.Precision.HIGHEST, preferred_element_type=F32)
        dfl = dlf * jax.nn.sigmoid(-(fl_ref[...] + b_ref[...]))
        dfl_ref[...] = dfl

        @pl.when(i == 0)
        def _():
            db_ref[...] = jnp.zeros_like(db_ref)

        db_ref[...] += jnp.sum(dfl, axis=0, keepdims=True)

    return _pcall(body, name="decay_bwd", grid=(LP // tc,),
                  in_specs=[pl.BlockSpec((tc, LANE), lambda i: (i, _cb(OFF_FL, LANE))),
                            pl.BlockSpec((1, LANE), lambda i: (0, 0)), pl.BlockSpec((LP, LANE), lambda i: (0, 0))],
                  out_specs=[pl.BlockSpec((tc, LANE), lambda i: (i, 0)), pl.BlockSpec((1, LANE), lambda i: (0, 0))],
                  out_shape=[jax.ShapeDtypeStruct((LP, LANE), F32), jax.ShapeDtypeStruct((1, LANE), F32)],
                  compiler_params=_params(("arbitrary",)))(p_all, b_pad, dc)


def _conv_specs():
    c0 = _cb(OFF_CONV, CW)
    step = BW // CW
    return [pl.BlockSpec((LP, CW), lambda j: (0, c0 + j)), pl.BlockSpec((LP, CW), lambda j: (0, c0 + step + j)),
            pl.BlockSpec((LP, CW), lambda j: (0, c0 + 2 * step + j)), pl.BlockSpec((3, CW), lambda j: (0, j))]


def _shift_down(x, k, t):
    return jnp.where(t >= k, pltpu.roll(x, k, 0), 0.0)


def _shift_up(x, k, t):
    return jnp.where(t < LP - k, pltpu.roll(x, LP - k, 0), 0.0)


def _conv_fwd(p_all, cw):
    def body(b_ref, c_ref, x_ref, w_ref, o_ref):
        t = lax.broadcasted_iota(jnp.int32, (LP, CW), 0)
        uu = c_ref[...] * x_ref[...]
        w = w_ref[...]
        u = w[2:3] * uu + w[1:2] * _shift_down(uu, 1, t) + w[0:1] * _shift_down(uu, 2, t)
        o_ref[...] = b_ref[...] * u

    return _pcall(body, name="conv_fwd", grid=(BW // CW,), in_specs=_conv_specs(),
                  out_specs=pl.BlockSpec((LP, CW), lambda j: (0, j)), out_shape=jax.ShapeDtypeStruct((LP, BW), F32),
                  compiler_params=_params(("parallel",)))(p_all, p_all, p_all, cw)


def _conv_bwd(p_all, cw, do):
    def body(b_ref, c_ref, x_ref, w_ref, do_ref, d_ref, dw_ref):
        t = lax.broadcasted_iota(jnp.int32, (LP, CW), 0)
        cc, xx, w, dout = c_ref[...], x_ref[...], w_ref[...], do_ref[...]
        uu = cc * xx
        s1, s2 = _shift_down(uu, 1, t), _shift_down(uu, 2, t)
        u = w[2:3] * uu + w[1:2] * s1 + w[0:1] * s2
        du = dout * b_ref[...]
        duu = w[2:3] * du + w[1:2] * _shift_up(du, 1, t) + w[0:1] * _shift_up(du, 2, t)
        d_ref[0] = dout * u
        d_ref[1] = duu * xx
        d_ref[2] = duu * cc
        dw_ref[0:1, :] = jnp.sum(du * s2, axis=0, keepdims=True)
        dw_ref[1:2, :] = jnp.sum(du * s1, axis=0, keepdims=True)
        dw_ref[2:3, :] = jnp.sum(du * uu, axis=0, keepdims=True)

    return _pcall(body, name="conv_bwd", grid=(BW // CW,),
                  in_specs=_conv_specs() + [pl.BlockSpec((LP, CW), lambda j: (0, j))],
                  out_specs=[pl.BlockSpec((3, LP, CW), lambda j: (0, 0, j)), pl.BlockSpec((3, CW), lambda j: (0, j))],
                  out_shape=[jax.ShapeDtypeStruct((3, LP, BW), F32), jax.ShapeDtypeStruct((3, BW), F32)],
                  compiler_params=_params(("parallel",)))(p_all, p_all, p_all, cw, do)


def _place():
    return lax.axis_index("x"), lax.axis_index("y"), lax.axis_index("c")


_ANY = pl.BlockSpec(memory_space=pl.ANY)


def _all_gather(arrs, name):
    n = len(arrs)

    def body(*refs):
        ins, outs = refs[:n], refs[n:2 * n]
        send, recv, loc = refs[2 * n:]
        x, y, c = _place()
        sib = (x, y, 1 - c)
        chips = [(1 - x, y), (x, 1 - y), (1 - x, 1 - y)]

        def idx(px, py, pc):
            return 4 * px + 2 * py + pc

        def copy(j, k, block, to, src=None):
            dst = outs[k].at[idx(*block)]
            return pltpu.make_async_remote_copy(src_ref=dst if src is None else src, dst_ref=dst, send_sem=send.at[j, k],
                                                recv_sem=recv.at[j, k], device_id=to, device_id_type=MESH)

        me = (x, y, c)
        mine = [pltpu.make_async_copy(ins[k], outs[k].at[idx(*me)], loc.at[k]) for k in range(n)]
        for cp in mine:
            cp.start()
        first = [copy(0, k, me, sib, src=ins[k]) for k in range(n)]
        first += [copy(1 + j, k, me, (*chip, c), src=ins[k]) for j, chip in enumerate(chips) for k in range(n)]
        for cp in first:
            cp.start()
        passed = []
        for j, chip in enumerate(chips):
            for k in range(n):
                copy(1 + j, k, (*chip, c), me).wait_recv()
                cp = copy(4 + j, k, (*chip, c), sib)
                cp.start()
                passed.append(cp)
        for k in range(n):
            copy(0, k, sib, me).wait_recv()
        for j, chip in enumerate(chips):
            for k in range(n):
                copy(4 + j, k, (*chip, 1 - c), me).wait_recv()
        for cp in first + passed:
            cp.wait_send()
        for cp in mine:
            cp.wait()

    return _pcall(body, name=name, in_specs=[_ANY] * n, out_specs=[_ANY] * n,
                  out_shape=[jax.ShapeDtypeStruct((N_DEV,) + a.shape, a.dtype) for a in arrs],
                  scratch_shapes=[pltpu.SemaphoreType.DMA((7, n)), pltpu.SemaphoreType.DMA((7, n)),
                                  pltpu.SemaphoreType.DMA((n,))])(*arrs)


def _swap_sibling(arrs, name):
    n = len(arrs)

    def body(*refs):
        ins, outs = refs[:n], refs[n:2 * n]
        send, recv = refs[2 * n:]
        x, y, c = _place()
        cps = [pltpu.make_async_remote_copy(src_ref=ins[k].at[1 - c], dst_ref=outs[k], send_sem=send.at[k],
                                            recv_sem=recv.at[k], device_id=(x, y, 1 - c), device_id_type=MESH)
               for k in range(n)]
        for cp in cps:
            cp.start()
        for cp in cps:
            cp.wait()

    return _pcall(body, name=name, in_specs=[_ANY] * n, out_specs=[_ANY] * n,
                  out_shape=[jax.ShapeDtypeStruct(a.shape[1:], a.dtype) for a in arrs],
                  scratch_shapes=[pltpu.SemaphoreType.DMA((n,)), pltpu.SemaphoreType.DMA((n,))])(*arrs)


def _swap_chips(arrs, name):
    n = len(arrs)

    def body(*refs):
        ins, outs = refs[:n], refs[n:2 * n]
        send, recv, loc = refs[2 * n:]
        x, y, c = _place()
        mychip = 2 * x + y
        mine = [pltpu.make_async_copy(ins[k].at[mychip], outs[k].at[mychip], loc.at[k]) for k in range(n)]
        for cp in mine:
            cp.start()
        cps = []
        for j, (cx, cy) in enumerate([(1 - x, y), (x, 1 - y), (1 - x, 1 - y)]):
            for k in range(n):
                cps.append(pltpu.make_async_remote_copy(
                    src_ref=ins[k].at[2 * cx + cy], dst_ref=outs[k].at[mychip], send_sem=send.at[j, k],
                    recv_sem=recv.at[j, k], device_id=(cx, cy, c), device_id_type=MESH))
        for cp in cps:
            cp.start()
        for cp in cps:
            cp.wait()
        for cp in mine:
            cp.wait()

    return _pcall(body, name=name, in_specs=[_ANY] * n, out_specs=[_ANY] * n,
                  out_shape=[jax.ShapeDtypeStruct(a.shape, a.dtype) for a in arrs],
                  scratch_shapes=[pltpu.SemaphoreType.DMA((3, n)), pltpu.SemaphoreType.DMA((3, n)),
                                  pltpu.SemaphoreType.DMA((n,))])(*arrs)


def _rows_tile(rows):
    return _pick(rows, (128, 64, 32, 16))


def _pair_sum(g, r1, core, name):
    _, rows, cols = g.shape
    tr = _rows_tile(rows)

    def body(c_ref, g_ref, r_ref, o_ref):
        o_ref[...] = (g_ref[...].astype(F32) + r_ref[...].astype(F32)).astype(o_ref.dtype)

    return _pcall(
        body, name=name,
        grid_spec=pltpu.PrefetchScalarGridSpec(
            num_scalar_prefetch=1, grid=(rows // tr,),
            in_specs=[pl.BlockSpec((None, tr, cols), lambda i, c: (c[0], i, 0)), pl.BlockSpec((tr, cols), lambda i, c: (i, 0))],
            out_specs=pl.BlockSpec((tr, cols), lambda i, c: (i, 0))),
        out_shape=jax.ShapeDtypeStruct((rows, cols), g.dtype), compiler_params=_params(("parallel",)))(core, g, r1)


def _adamw(g, w, m, v):
    m = ADAM_B1 * m + (1.0 - ADAM_B1) * g
    v = ADAM_B2 * v + (1.0 - ADAM_B2) * jnp.square(g)
    m_hat = m / (1.0 - ADAM_B1 ** ADAM_STEP)
    v_hat = v / (1.0 - ADAM_B2 ** ADAM_STEP)
    return -ADAM_LR * (m_hat / (jnp.sqrt(v_hat) + ADAM_EPS) + ADAM_WD * w), m, v


def _sum_adamw(parts, w, m, v, name):
    npart, rows, cols = parts.shape
    tr = _rows_tile(rows)

    def body(p_ref, w_ref, m_ref, v_ref, g_ref, d_ref, nm_ref, nv_ref):
        g = p_ref[0].astype(F32)
        for k in range(1, npart):
            g = g + p_ref[k].astype(F32)
        g_ref[...] = g
        d_ref[...], nm_ref[...], nv_ref[...] = _adamw(g, w_ref[...], m_ref[...], v_ref[...])

    blk = pl.BlockSpec((tr, cols), lambda i: (i, 0))
    return _pcall(body, name=name, grid=(rows // tr,),
                  in_specs=[pl.BlockSpec((npart, tr, cols), lambda i: (0, i, 0)), blk, blk, blk], out_specs=[blk] * 4,
                  out_shape=[jax.ShapeDtypeStruct((rows, cols), F32)] * 4,
                  compiler_params=_params(("parallel",)))(parts, w, m, v)


def _rot_cols(w):
    return jnp.concatenate([-w[..., ROPE // 2:], w[..., :ROPE // 2]], axis=-1)


def _rot_cols_t(dw):
    return jnp.concatenate([dw[..., ROPE // 2:], -dw[..., :ROPE // 2]], axis=-1)


_IN_SPLITS = np.cumsum([0, Q_RANK, KV_RANK, ROPE, BW, BW, BW, BW, BW, BW, HEADS, 3 * D_MODEL])


def _ext_w_in(w):
    o = _IN_SPLITS
    kpe = w[:, o[2]:o[3]]
    z = lambda n: jnp.zeros((w.shape[0], n), w.dtype)
    return jnp.concatenate([w[:, o[10]:o[11]], w[:, o[3]:o[9]], w[:, o[0]:o[2]], kpe, z(LANE - ROPE), _rot_cols(kpe),
                            z(LANE - ROPE), w[:, o[9]:o[10]], z(2 * LANE - HEADS)], axis=1)


def _unext_w_in(dw):
    kpe = dw[:, OFF_KPE:OFF_KPE + ROPE] + _rot_cols_t(dw[:, OFF_KROT:OFF_KROT + ROPE])
    return jnp.concatenate([dw[:, OFF_CQ:OFF_KPE], kpe, dw[:, OFF_CONV:OFF_CQ], dw[:, OFF_FL:OFF_FL + HEADS],
                            dw[:, OFF_GATE:OFF_CONV]], axis=1)


def _ext_w_uq(w):
    w3 = w.reshape(Q_RANK, HEADS, HD + ROPE)
    pe = w3[..., HD:]
    z = jnp.zeros((Q_RANK, HEADS, LANE - ROPE), w.dtype)
    return jnp.concatenate([w3[..., :HD], pe, z, _rot_cols(pe), z], axis=-1).reshape(Q_RANK, HEADS * QG)


def _unext_w_uq(dw):
    d3 = dw.reshape(Q_RANK, HEADS, QG)
    pe = d3[..., HD:HD + ROPE] + _rot_cols_t(d3[..., 2 * HD:2 * HD + ROPE])
    return jnp.concatenate([d3[..., :HD], pe], axis=-1).reshape(Q_RANK, HEADS * (HD + ROPE))


def _perm_ffn_in(w):
    nb = D_FF // TW
    return jnp.stack([w[:, :D_FF].reshape(-1, nb, TW), w[:, D_FF:].reshape(-1, nb, TW)], axis=2).reshape(-1, 2 * D_FF)


def _unperm_ffn_in(dw):
    d4 = dw.reshape(-1, D_FF // TW, 2, TW)
    return jnp.concatenate([d4[:, :, 0].reshape(-1, D_FF), d4[:, :, 1].reshape(-1, D_FF)], axis=1)


_BIG = (("meta", 1, F32), ("w_in", 2, BF16), ("w_uq", 2, BF16), ("w_ukv", 2, BF16), ("conv_w", 2, F32),
        ("w_branch", 3, BF16), ("w_out", 1, BF16), ("w_ffn_in", 2, BF16), ("w_ffn_out", 1, BF16))
_SMALL = ("b_forget", "g_q_lat", "g_kv_lat", "g_mix_pre", "g_mix_post", "g_ffn_pre", "g_ffn_post")
_ORDER = ("meta", "w_in", "b_forget", "g_q_lat", "g_kv_lat", "w_uq", "w_ukv", "conv_w", "w_branch", "w_out",
          "w_ffn_in", "w_ffn_out", "g_mix_pre", "g_mix_post", "g_ffn_pre", "g_ffn_post")


def _unshard(g, axis):
    g = jnp.moveaxis(g, 0, axis)
    return g.reshape(g.shape[:axis] + (g.shape[axis] * g.shape[axis + 1],) + g.shape[axis + 2:])


def _by_dest(full, axis, dtype):
    s = full.shape
    g = full.reshape(s[:axis] + (N_DEV // 2, 2, s[axis] // N_DEV) + s[axis + 1:])
    g = jnp.moveaxis(g, (axis, axis + 1), (1, 0))
    return g.astype(dtype)


def _as2d(a, lead=0):
    return a.reshape(a.shape[:lead] + (-1, a.shape[-1]))


def kernel(x, meta, w_in, b_forget, g_q_lat, g_kv_lat, w_uq, w_ukv, conv_w, w_branch, w_out, w_ffn_in, w_ffn_out, g_mix_pre, g_mix_post, g_ffn_pre, g_ffn_post, loss_target, m_meta, m_w_in, m_b_forget, m_g_q_lat, m_g_kv_lat, m_w_uq, m_w_ukv, m_conv_w, m_w_branch, m_w_out, m_w_ffn_in, m_w_ffn_out, m_g_mix_pre, m_g_mix_post, m_g_ffn_pre, m_g_ffn_post, v_meta, v_w_in, v_b_forget, v_g_q_lat, v_g_kv_lat, v_w_uq, v_w_ukv, v_conv_w, v_w_branch, v_w_out, v_w_ffn_in, v_w_ffn_out, v_g_mix_pre, v_g_mix_post, v_g_ffn_pre, v_g_ffn_post):
    given = dict(locals())
    core = lax.axis_index("c").astype(jnp.int32).reshape(1)

    gathered = _all_gather([given[n].astype(dt) for n, _, dt in _BIG], "gather_weights")
    full = {n: _unshard(g, ax) for (n, ax, _), g in zip(_BIG, gathered)}

    pos = jnp.arange(LP, dtype=F32)[:, None]
    inv_freq = 1.0 / (ROPE_THETA ** (jnp.arange(0, ROPE, 2, dtype=F32) / ROPE))
    ang = pos * inv_freq[None, :]
    zpad = jnp.zeros((LP, LANE - ROPE), F32)
    cosp = jnp.concatenate([jnp.cos(ang), jnp.cos(ang), zpad], axis=1)
    sinp = jnp.concatenate([jnp.sin(ang), jnp.sin(ang), zpad], axis=1)

    tail = jnp.zeros((LP - L_TOK, D_MODEL), F32)
    h = jnp.concatenate([full["meta"], x[0], tail], axis=0)
    ltp = jnp.concatenate([jnp.zeros((N_META, D_MODEL), F32), loss_target[0], tail], axis=0)
    row = jnp.arange(LP)[:, None]
    rmask = ((row >= N_META) & (row < L_TOK)).astype(F32)

    def vec(a, l):
        return a[l][None, :]

    wl = []
    for l in range(DEPTH):
        wl.append(dict(
            w_all=_ext_w_in(full["w_in"][l]), w_uq=_ext_w_uq(full["w_uq"][l]), w_ukv=full["w_ukv"][l],
            conv_w=full["conv_w"][l], w_branch=full["w_branch"][l], w_out=full["w_out"][l],
            w_ffn_in=_perm_ffn_in(full["w_ffn_in"][l]), w_ffn_out=full["w_ffn_out"][l],
            b_pad=jnp.concatenate([b_forget[l], jnp.zeros((LANE - HEADS,), F32)])[None, :],
            gq=vec(g_q_lat, l), gkv=vec(g_kv_lat, l), g1=vec(g_mix_pre, l), g2=vec(g_mix_post, l),
            g3=vec(g_ffn_pre, l), g4=vec(g_ffn_post, l)))

    def prep_ins(p_all, w):
        return [R(p_all, Q_RANK, _cb(OFF_CQ, Q_RANK)), R(p_all, KV_RANK, _cb(OFF_CKV, KV_RANK)),
                R(p_all, LANE, _cb(OFF_KPE, LANE)), R(p_all, LANE, _cb(OFF_KROT, LANE)), Pm(w["gq"]), Pm(w["gkv"]),
                R(cosp, LANE), R(sinp, LANE)]

    def merge_ins(p_all, ys):
        return [R(p_all, GW, _cb(OFF_GATE + n * D_MODEL, GW), 1) for n in range(3)] + [R(yv, GW, 0, 1) for yv in ys]

    (hn,) = _rw(lambda a, g: (_rms(a, g),), [R(h, D_MODEL), Pm(wl[0]["g1"])], [OR(D_MODEL, D_MODEL)], name="rms_in")
    saved = []
    for l in range(DEPTH):
        w = wl[l]
        s = dict(h=h, hn=hn)
        p_all = _mm(hn, w["w_all"], name="proj_in")
        cqn, ckvn, kper = _rw(_mla_prep, prep_ins(p_all, w), [OR(Q_RANK, Q_RANK), OR(KV_RANK, KV_RANK), OR(LANE, LANE)],
                              name="mla_prep")
        q = _mm(cqn, w["w_uq"], name="proj_q")
        kv = _mm(ckvn, w["w_ukv"], name="proj_kv")
        o_a = _mla_fwd(q, kv, kper, cosp, sinp)
        o_b = _conv_fwd(p_all, w["conv_w"])
        cdec = _decay_fwd(p_all, w["b_pad"])
        cq3 = cdec[:, :HEADS].T[:, :, None]
        ck3 = cdec[:, :HEADS].T[:, None, :]
        o_c = _fox_fwd(p_all, cq3, ck3)
        outs = (o_a, o_b, o_c)
        ys = [_mm(outs[n], w["w_branch"][n], name="proj_branch") for n in range(3)]
        (merged,) = _rw(_merge, merge_ins(p_all, ys), [OR(D_MODEL, GW, 1)], ncol=D_MODEL // GW, name="merge")
        mix = _mm(merged, w["w_out"], name="proj_out")
        h2, hn2 = _rw(_resid_norm, [R(h, D_MODEL), R(mix, D_MODEL), Pm(w["g2"]), Pm(w["g3"])],
                      [OR(D_MODEL, D_MODEL)] * 2, name="resid_norm")
        gu = _mm(hn2, w["w_ffn_in"], name="ffn_in")
        (act,) = _rw(_swiglu, [R(gu, 2 * TW, 0, 1)], [OR(D_FF, TW, 1)], ncol=D_FF // TW, name="swiglu")
        f = _mm(act, w["w_ffn_out"], name="ffn_out")
        s.update(p_all=p_all, cqn=cqn, ckvn=ckvn, kper=kper, q=q, kv=kv, outs=outs, cq3=cq3, ck3=ck3, ys=ys,
                 merged=merged, mix=mix, h2=h2, hn2=hn2, gu=gu, act=act, f=f)
        saved.append(s)
        if l + 1 < DEPTH:
            h, hn = _rw(_resid_norm, [R(h2, D_MODEL), R(f, D_MODEL), Pm(w["g4"]), Pm(wl[l + 1]["g1"])],
                        [OR(D_MODEL, D_MODEL)] * 2, name="resid_norm")

    grads = {n: [None] * DEPTH for n in _ORDER if n != "meta"}
    s, w = saved[-1], wl[-1]
    dh2, df, dg4, loss_acc = _rw(
        _loss_bwd, [R(s["h2"], D_MODEL), R(s["f"], D_MODEL), Pm(w["g4"]), R(ltp, D_MODEL), R(rmask, 1)],
        [OR(D_MODEL, D_MODEL), OR(D_MODEL, D_MODEL), OA((1, D_MODEL)), OA((1, LANE))], name="loss_bwd")
    loss = lax.psum(loss_acc[0, 0], ("x", "y", "c"))
    grads["g_ffn_post"][DEPTH - 1] = dg4[0]
    for l in reversed(range(DEPTH)):
        s, w = saved[l], wl[l]
        p_all = s["p_all"]
        dact = _mm(df, w["w_ffn_out"], tb=True, name="d_act")
        grads["w_ffn_out"][l] = _mm(s["act"], df, ta=True, name="dw_ffn_out")
        (dgu,) = _rw(_swiglu_bwd, [R(s["gu"], 2 * TW, 0, 1), R(dact, TW, 0, 1)], [OR(2 * D_FF, 2 * TW, 1)],
                     ncol=D_FF // TW, name="swiglu_bwd")
        dhn2 = _mm(dgu, w["w_ffn_in"], tb=True, name="d_hn2")
        grads["w_ffn_in"][l] = _unperm_ffn_in(_mm(s["hn2"], dgu, ta=True, name="dw_ffn_in"))
        dh, dmix, dg2, dg3 = _rw(
            _resid_norm_bwd, [R(s["h"], D_MODEL), R(s["mix"], D_MODEL), Pm(w["g2"]), Pm(w["g3"]), R(dh2, D_MODEL),
                              R(dhn2, D_MODEL)],
            [OR(D_MODEL, D_MODEL), OR(D_MODEL, D_MODEL), OA((1, D_MODEL)), OA((1, D_MODEL))], name="resid_norm_bwd")
        grads["g_mix_post"][l], grads["g_ffn_pre"][l] = dg2[0], dg3[0]
        dmerged = _mm(dmix, w["w_out"], tb=True, name="d_merged")
        grads["w_out"][l] = _mm(s["merged"], dmix, ta=True, name="dw_out")
        mb = _rw(_merge_bwd, merge_ins(p_all, s["ys"]) + [R(dmerged, GW, 0, 1)], [OR(D_MODEL, GW, 1)] * 6,
                 ncol=D_MODEL // GW, name="merge_bwd")
        dgate, dys = mb[:3], mb[3:]
        dos = [_mm(dys[n], w["w_branch"][n], tb=True, name="d_branch") for n in range(3)]
        grads["w_branch"][l] = jnp.stack([_mm(s["outs"][n], dys[n], ta=True, name="dw_branch") for n in range(3)])
        dfq, dfk, dfv, dcq3, dck3 = _fox_bwd(p_all, s["cq3"], s["ck3"], dos[2])
        dc = jnp.concatenate([dcq3[:, :, 0].T + dck3[:, 0, :].T, jnp.zeros((LP, LANE - HEADS), F32)], axis=1)
        dfl, db = _decay_bwd(p_all, w["b_pad"], dc)
        grads["b_forget"][l] = db[0, :HEADS]
        dconv, dcw = _conv_bwd(p_all, w["conv_w"], dos[1])
        grads["conv_w"][l] = dcw
        dq, dkv, dkper = _mla_bwd(s["q"], s["kv"], s["kper"], cosp, sinp, dos[0])
        dcqn = _mm(dq, w["w_uq"], tb=True, name="d_cqn")
        grads["w_uq"][l] = _unext_w_uq(_mm(s["cqn"], dq, ta=True, name="dw_uq"))
        dckvn = _mm(dkv, w["w_ukv"], tb=True, name="d_ckvn")
        grads["w_ukv"][l] = _mm(s["ckvn"], dkv, ta=True, name="dw_ukv")
        dcq, dckv, dkpe, dkrot, dgq, dgkv = _rw(
            _mla_prep_bwd, prep_ins(p_all, w) + [R(dcqn, Q_RANK), R(dckvn, KV_RANK), R(dkper, LANE)],
            [OR(Q_RANK, Q_RANK), OR(KV_RANK, KV_RANK), OR(LANE, LANE), OR(LANE, LANE), OA((1, Q_RANK)),
             OA((1, KV_RANK))], name="mla_prep_bwd")
        grads["g_q_lat"][l], grads["g_kv_lat"][l] = dgq[0], dgkv[0]
        dp = jnp.concatenate([*dgate, dconv[0], dconv[1], dconv[2], dfq, dfk, dfv, dcq, dckv, dkpe, dkrot, dfl,
                              jnp.zeros((LP, LANE), F32)], axis=1)
        dhn = _mm(dp, w["w_all"], tb=True, name="d_hn")
        grads["w_in"][l] = _unext_w_in(_mm(s["hn"], dp, ta=True, name="dw_in"))
        if l > 0:
            sp, wp = saved[l - 1], wl[l - 1]
            dh2, df, dg4, dg1 = _rw(
                _resid_norm_bwd, [R(sp["h2"], D_MODEL), R(sp["f"], D_MODEL), Pm(wp["g4"]), Pm(w["g1"]),
                                  R(dh, D_MODEL), R(dhn, D_MODEL)],
                [OR(D_MODEL, D_MODEL), OR(D_MODEL, D_MODEL), OA((1, D_MODEL)), OA((1, D_MODEL))], name="resid_norm_bwd")
            grads["g_ffn_post"][l - 1], grads["g_mix_pre"][l] = dg4[0], dg1[0]
        else:
            dh0, dg1 = _rw(_rms_bwd, [R(s["h"], D_MODEL), Pm(w["g1"]), R(dhn, D_MODEL), R(dh, D_MODEL)],
                           [OR(D_MODEL, D_MODEL), OA((1, D_MODEL))], name="rms_in_bwd")
            grads["g_mix_pre"][0] = dg1[0]
    grad_x = dh0[N_META:L_TOK][None]
    gfull = {n: jnp.stack(grads[n]) for n in grads}
    gfull["meta"] = dh0[:N_META]

    by_dest = [_as2d(_by_dest(gfull[n], ax, dt), 2) for n, ax, dt in _BIG]
    from_sib = _swap_sibling(by_dest, "scatter_sibling")
    chip_sums = []
    for (n, _, _), g, r in zip(_BIG, by_dest, from_sib):
        rows, cols = g.shape[2], g.shape[3]
        ps = _pair_sum(g.reshape(2, 4 * rows, cols), r.reshape(4 * rows, cols), core, "pair_sum_" + n)
        chip_sums.append(ps.reshape(4, rows, cols))
    parts = _swap_chips(chip_sums, "scatter_chips")
    out = {}
    for (n, _, _), p in zip(_BIG, parts):
        shape = given[n].shape
        res = _sum_adamw(p, _as2d(given[n]), _as2d(given["m_" + n]), _as2d(given["v_" + n]), "adamw_" + n)
        out[n] = [r.reshape(shape) for r in res]

    def pack(d):
        flat = jnp.concatenate([d[n].reshape(-1) for n in _SMALL])
        return jnp.concatenate([flat, jnp.zeros((-flat.shape[0]) % (8 * LANE), F32)]).reshape(-1, LANE)

    (small_parts,) = _all_gather([pack(gfull)], "gather_small_grads")
    res = _sum_adamw(small_parts, pack(given), pack({n: given["m_" + n] for n in _SMALL}),
                     pack({n: given["v_" + n] for n in _SMALL}), "adamw_small")
    off = 0
    for n in _SMALL:
        size = int(np.prod(given[n].shape))
        out[n] = [r.reshape(-1)[off:off + size].reshape(given[n].shape) for r in res]
        off += size

    return (loss, grad_x, *[out[n][0] for n in _ORDER], *[out[n][1] for n in _ORDER], *[out[n][2] for n in _ORDER],
            *[out[n][3] for n in _ORDER])
```

```python
import functools

import numpy as np
import jax
import jax.numpy as jnp
from jax import lax
from jax.experimental import pallas as pl
from jax.experimental.pallas import tpu as pltpu

D_MODEL = 2048
SEQ = 2048
DEPTH = 4
Q_RANK = 512
KV_RANK = 512
D_FF = 5632
N_META = 16
HEADS = 8
HD = 128
ROPE = 64
BW = HEADS * HD
EPS = 1e-6
NEG_INF = -1e30
ROPE_THETA = 10000.0
N_DEV = 8
LANE = 128
L_TOK = N_META + SEQ
LP = -(-L_TOK // LANE) * LANE
D_IN = Q_RANK + KV_RANK + ROPE + 6 * BW + HEADS + 3 * D_MODEL
MLA_SCALE = (HD + ROPE) ** -0.5
FOX_SCALE = HD ** -0.5
ADAM_LR, ADAM_B1, ADAM_B2, ADAM_EPS, ADAM_WD, ADAM_STEP = 0.001, 0.9, 0.999, 1e-08, 0.01, 10
VMEM_LIMIT = 48 * 1024 * 1024

F32 = jnp.float32
BF16 = jnp.bfloat16
MESH = pl.DeviceIdType.MESH

OFF_GATE = 0
OFF_CONV = 3 * D_MODEL
OFF_FOX = OFF_CONV + 3 * BW
OFF_CQ = OFF_FOX + 3 * BW
OFF_CKV = OFF_CQ + Q_RANK
OFF_KPE = OFF_CKV + KV_RANK
OFF_KROT = OFF_KPE + LANE
OFF_FL = OFF_KROT + LANE
W_ALL = OFF_FL + 2 * LANE
GW = 512 if D_MODEL % 512 == 0 else 256
TM_FF = 64
CW = 128
QG = 3 * LANE


def _pick(n, prefs):
    for p in prefs:
        if n % p == 0:
            return p
    return n


TM_ROW = _pick(LP, (136, 128))
TQ = _pick(LP, (272, 128))


def _cb(off, w):
    assert off % w == 0, (off, w)
    return off // w


def _pcall(body, **kw):
    return pl.pallas_call(body, **kw)


def _params(sem):
    return pltpu.CompilerParams(dimension_semantics=sem, vmem_limit_bytes=VMEM_LIMIT)


def _bf(x):
    return x.astype(BF16)


def _dot(a, b, dims, **kw):
    return lax.dot_general(_bf(a), _bf(b), (dims, ((), ())), preferred_element_type=F32, **kw)


_NN = ((1,), (0,))
_NT = ((1,), (1,))
_TN = ((0,), (0,))
_ANY = pl.BlockSpec(memory_space=pl.ANY)


def _mm(a, b, *, ta=False, tb=False, bidx=(), stack=None, name):
    if ta:
        kd, m = a.shape
    else:
        m, kd = a.shape
    nlead = len(bidx)
    if tb:
        n, kd2 = b.shape[nlead:]
    else:
        kd2, n = b.shape[nlead:]
    assert kd == kd2, (a.shape, b.shape, ta, tb)
    tm = _pick(m, (1088, 1024, 544, 512, 272, 256, 128))
    tn = _pick(n, (1024, 768, 512, 384, 256, 128))
    tk = _pick(kd, (1024, 544, 512, 384, 272, 256, 128))
    nk = kd // tk
    dims = _TN if ta else (_NT if tb else _NN)

    def body(a_ref, b_ref, *rest):
        o_ref, acc_ref = rest[-2:]
        k = pl.program_id(2)

        @pl.when(k == 0)
        def _():
            acc_ref[...] = jnp.zeros_like(acc_ref)

        acc_ref[...] += _dot(a_ref[...], b_ref[...], dims)

        @pl.when(k == nk - 1)
        def _():
            o_ref[...] = acc_ref[...].astype(o_ref.dtype)

    lead = (None,) * nlead
    a_spec = pl.BlockSpec((tk, tm), lambda i, j, k: (k, i)) if ta else pl.BlockSpec((tm, tk), lambda i, j, k: (i, k))
    if tb:
        b_spec = pl.BlockSpec(lead + (tn, tk), lambda i, j, k: bidx + (j, k))
    else:
        b_spec = pl.BlockSpec(lead + (tk, tn), lambda i, j, k: bidx + (k, j))
    in_specs, args, extra = [a_spec, b_spec], [a, b], {}
    if stack is None:
        out_spec = pl.BlockSpec((tm, tn), lambda i, j, k: (i, j))
        out_shape = jax.ShapeDtypeStruct((m, n), F32)
    else:
        buf, sidx = stack
        assert buf.shape[len(sidx):] == (m, n), (buf.shape, sidx, m, n)
        in_specs.append(_ANY)
        args.append(buf)
        extra = dict(input_output_aliases={2: 0})
        out_spec = pl.BlockSpec((None,) * len(sidx) + (tm, tn), lambda i, j, k: sidx + (i, j))
        out_shape = jax.ShapeDtypeStruct(buf.shape, buf.dtype)
    return _pcall(
        body, name=name, grid=(m // tm, n // tn, nk), in_specs=in_specs, out_specs=out_spec, out_shape=out_shape,
        scratch_shapes=[pltpu.VMEM((tm, tn), F32)],
        compiler_params=_params(("parallel", "parallel", "arbitrary")), **extra)(*args)


class R:
    def __init__(self, arr, w, cb0=0, cstep=0):
        self.arr, self.w, self.cb0, self.cstep = arr, w, cb0, cstep


class Pm:
    def __init__(self, arr):
        self.arr = arr


class OR:
    def __init__(self, total, w, cstep=0):
        self.total, self.w, self.cstep = total, w, cstep


class OA:
    def __init__(self, shape):
        self.shape = shape


def _rw(fn, ins, outs, *, name, ncol=1, tm=None):
    tm = TM_ROW if tm is None else tm
    nrow = LP // tm
    n_in = len(ins)

    def body(*refs):
        j, i = pl.program_id(0), pl.program_id(1)
        res = fn(*[r[...] for r in refs[:n_in]])
        for o, ref, val in zip(outs, refs[n_in:], res):
            if isinstance(o, OR):
                ref[...] = val.astype(ref.dtype)
            else:
                @pl.when((i == 0) & (j == 0))
                def _(ref=ref):
                    ref[...] = jnp.zeros_like(ref)

                ref[...] += val

    in_specs = []
    for s in ins:
        if isinstance(s, R):
            in_specs.append(pl.BlockSpec((tm, s.w), lambda j, i, s=s: (i, s.cb0 + j * s.cstep)))
        else:
            in_specs.append(pl.BlockSpec(s.arr.shape, lambda j, i, nd=s.arr.ndim: (0,) * nd))
    out_specs, out_shape = [], []
    for o in outs:
        if isinstance(o, OR):
            out_specs.append(pl.BlockSpec((tm, o.w), lambda j, i, o=o: (i, j * o.cstep)))
            out_shape.append(jax.ShapeDtypeStruct((LP, o.total), F32))
        else:
            out_specs.append(pl.BlockSpec(o.shape, lambda j, i: (0, 0)))
            out_shape.append(jax.ShapeDtypeStruct(o.shape, F32))
    return _pcall(body, name=name, grid=(ncol, nrow), in_specs=in_specs, out_specs=out_specs, out_shape=out_shape,
                  compiler_params=_params(("arbitrary", "arbitrary")))(*[s.arr for s in ins])


def _rms(x, g):
    return x * lax.rsqrt(jnp.mean(x * x, axis=-1, keepdims=True) + EPS) * g


def _resid_norm(h, z, ga, gb):
    h2 = h + _rms(z, ga)
    return h2, _rms(h2, gb)


def _resid_norm_bwd(h, z, ga, gb, dh2, dhn2):
    _, vjp = jax.vjp(_resid_norm, h, z, ga, gb)
    return vjp((dh2, dhn2))


def _rms_bwd(h, g, dhn, dh_in):
    _, vjp = jax.vjp(_rms, h, g)
    dh, dg = vjp(dhn)
    return dh + dh_in, dg


def _loss_fn(h2, f, g4, lt, rmask):
    h3 = h2 + _rms(f, g4)
    err = jnp.square(h3 - lt)
    return 0.5 * jnp.sum(jnp.mean(err, axis=-1, keepdims=True) * rmask)


def _loss_bwd(h2, f, g4, lt, rmask):
    val, (dh2, df, dg4) = jax.value_and_grad(_loss_fn, argnums=(0, 1, 2))(h2, f, g4, lt, rmask)
    return dh2, df, dg4, jnp.broadcast_to(val, (1, LANE))


def _mla_prep(cq, ckv, kpe, krot, gq, gkv, cosp, sinp):
    return _rms(cq, gq), _rms(ckv, gkv), kpe * cosp + krot * sinp


def _mla_prep_bwd(cq, ckv, kpe, krot, gq, gkv, cosp, sinp, dcqn, dckvn, dkper):
    _, vjp = jax.vjp(lambda a, b, c, d, e, f: _mla_prep(a, b, c, d, e, f, cosp, sinp), cq, ckv, kpe, krot, gq, gkv)
    return vjp((dcqn, dckvn, dkper))


def _merge(g0, g1, g2, y0, y1, y2):
    return (jax.nn.sigmoid(g0) * y0 + jax.nn.sigmoid(g1) * y1 + jax.nn.sigmoid(g2) * y2,)


def _merge_bwd(g0, g1, g2, y0, y1, y2, dm):
    _, vjp = jax.vjp(_merge, g0, g1, g2, y0, y1, y2)
    return vjp((dm,))


def _swiglu(gu):
    g, u = gu[:, :D_FF], gu[:, D_FF:]
    return (g * jax.nn.sigmoid(g) * u,)


def _swiglu_bwd(gu, dact):
    _, vjp = jax.vjp(_swiglu, gu)
    return vjp((dact,))


def _causal_probs(s, iq):
    qpos = iq * TQ + lax.broadcasted_iota(jnp.int32, s.shape, 0)
    kpos = lax.broadcasted_iota(jnp.int32, s.shape, 1)
    s = jnp.where(kpos <= qpos, s, NEG_INF)
    e = jnp.exp(s - jnp.max(s, axis=-1, keepdims=True))
    return e / jnp.sum(e, axis=-1, keepdims=True)


def _softmax_vjp(p, dp):
    return p * (dp - jnp.sum(p * dp, axis=-1, keepdims=True))


def _mla_specs():
    return [pl.BlockSpec((TQ, QG), lambda h, i: (i, h)),
            pl.BlockSpec((LP, 2 * HD), lambda h, i: (0, h)),
            pl.BlockSpec((LP, LANE), lambda h, i: (0, 0)),
            pl.BlockSpec((TQ, LANE), lambda h, i: (i, 0)),
            pl.BlockSpec((TQ, LANE), lambda h, i: (i, 0))]


def _mla_parts(q_ref, kv_ref, cos_ref, sin_ref):
    q, kv = q_ref[...], kv_ref[...]
    qn = q[:, :HD]
    qp = q[:, HD:2 * HD] * cos_ref[...] + q[:, 2 * HD:] * sin_ref[...]
    return qn, qp, kv[:, :HD], kv[:, HD:]


def _mla_fwd(q, kv, kper, cosp, sinp):
    def body(q_ref, kv_ref, kp_ref, cos_ref, sin_ref, o_ref):
        qn, qp, kn, v = _mla_parts(q_ref, kv_ref, cos_ref, sin_ref)
        s = (_dot(qn, kn, _NT) + _dot(qp, kp_ref[...], _NT)) * MLA_SCALE
        o_ref[...] = _dot(_causal_probs(s, pl.program_id(1)), v, _NN)

    return _pcall(body, name="mla_fwd", grid=(HEADS, LP // TQ), in_specs=_mla_specs(),
                  out_specs=pl.BlockSpec((TQ, HD), lambda h, i: (i, h)),
                  out_shape=jax.ShapeDtypeStruct((LP, BW), F32),
                  compiler_params=_params(("parallel", "parallel")))(q, kv, kper, cosp, sinp)


def _mla_bwd(q, kv, kper, cosp, sinp, do):
    def body(q_ref, kv_ref, kp_ref, cos_ref, sin_ref, do_ref, dq_ref, dkv_ref, dkp_ref):
        h, iq = pl.program_id(0), pl.program_id(1)
        qn, qp, kn, v = _mla_parts(q_ref, kv_ref, cos_ref, sin_ref)
        kp, dout = kp_ref[...], do_ref[...]
        p = _causal_probs((_dot(qn, kn, _NT) + _dot(qp, kp, _NT)) * MLA_SCALE, iq)
        ds = _softmax_vjp(p, _dot(dout, v, _NT)) * MLA_SCALE
        dqp = _dot(ds, kp, _NN)
        dq_ref[...] = jnp.concatenate([_dot(ds, kn, _NN), dqp * cos_ref[...], dqp * sin_ref[...]], axis=1)

        @pl.when(iq == 0)
        def _():
            dkv_ref[...] = jnp.zeros_like(dkv_ref)

        dkv_ref[...] += jnp.concatenate([_dot(ds, qn, _TN), _dot(p, dout, _TN)], axis=1)

        @pl.when((iq == 0) & (h == 0))
        def _():
            dkp_ref[...] = jnp.zeros_like(dkp_ref)

        dkp_ref[...] += _dot(ds, qp, _TN)

    return _pcall(
        body, name="mla_bwd", grid=(HEADS, LP // TQ),
        in_specs=_mla_specs() + [pl.BlockSpec((TQ, HD), lambda h, i: (i, h))],
        out_specs=[pl.BlockSpec((TQ, QG), lambda h, i: (i, h)), pl.BlockSpec((LP, 2 * HD), lambda h, i: (0, h)),
                   pl.BlockSpec((LP, LANE), lambda h, i: (0, 0))],
        out_shape=[jax.ShapeDtypeStruct((LP, HEADS * QG), F32), jax.ShapeDtypeStruct((LP, 2 * BW), F32),
                   jax.ShapeDtypeStruct((LP, LANE), F32)],
        compiler_params=_params(("arbitrary", "arbitrary")))(q, kv, kper, cosp, sinp, do)


def _fox_specs():
    cq, ck, cv = _cb(OFF_FOX, HD), _cb(OFF_FOX + BW, HD), _cb(OFF_FOX + 2 * BW, HD)
    return [pl.BlockSpec((TQ, HD), lambda h, i: (i, cq + h)),
            pl.BlockSpec((LP, HD), lambda h, i: (0, ck + h)),
            pl.BlockSpec((LP, HD), lambda h, i: (0, cv + h)),
            pl.BlockSpec((1, TQ, 1), lambda h, i: (h, i, 0)),
            pl.BlockSpec((1, 1, LP), lambda h, i: (h, 0, 0))]


def _fox_probs(q_ref, k_ref, cq_ref, ck_ref, iq):
    s = _dot(q_ref[...], k_ref[...], _NT) * FOX_SCALE + (cq_ref[0] - ck_ref[0])
    return _causal_probs(s, iq)


def _fox_fwd(p_all, cq3, ck3):
    def body(q_ref, k_ref, v_ref, cq_ref, ck_ref, o_ref):
        o_ref[...] = _dot(_fox_probs(q_ref, k_ref, cq_ref, ck_ref, pl.program_id(1)), v_ref[...], _NN)

    return _pcall(body, name="fox_fwd", grid=(HEADS, LP // TQ), in_specs=_fox_specs(),
                  out_specs=pl.BlockSpec((TQ, HD), lambda h, i: (i, h)),
                  out_shape=jax.ShapeDtypeStruct((LP, BW), F32),
                  compiler_params=_params(("parallel", "parallel")))(p_all, p_all, p_all, cq3, ck3)


def _fox_bwd(p_all, cq3, ck3, do):
    def body(q_ref, k_ref, v_ref, cq_ref, ck_ref, do_ref, dq_ref, dk_ref, dv_ref, dcq_ref, dck_ref):
        iq = pl.program_id(1)
        p = _fox_probs(q_ref, k_ref, cq_ref, ck_ref, iq)
        dout = do_ref[...]
        ds = _softmax_vjp(p, _dot(dout, v_ref[...], _NT))
        dss = ds * FOX_SCALE
        dq_ref[...] = _dot(dss, k_ref[...], _NN)
        dcq_ref[0] = jnp.sum(ds, axis=1, keepdims=True)

        @pl.when(iq == 0)
        def _():
            dk_ref[...] = jnp.zeros_like(dk_ref)
            dv_ref[...] = jnp.zeros_like(dv_ref)
            dck_ref[...] = jnp.zeros_like(dck_ref)

        dk_ref[...] += _dot(dss, q_ref[...], _TN)
        dv_ref[...] += _dot(p, dout, _TN)
        dck_ref[0] -= jnp.sum(ds, axis=0, keepdims=True)

    head_rows = pl.BlockSpec((TQ, HD), lambda h, i: (i, h))
    head_all = pl.BlockSpec((LP, HD), lambda h, i: (0, h))
    return _pcall(
        body, name="fox_bwd", grid=(HEADS, LP // TQ), in_specs=_fox_specs() + [head_rows],
        out_specs=[head_rows, head_all, head_all, pl.BlockSpec((1, TQ, 1), lambda h, i: (h, i, 0)),
                   pl.BlockSpec((1, 1, LP), lambda h, i: (h, 0, 0))],
        out_shape=[jax.ShapeDtypeStruct((LP, BW), F32)] * 3
        + [jax.ShapeDtypeStruct((HEADS, LP, 1), F32), jax.ShapeDtypeStruct((HEADS, 1, LP), F32)],
        compiler_params=_params(("arbitrary", "arbitrary")))(p_all, p_all, p_all, cq3, ck3, do)


def _log_sigmoid(x):
    return jnp.minimum(x, 0.0) - jnp.log(1.0 + jnp.exp(-jnp.abs(x)))


def _decay_fwd(p_all, b_pad):
    tc = TM_ROW

    def body(fl_ref, b_ref, c_ref):
        lf = _log_sigmoid(fl_ref[...] + b_ref[...])
        r = pl.program_id(0) * tc + lax.broadcasted_iota(jnp.int32, (tc, LP), 0)
        s = lax.broadcasted_iota(jnp.int32, (tc, LP), 1)
        c_ref[...] = jnp.dot((s <= r).astype(F32), lf, precision=lax.Precision.HIGHEST, preferred_element_type=F32)

    return _pcall(body, name="decay_fwd", grid=(LP // tc,),
                  in_specs=[pl.BlockSpec((LP, LANE), lambda i: (0, _cb(OFF_FL, LANE))),
                            pl.BlockSpec((1, LANE), lambda i: (0, 0))],
                  out_specs=pl.BlockSpec((tc, LANE), lambda i: (i, 0)),
                  out_shape=jax.ShapeDtypeStruct((LP, LANE), F32), compiler_params=_params(("parallel",)))(p_all, b_pad)


def _decay_bwd(p_all, b_pad, dc):
    tc = TM_ROW

    def body(fl_ref, b_ref, dc_ref, dfl_ref, db_ref):
        i = pl.program_id(0)
        r = i * tc + lax.broadcasted_iota(jnp.int32, (tc, LP), 0)
        t = lax.broadcasted_iota(jnp.int32, (tc, LP), 1)
        dlf = jnp.dot((t >= r).astype(F32), dc_ref[...], precision=lax.Precision.HIGHEST, preferred_element_type=F32)
        dfl = dlf * jax.nn.sigmoid(-(fl_ref[...] + b_ref[...]))
        dfl_ref[...] = dfl

        @pl.when(i == 0)
        def _():
            db_ref[...] = jnp.zeros_like(db_ref)

        db_ref[...] += jnp.sum(dfl, axis=0, keepdims=True)

    return _pcall(body, name="decay_bwd", grid=(LP // tc,),
                  in_specs=[pl.BlockSpec((tc, LANE), lambda i: (i, _cb(OFF_FL, LANE))),
                            pl.BlockSpec((1, LANE), lambda i: (0, 0)), pl.BlockSpec((LP, LANE), lambda i: (0, 0))],
                  out_specs=[pl.BlockSpec((tc, LANE), lambda i: (i, 0)), pl.BlockSpec((1, LANE), lambda i: (0, 0))],
                  out_shape=[jax.ShapeDtypeStruct((LP, LANE), F32), jax.ShapeDtypeStruct((1, LANE), F32)],
                  compiler_params=_params(("arbitrary",)))(p_all, b_pad, dc)


def _conv_specs(layer):
    c0 = _cb(OFF_CONV, CW)
    step = BW // CW
    return [pl.BlockSpec((LP, CW), lambda j: (0, c0 + j)), pl.BlockSpec((LP, CW), lambda j: (0, c0 + step + j)),
            pl.BlockSpec((LP, CW), lambda j: (0, c0 + 2 * step + j)),
            pl.BlockSpec((None, 3, CW), lambda j: (layer, 0, j))]


def _shift_down(x, k, t):
    return jnp.where(t >= k, pltpu.roll(x, k, 0), 0.0)


def _shift_up(x, k, t):
    return jnp.where(t < LP - k, pltpu.roll(x, LP - k, 0), 0.0)


def _conv_fwd(p_all, cw, layer):
    def body(b_ref, c_ref, x_ref, w_ref, o_ref):
        t = lax.broadcasted_iota(jnp.int32, (LP, CW), 0)
        uu = c_ref[...] * x_ref[...]
        w = w_ref[...]
        u = w[2:3] * uu + w[1:2] * _shift_down(uu, 1, t) + w[0:1] * _shift_down(uu, 2, t)
        o_ref[...] = b_ref[...] * u

    return _pcall(body, name="conv_fwd", grid=(BW // CW,), in_specs=_conv_specs(layer),
                  out_specs=pl.BlockSpec((LP, CW), lambda j: (0, j)), out_shape=jax.ShapeDtypeStruct((LP, BW), F32),
                  compiler_params=_params(("parallel",)))(p_all, p_all, p_all, cw)


def _conv_bwd(p_all, cw, layer, do):
    def body(b_ref, c_ref, x_ref, w_ref, do_ref, d_ref, dw_ref):
        t = lax.broadcasted_iota(jnp.int32, (LP, CW), 0)
        cc, xx, w, dout = c_ref[...], x_ref[...], w_ref[...], do_ref[...]
        uu = cc * xx
        s1, s2 = _shift_down(uu, 1, t), _shift_down(uu, 2, t)
        u = w[2:3] * uu + w[1:2] * s1 + w[0:1] * s2
        du = dout * b_ref[...]
        duu = w[2:3] * du + w[1:2] * _shift_up(du, 1, t) + w[0:1] * _shift_up(du, 2, t)
        d_ref[0] = dout * u
        d_ref[1] = duu * xx
        d_ref[2] = duu * cc
        dw_ref[0:1, :] = jnp.sum(du * s2, axis=0, keepdims=True)
        dw_ref[1:2, :] = jnp.sum(du * s1, axis=0, keepdims=True)
        dw_ref[2:3, :] = jnp.sum(du * uu, axis=0, keepdims=True)

    return _pcall(body, name="conv_bwd", grid=(BW // CW,),
                  in_specs=_conv_specs(layer) + [pl.BlockSpec((LP, CW), lambda j: (0, j))],
                  out_specs=[pl.BlockSpec((3, LP, CW), lambda j: (0, 0, j)), pl.BlockSpec((3, CW), lambda j: (0, j))],
                  out_shape=[jax.ShapeDtypeStruct((3, LP, BW), F32), jax.ShapeDtypeStruct((3, BW), F32)],
                  compiler_params=_params(("parallel",)))(p_all, p_all, p_all, cw, do)


def _place():
    return lax.axis_index("x"), lax.axis_index("y"), lax.axis_index("c")


def _dest_slice(ref, kind, d):
    if kind == "raw":
        return ref.at[d]
    nd = len(ref.shape)
    if kind == "col":
        w = ref.shape[-1] // N_DEV
        return ref.at[(slice(None),) * (nd - 1) + (pl.ds(d * w, w),)]
    r = ref.shape[-2] // N_DEV
    return ref.at[(slice(None),) * (nd - 2) + (pl.ds(d * r, r), slice(None))]


def _unsharded_shape(shape, kind, lead=N_DEV):
    if kind == "raw":
        return (lead,) + shape
    if kind == "col":
        return shape[:-1] + (N_DEV * shape[-1],)
    return shape[:-2] + (N_DEV * shape[-2], shape[-1])


def _all_gather(arrs, kinds, name):
    n = len(arrs)

    def body(*refs):
        ins, outs = refs[:n], refs[n:2 * n]
        send, recv, loc = refs[2 * n:]
        x, y, c = _place()
        sib = (x, y, 1 - c)
        chips = [(1 - x, y), (x, 1 - y), (1 - x, 1 - y)]

        def idx(px, py, pc):
            return 4 * px + 2 * py + pc

        def copy(j, k, block, to, src=None):
            dst = _dest_slice(outs[k], kinds[k], idx(*block))
            return pltpu.make_async_remote_copy(src_ref=dst if src is None else src, dst_ref=dst, send_sem=send.at[j, k],
                                                recv_sem=recv.at[j, k], device_id=to, device_id_type=MESH)

        me = (x, y, c)
        mine = [pltpu.make_async_copy(ins[k], _dest_slice(outs[k], kinds[k], idx(*me)), loc.at[k]) for k in range(n)]
        for cp in mine:
            cp.start()
        first = [copy(0, k, me, sib, src=ins[k]) for k in range(n)]
        first += [copy(1 + j, k, me, (*chip, c), src=ins[k]) for j, chip in enumerate(chips) for k in range(n)]
        for cp in first:
            cp.start()
        passed = []
        for j, chip in enumerate(chips):
            for k in range(n):
                copy(1 + j, k, (*chip, c), me, src=ins[k]).wait_recv()
                cp = copy(4 + j, k, (*chip, c), sib)
                cp.start()
                passed.append(cp)
        for k in range(n):
            copy(0, k, sib, me, src=ins[k]).wait_recv()
        for j, chip in enumerate(chips):
            for k in range(n):
                copy(4 + j, k, (*chip, 1 - c), me).wait_recv()
        for cp in first + passed:
            cp.wait_send()
        for cp in mine:
            cp.wait()

    return _pcall(body, name=name, in_specs=[_ANY] * n, out_specs=[_ANY] * n,
                  out_shape=[jax.ShapeDtypeStruct(_unsharded_shape(a.shape, kd), a.dtype) for a, kd in zip(arrs, kinds)],
                  scratch_shapes=[pltpu.SemaphoreType.DMA((7, n)), pltpu.SemaphoreType.DMA((7, n)),
                                  pltpu.SemaphoreType.DMA((n,))])(*arrs)


def _shard_shape(a, kind):
    if kind == "raw":
        return a.shape[2:]
    if kind == "col":
        return a.shape[:-1] + (a.shape[-1] // N_DEV,)
    return a.shape[:-2] + (a.shape[-2] // N_DEV, a.shape[-1])


def _swap_sibling(arrs, kinds, name):
    n = len(arrs)
    npieces = sum(1 if kd == "raw" else 4 for kd in kinds)

    def body(*refs):
        ins, outs = refs[:n], refs[n:2 * n]
        send, recv = refs[2 * n:]
        x, y, c = _place()
        pieces = []
        for k in range(n):
            if kinds[k] == "raw":
                pieces.append((ins[k].at[1 - c], outs[k]))
            else:
                pieces += [(_dest_slice(ins[k], kinds[k], 2 * p + 1 - c), outs[k].at[p]) for p in range(4)]
        cps = [pltpu.make_async_remote_copy(src_ref=src, dst_ref=dst, send_sem=send.at[i], recv_sem=recv.at[i],
                                            device_id=(x, y, 1 - c), device_id_type=MESH)
               for i, (src, dst) in enumerate(pieces)]
        for cp in cps:
            cp.start()
        for cp in cps:
            cp.wait()

    return _pcall(body, name=name, in_specs=[_ANY] * n, out_specs=[_ANY] * n,
                  out_shape=[jax.ShapeDtypeStruct((4,) + _shard_shape(a, kd), a.dtype) for a, kd in zip(arrs, kinds)],
                  scratch_shapes=[pltpu.SemaphoreType.DMA((npieces,)), pltpu.SemaphoreType.DMA((npieces,))])(*arrs)


def _swap_chips(arrs, name):
    n = len(arrs)

    def body(*refs):
        ins, outs = refs[:n], refs[n:2 * n]
        send, recv, loc = refs[2 * n:]
        x, y, c = _place()
        mychip = 2 * x + y
        mine = [pltpu.make_async_copy(ins[k].at[mychip], outs[k].at[mychip], loc.at[k]) for k in range(n)]
        for cp in mine:
            cp.start()
        cps = []
        for j, (cx, cy) in enumerate([(1 - x, y), (x, 1 - y), (1 - x, 1 - y)]):
            for k in range(n):
                cps.append(pltpu.make_async_remote_copy(
                    src_ref=ins[k].at[2 * cx + cy], dst_ref=outs[k].at[mychip], send_sem=send.at[j, k],
                    recv_sem=recv.at[j, k], device_id=(cx, cy, c), device_id_type=MESH))
        for cp in cps:
            cp.start()
        for cp in cps:
            cp.wait()
        for cp in mine:
            cp.wait()

    return _pcall(body, name=name, in_specs=[_ANY] * n, out_specs=[_ANY] * n,
                  out_shape=[jax.ShapeDtypeStruct(a.shape, a.dtype) for a in arrs],
                  scratch_shapes=[pltpu.SemaphoreType.DMA((3, n)), pltpu.SemaphoreType.DMA((3, n)),
                                  pltpu.SemaphoreType.DMA((n,))])(*arrs)


def _rows_tile(rows):
    return _pick(rows, (128, 64, 32, 16))


def _pair_sum(g, r1, core, kind, name):
    _, groups, rows, cols = r1.shape
    tr = _rows_tile(rows)
    nr = rows // tr
    if kind == "raw":
        g_spec = pl.BlockSpec((None, None, None, tr, cols), lambda p, q, i, c: (c[0], p, q, i, 0))
    elif kind == "col":
        g_spec = pl.BlockSpec((None, tr, cols), lambda p, q, i, c: (q, i, 2 * p + c[0]))
    else:
        g_spec = pl.BlockSpec((None, tr, cols), lambda p, q, i, c: (q, (2 * p + c[0]) * nr + i, 0))
    r_spec = pl.BlockSpec((None, None, tr, cols), lambda p, q, i, c: (p, q, i, 0))

    def body(c_ref, g_ref, r_ref, o_ref):
        o_ref[...] = (g_ref[...].astype(F32) + r_ref[...].astype(F32)).astype(o_ref.dtype)

    return _pcall(
        body, name=name,
        grid_spec=pltpu.PrefetchScalarGridSpec(num_scalar_prefetch=1, grid=(4, groups, nr), in_specs=[g_spec, r_spec],
                                               out_specs=r_spec),
        out_shape=jax.ShapeDtypeStruct(r1.shape, r1.dtype),
        compiler_params=_params(("parallel", "parallel", "parallel")))(core, g, r1)


def _adamw(g, w, m, v):
    m = ADAM_B1 * m + (1.0 - ADAM_B1) * g
    v = ADAM_B2 * v + (1.0 - ADAM_B2) * jnp.square(g)
    m_hat = m / (1.0 - ADAM_B1 ** ADAM_STEP)
    v_hat = v / (1.0 - ADAM_B2 ** ADAM_STEP)
    return -ADAM_LR * (m_hat / (jnp.sqrt(v_hat) + ADAM_EPS) + ADAM_WD * w), m, v


def _sum_adamw(parts, w, m, v, name):
    npart, groups, rows, cols = parts.shape
    tr = _rows_tile(rows)

    def body(p_ref, w_ref, m_ref, v_ref, g_ref, d_ref, nm_ref, nv_ref):
        g = p_ref[0].astype(F32)
        for k in range(1, npart):
            g = g + p_ref[k].astype(F32)
        g_ref[...] = g
        d_ref[...], nm_ref[...], nv_ref[...] = _adamw(g, w_ref[...], m_ref[...], v_ref[...])

    blk = pl.BlockSpec((None, tr, cols), lambda q, i: (q, i, 0))
    return _pcall(body, name=name, grid=(groups, rows // tr),
                  in_specs=[pl.BlockSpec((npart, None, tr, cols), lambda q, i: (0, q, i, 0)), blk, blk, blk],
                  out_specs=[blk] * 4, out_shape=[jax.ShapeDtypeStruct((groups, rows, cols), F32)] * 4,
                  compiler_params=_params(("parallel", "parallel")))(parts, w, m, v)


def _rot_cols(w):
    return jnp.concatenate([-w[..., ROPE // 2:], w[..., :ROPE // 2]], axis=-1)


def _rot_cols_t(dw):
    return jnp.concatenate([dw[..., ROPE // 2:], -dw[..., :ROPE // 2]], axis=-1)


_IN_SPLITS = np.cumsum([0, Q_RANK, KV_RANK, ROPE, BW, BW, BW, BW, BW, BW, HEADS, 3 * D_MODEL])


def _ext_w_in(w):
    o = _IN_SPLITS
    kpe = w[:, o[2]:o[3]]
    z = lambda n: jnp.zeros((w.shape[0], n), w.dtype)
    return jnp.concatenate([w[:, o[10]:o[11]], w[:, o[3]:o[9]], w[:, o[0]:o[2]], kpe, z(LANE - ROPE), _rot_cols(kpe),
                            z(LANE - ROPE), w[:, o[9]:o[10]], z(2 * LANE - HEADS)], axis=1)


def _unext_w_in(dw):
    kpe = dw[:, OFF_KPE:OFF_KPE + ROPE] + _rot_cols_t(dw[:, OFF_KROT:OFF_KROT + ROPE])
    return jnp.concatenate([dw[:, OFF_CQ:OFF_KPE], kpe, dw[:, OFF_CONV:OFF_CQ], dw[:, OFF_FL:OFF_FL + HEADS],
                            dw[:, OFF_GATE:OFF_CONV]], axis=1)


def _ext_w_uq(w):
    w3 = w.reshape(Q_RANK, HEADS, HD + ROPE)
    pe = w3[..., HD:]
    z = jnp.zeros((Q_RANK, HEADS, LANE - ROPE), w.dtype)
    return jnp.concatenate([w3[..., :HD], pe, z, _rot_cols(pe), z], axis=-1).reshape(Q_RANK, HEADS * QG)


def _unext_w_uq(dw):
    d3 = dw.reshape(Q_RANK, HEADS, QG)
    pe = d3[..., HD:HD + ROPE] + _rot_cols_t(d3[..., 2 * HD:2 * HD + ROPE])
    return jnp.concatenate([d3[..., :HD], pe], axis=-1).reshape(Q_RANK, HEADS * (HD + ROPE))


_BIG = (("meta", "col", F32), ("w_in", "raw", BF16), ("w_uq", "raw", BF16), ("w_ukv", "col", BF16),
        ("conv_w", "col", F32), ("w_branch", "col", BF16), ("w_out", "row", BF16), ("w_ffn_in", "col", BF16),
        ("w_ffn_out", "row", BF16))
_KINDS = tuple(kd for _, kd, _ in _BIG)
_SMALL = ("b_forget", "g_q_lat", "g_kv_lat", "g_mix_pre", "g_mix_post", "g_ffn_pre", "g_ffn_post")
_ORDER = ("meta", "w_in", "b_forget", "g_q_lat", "g_kv_lat", "w_uq", "w_ukv", "conv_w", "w_branch", "w_out",
          "w_ffn_in", "w_ffn_out", "g_mix_pre", "g_mix_post", "g_ffn_pre", "g_ffn_post")


def _unshard_cols(g):
    g = jnp.moveaxis(g, 0, -2)
    return g.reshape(g.shape[:-2] + (g.shape[-2] * g.shape[-1],))


def _by_dest_cols(full, dtype):
    g = full.reshape(full.shape[:-1] + (N_DEV // 2, 2, full.shape[-1] // N_DEV))
    return jnp.moveaxis(g, (-3, -2), (1, 0)).astype(dtype)


def _as3d(a, lead=0):
    return a.reshape(a.shape[:lead] + (-1,) + a.shape[-2:])


def kernel(x, meta, w_in, b_forget, g_q_lat, g_kv_lat, w_uq, w_ukv, conv_w, w_branch, w_out, w_ffn_in, w_ffn_out, g_mix_pre, g_mix_post, g_ffn_pre, g_ffn_post, loss_target, m_meta, m_w_in, m_b_forget, m_g_q_lat, m_g_kv_lat, m_w_uq, m_w_ukv, m_conv_w, m_w_branch, m_w_out, m_w_ffn_in, m_w_ffn_out, m_g_mix_pre, m_g_mix_post, m_g_ffn_pre, m_g_ffn_post, v_meta, v_w_in, v_b_forget, v_g_q_lat, v_g_kv_lat, v_w_uq, v_w_ukv, v_conv_w, v_w_branch, v_w_out, v_w_ffn_in, v_w_ffn_out, v_g_mix_pre, v_g_mix_post, v_g_ffn_pre, v_g_ffn_post):
    given = dict(locals())
    core = lax.axis_index("c").astype(jnp.int32).reshape(1)

    gathered = _all_gather([given[n].astype(dt) for n, _, dt in _BIG], _KINDS, "gather_weights")
    full = {n: _unshard_cols(g) if kd == "raw" else g for (n, kd, _), g in zip(_BIG, gathered)}

    pos = jnp.arange(LP, dtype=F32)[:, None]
    inv_freq = 1.0 / (ROPE_THETA ** (jnp.arange(0, ROPE, 2, dtype=F32) / ROPE))
    ang = pos * inv_freq[None, :]
    zpad = jnp.zeros((LP, LANE - ROPE), F32)
    cosp = jnp.concatenate([jnp.cos(ang), jnp.cos(ang), zpad], axis=1)
    sinp = jnp.concatenate([jnp.sin(ang), jnp.sin(ang), zpad], axis=1)

    tail = jnp.zeros((LP - L_TOK, D_MODEL), F32)
    h = jnp.concatenate([full["meta"], x[0], tail], axis=0)
    ltp = jnp.concatenate([jnp.zeros((N_META, D_MODEL), F32), loss_target[0], tail], axis=0)
    row = jnp.arange(LP)[:, None]
    rmask = ((row >= N_META) & (row < L_TOK)).astype(F32)

    def vec(a, l):
        return a[l][None, :]

    wl = []
    for l in range(DEPTH):
        wl.append(dict(
            w_all=_ext_w_in(full["w_in"][l]), w_uq=_ext_w_uq(full["w_uq"][l]),
            b_pad=jnp.concatenate([b_forget[l], jnp.zeros((LANE - HEADS,), F32)])[None, :],
            gq=vec(g_q_lat, l), gkv=vec(g_kv_lat, l), g1=vec(g_mix_pre, l), g2=vec(g_mix_post, l),
            g3=vec(g_ffn_pre, l), g4=vec(g_ffn_post, l)))

    def prep_ins(p_all, w):
        return [R(p_all, Q_RANK, _cb(OFF_CQ, Q_RANK)), R(p_all, KV_RANK, _cb(OFF_CKV, KV_RANK)),
                R(p_all, LANE, _cb(OFF_KPE, LANE)), R(p_all, LANE, _cb(OFF_KROT, LANE)), Pm(w["gq"]), Pm(w["gkv"]),
                R(cosp, LANE), R(sinp, LANE)]

    def merge_ins(p_all, ys):
        return [R(p_all, GW, _cb(OFF_GATE + n * D_MODEL, GW), 1) for n in range(3)] + [R(yv, GW, 0, 1) for yv in ys]

    (hn,) = _rw(lambda a, g: (_rms(a, g),), [R(h, D_MODEL), Pm(wl[0]["g1"])], [OR(D_MODEL, D_MODEL)], name="rms_in")
    saved = []
    for l in range(DEPTH):
        w = wl[l]
        s = dict(h=h, hn=hn)
        p_all = _mm(hn, w["w_all"], name="proj_in")
        cqn, ckvn, kper = _rw(_mla_prep, prep_ins(p_all, w), [OR(Q_RANK, Q_RANK), OR(KV_RANK, KV_RANK), OR(LANE, LANE)],
                              name="mla_prep")
        q = _mm(cqn, w["w_uq"], name="proj_q")
        kv = _mm(ckvn, full["w_ukv"], bidx=(l,), name="proj_kv")
        o_a = _mla_fwd(q, kv, kper, cosp, sinp)
        o_b = _conv_fwd(p_all, full["conv_w"], l)
        cdec = _decay_fwd(p_all, w["b_pad"])
        cq3 = cdec[:, :HEADS].T[:, :, None]
        ck3 = cdec[:, :HEADS].T[:, None, :]
        o_c = _fox_fwd(p_all, cq3, ck3)
        outs = (o_a, o_b, o_c)
        ys = [_mm(outs[n], full["w_branch"], bidx=(l, n), name="proj_branch") for n in range(3)]
        (merged,) = _rw(_merge, merge_ins(p_all, ys), [OR(D_MODEL, GW, 1)], ncol=D_MODEL // GW, name="merge")
        mix = _mm(merged, full["w_out"], bidx=(l,), name="proj_out")
        h2, hn2 = _rw(_resid_norm, [R(h, D_MODEL), R(mix, D_MODEL), Pm(w["g2"]), Pm(w["g3"])],
                      [OR(D_MODEL, D_MODEL)] * 2, name="resid_norm")
        gu = _mm(hn2, full["w_ffn_in"], bidx=(l,), name="ffn_in")
        (act,) = _rw(_swiglu, [R(gu, 2 * D_FF)], [OR(D_FF, D_FF)], tm=TM_FF, name="swiglu")
        f = _mm(act, full["w_ffn_out"], bidx=(l,), name="ffn_out")
        s.update(p_all=p_all, cqn=cqn, ckvn=ckvn, kper=kper, q=q, kv=kv, outs=outs, cq3=cq3, ck3=ck3, ys=ys,
                 merged=merged, mix=mix, h2=h2, hn2=hn2, gu=gu, act=act, f=f)
        saved.append(s)
        if l + 1 < DEPTH:
            h, hn = _rw(_resid_norm, [R(h2, D_MODEL), R(f, D_MODEL), Pm(w["g4"]), Pm(wl[l + 1]["g1"])],
                        [OR(D_MODEL, D_MODEL)] * 2, name="resid_norm")

    grads = {n: [None] * DEPTH for n in _ORDER if n != "meta"}
    gst = {n: lax.empty(full[n].shape, dt) for n, kd, dt in _BIG if n in ("w_ukv", "w_branch", "w_out", "w_ffn_in", "w_ffn_out")}
    s, w = saved[-1], wl[-1]
    dh2, df, dg4, loss_acc = _rw(
        _loss_bwd, [R(s["h2"], D_MODEL), R(s["f"], D_MODEL), Pm(w["g4"]), R(ltp, D_MODEL), R(rmask, 1)],
        [OR(D_MODEL, D_MODEL), OR(D_MODEL, D_MODEL), OA((1, D_MODEL)), OA((1, LANE))], name="loss_bwd")
    loss = lax.psum(loss_acc[0, 0], ("x", "y", "c"))
    grads["g_ffn_post"][DEPTH - 1] = dg4[0]
    for l in reversed(range(DEPTH)):
        s, w = saved[l], wl[l]
        p_all = s["p_all"]
        dact = _mm(df, full["w_ffn_out"], tb=True, bidx=(l,), name="d_act")
        gst["w_ffn_out"] = _mm(s["act"], df, ta=True, stack=(gst["w_ffn_out"], (l,)), name="dw_ffn_out")
        (dgu,) = _rw(_swiglu_bwd, [R(s["gu"], 2 * D_FF), R(dact, D_FF)], [OR(2 * D_FF, 2 * D_FF)], tm=TM_FF,
                     name="swiglu_bwd")
        dhn2 = _mm(dgu, full["w_ffn_in"], tb=True, bidx=(l,), name="d_hn2")
        gst["w_ffn_in"] = _mm(s["hn2"], dgu, ta=True, stack=(gst["w_ffn_in"], (l,)), name="dw_ffn_in")
        dh, dmix, dg2, dg3 = _rw(
            _resid_norm_bwd, [R(s["h"], D_MODEL), R(s["mix"], D_MODEL), Pm(w["g2"]), Pm(w["g3"]), R(dh2, D_MODEL),
                              R(dhn2, D_MODEL)],
            [OR(D_MODEL, D_MODEL), OR(D_MODEL, D_MODEL), OA((1, D_MODEL)), OA((1, D_MODEL))], name="resid_norm_bwd")
        grads["g_mix_post"][l], grads["g_ffn_pre"][l] = dg2[0], dg3[0]
        dmerged = _mm(dmix, full["w_out"], tb=True, bidx=(l,), name="d_merged")
        gst["w_out"] = _mm(s["merged"], dmix, ta=True, stack=(gst["w_out"], (l,)), name="dw_out")
        mb = _rw(_merge_bwd, merge_ins(p_all, s["ys"]) + [R(dmerged, GW, 0, 1)], [OR(D_MODEL, GW, 1)] * 6,
                 ncol=D_MODEL // GW, name="merge_bwd")
        dgate, dys = mb[:3], mb[3:]
        dos = [_mm(dys[n], full["w_branch"], tb=True, bidx=(l, n), name="d_branch") for n in range(3)]
        for n in range(3):
            gst["w_branch"] = _mm(s["outs"][n], dys[n], ta=True, stack=(gst["w_branch"], (l, n)), name="dw_branch")
        dfq, dfk, dfv, dcq3, dck3 = _fox_bwd(p_all, s["cq3"], s["ck3"], dos[2])
        dc = jnp.concatenate([dcq3[:, :, 0].T + dck3[:, 0, :].T, jnp.zeros((LP, LANE - HEADS), F32)], axis=1)
        dfl, db = _decay_bwd(p_all, w["b_pad"], dc)
        grads["b_forget"][l] = db[0, :HEADS]
        dconv, dcw = _conv_bwd(p_all, full["conv_w"], l, dos[1])
        grads["conv_w"][l] = dcw
        dq, dkv, dkper = _mla_bwd(s["q"], s["kv"], s["kper"], cosp, sinp, dos[0])
        dcqn = _mm(dq, w["w_uq"], tb=True, name="d_cqn")
        grads["w_uq"][l] = _unext_w_uq(_mm(s["cqn"], dq, ta=True, name="dw_uq"))
        dckvn = _mm(dkv, full["w_ukv"], tb=True, bidx=(l,), name="d_ckvn")
        gst["w_ukv"] = _mm(s["ckvn"], dkv, ta=True, stack=(gst["w_ukv"], (l,)), name="dw_ukv")
        dcq, dckv, dkpe, dkrot, dgq, dgkv = _rw(
            _mla_prep_bwd, prep_ins(p_all, w) + [R(dcqn, Q_RANK), R(dckvn, KV_RANK), R(dkper, LANE)],
            [OR(Q_RANK, Q_RANK), OR(KV_RANK, KV_RANK), OR(LANE, LANE), OR(LANE, LANE), OA((1, Q_RANK)),
             OA((1, KV_RANK))], name="mla_prep_bwd")
        grads["g_q_lat"][l], grads["g_kv_lat"][l] = dgq[0], dgkv[0]
        dp = jnp.concatenate([*dgate, dconv[0], dconv[1], dconv[2], dfq, dfk, dfv, dcq, dckv, dkpe, dkrot, dfl,
                              jnp.zeros((LP, LANE), F32)], axis=1)
        dhn = _mm(dp, w["w_all"], tb=True, name="d_hn")
        grads["w_in"][l] = _unext_w_in(_mm(s["hn"], dp, ta=True, name="dw_in"))
        if l > 0:
            sp, wp = saved[l - 1], wl[l - 1]
            dh2, df, dg4, dg1 = _rw(
                _resid_norm_bwd, [R(sp["h2"], D_MODEL), R(sp["f"], D_MODEL), Pm(wp["g4"]), Pm(w["g1"]),
                                  R(dh, D_MODEL), R(dhn, D_MODEL)],
                [OR(D_MODEL, D_MODEL), OR(D_MODEL, D_MODEL), OA((1, D_MODEL)), OA((1, D_MODEL))], name="resid_norm_bwd")
            grads["g_ffn_post"][l - 1], grads["g_mix_pre"][l] = dg4[0], dg1[0]
        else:
            dh0, dg1 = _rw(_rms_bwd, [R(s["h"], D_MODEL), Pm(w["g1"]), R(dhn, D_MODEL), R(dh, D_MODEL)],
                           [OR(D_MODEL, D_MODEL), OA((1, D_MODEL))], name="rms_in_bwd")
            grads["g_mix_pre"][0] = dg1[0]
    grad_x = dh0[N_META:L_TOK][None]
    gfull = {n: jnp.stack(grads[n]) for n in grads if n not in gst}
    gfull["meta"] = dh0[:N_META]

    partial = [gst[n] if n in gst else (_by_dest_cols(gfull[n], dt) if kd == "raw" else gfull[n].astype(dt))
               for n, kd, dt in _BIG]
    from_sib = _swap_sibling(partial, _KINDS, "scatter_sibling")
    chip_sums = []
    for (n, kd, _), g, r in zip(_BIG, partial, from_sib):
        ps = _pair_sum(_as3d(g, 2 if kd == "raw" else 0), _as3d(r, 1), core, kd, "pair_sum_" + n)
        chip_sums.append(ps)
    parts = _swap_chips(chip_sums, "scatter_chips")
    out = {}
    for (n, _, _), p in zip(_BIG, parts):
        res = _sum_adamw(p, _as3d(given[n]), _as3d(given["m_" + n]), _as3d(given["v_" + n]), "adamw_" + n)
        out[n] = [r.reshape(given[n].shape) for r in res]

    def pack(d):
        flat = jnp.concatenate([d[n].reshape(-1) for n in _SMALL])
        return jnp.concatenate([flat, jnp.zeros((-flat.shape[0]) % (8 * LANE), F32)]).reshape(-1, LANE)

    (small_parts,) = _all_gather([pack(gfull)], ("raw",), "gather_small_grads")
    res = _sum_adamw(small_parts[:, None], pack(given)[None], pack({n: given["m_" + n] for n in _SMALL})[None],
                     pack({n: given["v_" + n] for n in _SMALL})[None], "adamw_small")
    off = 0
    for n in _SMALL:
        size = int(np.prod(given[n].shape))
        out[n] = [r.reshape(-1)[off:off + size].reshape(given[n].shape) for r in res]
        off += size

    return (loss, grad_x, *[out[n][0] for n in _ORDER], *[out[n][1] for n in _ORDER], *[out[n][2] for n in _ORDER],
            *[out[n][3] for n in _ORDER])
```

```python
import functools

import numpy as np
import jax
import jax.numpy as jnp
from jax import lax
from jax.experimental import pallas as pl
from jax.experimental.pallas import tpu as pltpu

D_MODEL = 2048
SEQ = 2048
DEPTH = 4
Q_RANK = 512
KV_RANK = 512
D_FF = 5632
N_META = 16
HEADS = 8
HD = 128
ROPE = 64
BW = HEADS * HD
EPS = 1e-6
NEG_INF = -1e30
ROPE_THETA = 10000.0
N_DEV = 8
LANE = 128
L_TOK = N_META + SEQ
LP = -(-L_TOK // LANE) * LANE
D_IN = Q_RANK + KV_RANK + ROPE + 6 * BW + HEADS + 3 * D_MODEL
MLA_SCALE = (HD + ROPE) ** -0.5
FOX_SCALE = HD ** -0.5
ADAM_LR, ADAM_B1, ADAM_B2, ADAM_EPS, ADAM_WD, ADAM_STEP = 0.001, 0.9, 0.999, 1e-08, 0.01, 10
VMEM_LIMIT = 48 * 1024 * 1024

F32 = jnp.float32
BF16 = jnp.bfloat16
MESH = pl.DeviceIdType.MESH

OFF_GATE = 0
OFF_CONV = 3 * D_MODEL
OFF_FOX = OFF_CONV + 3 * BW
OFF_CQ = OFF_FOX + 3 * BW
OFF_CKV = OFF_CQ + Q_RANK
OFF_KPE = OFF_CKV + KV_RANK
OFF_KROT = OFF_KPE + LANE
OFF_FL = OFF_KROT + LANE
W_ALL = OFF_FL + 2 * LANE
GW = 512 if D_MODEL % 512 == 0 else 256
TM_FF = 64
CW = 128
QG = 3 * LANE


def _pick(n, prefs):
    for p in prefs:
        if n % p == 0:
            return p
    return n


TM_ROW = 128
TQ = _pick(LP, (272, 128))


def _cb(off, w):
    assert off % w == 0, (off, w)
    return off // w


def _pcall(body, **kw):
    return pl.pallas_call(body, **kw)


def _params(sem):
    return pltpu.CompilerParams(dimension_semantics=sem, vmem_limit_bytes=VMEM_LIMIT)


def _bf(x):
    return x.astype(BF16)


def _dot(a, b, dims, **kw):
    return lax.dot_general(_bf(a), _bf(b), (dims, ((), ())), preferred_element_type=F32, **kw)


_NN = ((1,), (0,))
_NT = ((1,), (1,))
_TN = ((0,), (0,))
_ANY = pl.BlockSpec(memory_space=pl.ANY)


MM_VMEM_BUDGET = 38 * 1024 * 1024
HBM_BYTES_PER_STEP = 1 << 20


def _divisors(n, prefs):
    return [p for p in prefs if n % p == 0] or [n]


def _mm_tiles(m, n, kd, a_bytes, b_bytes, o_bytes):
    best = None
    for tm in _divisors(m, (2176, 2048, 1088, 1024, 544, 512, 272, 256, 128)):
        for tn in _divisors(n, (2048, 1536, 1024, 768, 512, 384, 256, 128)):
            for tk in _divisors(kd, (2816, 2304, 2176, 2048, 1536, 1408, 1024, 768, 544, 512, 384, 272, 256, 128)):
                nk = kd // tk
                vmem = 2 * (tm * tk * a_bytes + tk * tn * b_bytes + tm * tn * o_bytes) + 2 * tm * tn * 4
                vmem += (tm * tk * 2 if a_bytes == 4 else 0) + (tk * tn * 2 if b_bytes == 4 else 0)
                if vmem > MM_VMEM_BUDGET:
                    continue
                steps = (m // tm) * (n // tn) * nk
                cost = (m * kd * a_bytes * (1 if nk == 1 else n // tn) + kd * n * b_bytes * (m // tm)
                        + steps * HBM_BYTES_PER_STEP)
                if best is None or cost < best[0]:
                    best = (cost, tm, tn, tk)
    assert best is not None, (m, n, kd)
    return best[1:]


def _mm(a, b, *, ta=False, tb=False, bidx=(), stack=None, out_dtype=F32, name):
    if ta:
        kd, m = a.shape
    else:
        m, kd = a.shape
    nlead = len(bidx)
    if tb:
        n, kd2 = b.shape[nlead:]
    else:
        kd2, n = b.shape[nlead:]
    assert kd == kd2, (a.shape, b.shape, ta, tb)
    if stack is not None:
        out_dtype = stack[0].dtype
    tm, tn, tk = _mm_tiles(m, n, kd, a.dtype.itemsize, b.dtype.itemsize, jnp.dtype(out_dtype).itemsize)
    nk = kd // tk
    dims = _TN if ta else (_NT if tb else _NN)

    def body(a_ref, b_ref, *rest):
        o_ref, acc_ref = rest[-2:]
        if nk == 1:
            o_ref[...] = _dot(a_ref[...], b_ref[...], dims).astype(o_ref.dtype)
            return
        k = pl.program_id(2)

        @pl.when(k == 0)
        def _():
            acc_ref[...] = jnp.zeros_like(acc_ref)

        acc_ref[...] += _dot(a_ref[...], b_ref[...], dims)

        @pl.when(k == nk - 1)
        def _():
            o_ref[...] = acc_ref[...].astype(o_ref.dtype)

    lead = (None,) * nlead
    a_spec = pl.BlockSpec((tk, tm), lambda i, j, k: (k, i)) if ta else pl.BlockSpec((tm, tk), lambda i, j, k: (i, k))
    if tb:
        b_spec = pl.BlockSpec(lead + (tn, tk), lambda i, j, k: bidx + (j, k))
    else:
        b_spec = pl.BlockSpec(lead + (tk, tn), lambda i, j, k: bidx + (k, j))
    in_specs, args, extra = [a_spec, b_spec], [a, b], {}
    if stack is None:
        out_spec = pl.BlockSpec((tm, tn), lambda i, j, k: (i, j))
        out_shape = jax.ShapeDtypeStruct((m, n), out_dtype)
    else:
        buf, sidx = stack
        assert buf.shape[len(sidx):] == (m, n), (buf.shape, sidx, m, n)
        in_specs.append(_ANY)
        args.append(buf)
        extra = dict(input_output_aliases={2: 0})
        out_spec = pl.BlockSpec((None,) * len(sidx) + (tm, tn), lambda i, j, k: sidx + (i, j))
        out_shape = jax.ShapeDtypeStruct(buf.shape, buf.dtype)
    return _pcall(
        body, name=name, grid=(m // tm, n // tn, nk), in_specs=in_specs, out_specs=out_spec, out_shape=out_shape,
        scratch_shapes=[pltpu.VMEM((tm, tn) if nk > 1 else (8, LANE), F32)],
        compiler_params=_params(("parallel", "parallel", "arbitrary")), **extra)(*args)


class R:
    def __init__(self, arr, w, cb0=0, cstep=0):
        self.arr, self.w, self.cb0, self.cstep = arr, w, cb0, cstep


class Pm:
    def __init__(self, arr):
        self.arr = arr


class OR:
    def __init__(self, total, w, cstep=0, dtype=F32):
        self.total, self.w, self.cstep, self.dtype = total, w, cstep, dtype


class OA:
    def __init__(self, shape):
        self.shape = shape


def _rw(fn, ins, outs, *, name, ncol=1, tm=None):
    tm = TM_ROW if tm is None else tm
    nrow = LP // tm
    n_in = len(ins)

    def body(*refs):
        j, i = pl.program_id(0), pl.program_id(1)
        res = fn(*[r[...] for r in refs[:n_in]])
        for o, ref, val in zip(outs, refs[n_in:], res):
            if isinstance(o, OR):
                ref[...] = val.astype(ref.dtype)
            else:
                @pl.when((i == 0) & (j == 0))
                def _(ref=ref):
                    ref[...] = jnp.zeros_like(ref)

                ref[...] += val

    in_specs = []
    for s in ins:
        if isinstance(s, R):
            in_specs.append(pl.BlockSpec((tm, s.w), lambda j, i, s=s: (i, s.cb0 + j * s.cstep)))
        else:
            in_specs.append(pl.BlockSpec(s.arr.shape, lambda j, i, nd=s.arr.ndim: (0,) * nd))
    out_specs, out_shape = [], []
    for o in outs:
        if isinstance(o, OR):
            out_specs.append(pl.BlockSpec((tm, o.w), lambda j, i, o=o: (i, j * o.cstep)))
            out_shape.append(jax.ShapeDtypeStruct((LP, o.total), o.dtype))
        else:
            out_specs.append(pl.BlockSpec(o.shape, lambda j, i: (0, 0)))
            out_shape.append(jax.ShapeDtypeStruct(o.shape, F32))
    return _pcall(body, name=name, grid=(ncol, nrow), in_specs=in_specs, out_specs=out_specs, out_shape=out_shape,
                  compiler_params=_params(("arbitrary", "arbitrary")))(*[s.arr for s in ins])


def _rms(x, g):
    return x * lax.rsqrt(jnp.mean(x * x, axis=-1, keepdims=True) + EPS) * g


def _resid_norm(h, z, ga, gb):
    h2 = h + _rms(z, ga)
    return h2, _rms(h2, gb)


def _resid_norm_bwd(h, z, ga, gb, dh2, dhn2):
    _, vjp = jax.vjp(_resid_norm, h, z, ga, gb)
    return vjp((dh2, dhn2))


def _rms_bwd(h, g, dhn, dh_in):
    _, vjp = jax.vjp(_rms, h, g)
    dh, dg = vjp(dhn)
    return dh + dh_in, dg


def _loss_fn(h2, f, g4, lt, rmask):
    h3 = h2 + _rms(f, g4)
    err = jnp.square(h3 - lt)
    return 0.5 * jnp.sum(jnp.mean(err, axis=-1, keepdims=True) * rmask)


def _loss_bwd(h2, f, g4, lt, rmask):
    val, (dh2, df, dg4) = jax.value_and_grad(_loss_fn, argnums=(0, 1, 2))(h2, f, g4, lt, rmask)
    return dh2, df, dg4, jnp.broadcast_to(val, (1, LANE))


def _mla_prep(cq, ckv, kpe, krot, gq, gkv, cosp, sinp):
    return _rms(cq, gq), _rms(ckv, gkv), kpe * cosp + krot * sinp


def _mla_prep_bwd(cq, ckv, kpe, krot, gq, gkv, cosp, sinp, dcqn, dckvn, dkper):
    _, vjp = jax.vjp(lambda a, b, c, d, e, f: _mla_prep(a, b, c, d, e, f, cosp, sinp), cq, ckv, kpe, krot, gq, gkv)
    return vjp((dcqn, dckvn, dkper))


def _merge(g0, g1, g2, y0, y1, y2):
    return (jax.nn.sigmoid(g0) * y0 + jax.nn.sigmoid(g1) * y1 + jax.nn.sigmoid(g2) * y2,)


def _merge_bwd(g0, g1, g2, y0, y1, y2, dm):
    _, vjp = jax.vjp(_merge, g0, g1, g2, y0, y1, y2)
    return vjp((dm,))


def _swiglu(gu):
    g, u = gu[:, :D_FF], gu[:, D_FF:]
    return (g * jax.nn.sigmoid(g) * u,)


def _swiglu_bwd(gu, dact):
    _, vjp = jax.vjp(_swiglu, gu)
    return vjp((dact,))


def _causal_probs(s, iq):
    qpos = iq * TQ + lax.broadcasted_iota(jnp.int32, s.shape, 0)
    kpos = lax.broadcasted_iota(jnp.int32, s.shape, 1)
    s = jnp.where(kpos <= qpos, s, NEG_INF)
    e = jnp.exp(s - jnp.max(s, axis=-1, keepdims=True))
    return e / jnp.sum(e, axis=-1, keepdims=True)


def _softmax_vjp(p, dp):
    return p * (dp - jnp.sum(p * dp, axis=-1, keepdims=True))


def _mla_specs():
    return [pl.BlockSpec((TQ, QG), lambda h, i: (i, h)),
            pl.BlockSpec((LP, 2 * HD), lambda h, i: (0, h)),
            pl.BlockSpec((LP, LANE), lambda h, i: (0, 0)),
            pl.BlockSpec((TQ, LANE), lambda h, i: (i, 0)),
            pl.BlockSpec((TQ, LANE), lambda h, i: (i, 0))]


def _mla_parts(q_ref, kv_ref, cos_ref, sin_ref):
    q, kv = q_ref[...], kv_ref[...]
    qn = q[:, :HD]
    qp = q[:, HD:2 * HD] * cos_ref[...] + q[:, 2 * HD:] * sin_ref[...]
    return qn, qp, kv[:, :HD], kv[:, HD:]


def _mla_fwd(q, kv, kper, cosp, sinp):
    def body(q_ref, kv_ref, kp_ref, cos_ref, sin_ref, o_ref):
        qn, qp, kn, v = _mla_parts(q_ref, kv_ref, cos_ref, sin_ref)
        s = (_dot(qn, kn, _NT) + _dot(qp, kp_ref[...], _NT)) * MLA_SCALE
        o_ref[...] = _dot(_causal_probs(s, pl.program_id(1)), v, _NN).astype(o_ref.dtype)

    return _pcall(body, name="mla_fwd", grid=(HEADS, LP // TQ), in_specs=_mla_specs(),
                  out_specs=pl.BlockSpec((TQ, HD), lambda h, i: (i, h)),
                  out_shape=jax.ShapeDtypeStruct((LP, BW), BF16),
                  compiler_params=_params(("parallel", "parallel")))(q, kv, kper, cosp, sinp)


def _mla_bwd(q, kv, kper, cosp, sinp, do):
    def body(q_ref, kv_ref, kp_ref, cos_ref, sin_ref, do_ref, dq_ref, dkv_ref, dkp_ref):
        h, iq = pl.program_id(0), pl.program_id(1)
        qn, qp, kn, v = _mla_parts(q_ref, kv_ref, cos_ref, sin_ref)
        kp, dout = kp_ref[...], do_ref[...]
        p = _causal_probs((_dot(qn, kn, _NT) + _dot(qp, kp, _NT)) * MLA_SCALE, iq)
        ds = _softmax_vjp(p, _dot(dout, v, _NT)) * MLA_SCALE
        dqp = _dot(ds, kp, _NN)
        dq = jnp.concatenate([_dot(ds, kn, _NN), dqp * cos_ref[...], dqp * sin_ref[...]], axis=1)
        dq_ref[...] = dq.astype(dq_ref.dtype)

        @pl.when(iq == 0)
        def _():
            dkv_ref[...] = jnp.zeros_like(dkv_ref)

        dkv_ref[...] += jnp.concatenate([_dot(ds, qn, _TN), _dot(p, dout, _TN)], axis=1)

        @pl.when((iq == 0) & (h == 0))
        def _():
            dkp_ref[...] = jnp.zeros_like(dkp_ref)

        dkp_ref[...] += _dot(ds, qp, _TN)

    return _pcall(
        body, name="mla_bwd", grid=(HEADS, LP // TQ),
        in_specs=_mla_specs() + [pl.BlockSpec((TQ, HD), lambda h, i: (i, h))],
        out_specs=[pl.BlockSpec((TQ, QG), lambda h, i: (i, h)), pl.BlockSpec((LP, 2 * HD), lambda h, i: (0, h)),
                   pl.BlockSpec((LP, LANE), lambda h, i: (0, 0))],
        out_shape=[jax.ShapeDtypeStruct((LP, HEADS * QG), BF16), jax.ShapeDtypeStruct((LP, 2 * BW), F32),
                   jax.ShapeDtypeStruct((LP, LANE), F32)],
        compiler_params=_params(("arbitrary", "arbitrary")))(q, kv, kper, cosp, sinp, do)


def _fox_specs():
    cq, ck, cv = _cb(OFF_FOX, HD), _cb(OFF_FOX + BW, HD), _cb(OFF_FOX + 2 * BW, HD)
    return [pl.BlockSpec((TQ, HD), lambda h, i: (i, cq + h)),
            pl.BlockSpec((LP, HD), lambda h, i: (0, ck + h)),
            pl.BlockSpec((LP, HD), lambda h, i: (0, cv + h)),
            pl.BlockSpec((1, TQ, 1), lambda h, i: (h, i, 0)),
            pl.BlockSpec((1, 1, LP), lambda h, i: (h, 0, 0))]


def _fox_probs(q_ref, k_ref, cq_ref, ck_ref, iq):
    s = _dot(q_ref[...], k_ref[...], _NT) * FOX_SCALE + (cq_ref[0] - ck_ref[0])
    return _causal_probs(s, iq)


def _fox_fwd(p_all, cq3, ck3):
    def body(q_ref, k_ref, v_ref, cq_ref, ck_ref, o_ref):
        p = _fox_probs(q_ref, k_ref, cq_ref, ck_ref, pl.program_id(1))
        o_ref[...] = _dot(p, v_ref[...], _NN).astype(o_ref.dtype)

    return _pcall(body, name="fox_fwd", grid=(HEADS, LP // TQ), in_specs=_fox_specs(),
                  out_specs=pl.BlockSpec((TQ, HD), lambda h, i: (i, h)),
                  out_shape=jax.ShapeDtypeStruct((LP, BW), BF16),
                  compiler_params=_params(("parallel", "parallel")))(p_all, p_all, p_all, cq3, ck3)


def _fox_bwd(p_all, cq3, ck3, do):
    def body(q_ref, k_ref, v_ref, cq_ref, ck_ref, do_ref, dq_ref, dk_ref, dv_ref, dcq_ref, dck_ref):
        iq = pl.program_id(1)
        p = _fox_probs(q_ref, k_ref, cq_ref, ck_ref, iq)
        dout = do_ref[...]
        ds = _softmax_vjp(p, _dot(dout, v_ref[...], _NT))
        dss = ds * FOX_SCALE
        dq_ref[...] = _dot(dss, k_ref[...], _NN).astype(dq_ref.dtype)
        dcq_ref[0] = jnp.sum(ds, axis=1, keepdims=True)

        @pl.when(iq == 0)
        def _():
            dk_ref[...] = jnp.zeros_like(dk_ref)
            dv_ref[...] = jnp.zeros_like(dv_ref)
            dck_ref[...] = jnp.zeros_like(dck_ref)

        dk_ref[...] += _dot(dss, q_ref[...], _TN)
        dv_ref[...] += _dot(p, dout, _TN)
        dck_ref[0] -= jnp.sum(ds, axis=0, keepdims=True)

    head_rows = pl.BlockSpec((TQ, HD), lambda h, i: (i, h))
    head_all = pl.BlockSpec((LP, HD), lambda h, i: (0, h))
    return _pcall(
        body, name="fox_bwd", grid=(HEADS, LP // TQ), in_specs=_fox_specs() + [head_rows],
        out_specs=[head_rows, head_all, head_all, pl.BlockSpec((1, TQ, 1), lambda h, i: (h, i, 0)),
                   pl.BlockSpec((1, 1, LP), lambda h, i: (h, 0, 0))],
        out_shape=[jax.ShapeDtypeStruct((LP, BW), BF16)] + [jax.ShapeDtypeStruct((LP, BW), F32)] * 2
        + [jax.ShapeDtypeStruct((HEADS, LP, 1), F32), jax.ShapeDtypeStruct((HEADS, 1, LP), F32)],
        compiler_params=_params(("arbitrary", "arbitrary")))(p_all, p_all, p_all, cq3, ck3, do)


def _log_sigmoid(x):
    return jnp.minimum(x, 0.0) - jnp.log(1.0 + jnp.exp(-jnp.abs(x)))


def _decay_fwd(p_all, b_pad):
    tc = TM_ROW

    def body(fl_ref, b_ref, c_ref):
        lf = _log_sigmoid(fl_ref[...] + b_ref[...])
        r = pl.program_id(0) * tc + lax.broadcasted_iota(jnp.int32, (tc, LP), 0)
        s = lax.broadcasted_iota(jnp.int32, (tc, LP), 1)
        c_ref[...] = jnp.dot((s <= r).astype(F32), lf, precision=lax.Precision.HIGHEST, preferred_element_type=F32)

    return _pcall(body, name="decay_fwd", grid=(LP // tc,),
                  in_specs=[pl.BlockSpec((LP, LANE), lambda i: (0, _cb(OFF_FL, LANE))),
                            pl.BlockSpec((1, LANE), lambda i: (0, 0))],
                  out_specs=pl.BlockSpec((tc, LANE), lambda i: (i, 0)),
                  out_shape=jax.ShapeDtypeStruct((LP, LANE), F32), compiler_params=_params(("parallel",)))(p_all, b_pad)


def _decay_bwd(p_all, b_pad, dc):
    tc = TM_ROW

    def body(fl_ref, b_ref, dc_ref, dfl_ref, db_ref):
        i = pl.program_id(0)
        r = i * tc + lax.broadcasted_iota(jnp.int32, (tc, LP), 0)
        t = lax.broadcasted_iota(jnp.int32, (tc, LP), 1)
        dlf = jnp.dot((t >= r).astype(F32), dc_ref[...], precision=lax.Precision.HIGHEST, preferred_element_type=F32)
        dfl = dlf * jax.nn.sigmoid(-(fl_ref[...] + b_ref[...]))
        dfl_ref[...] = dfl.astype(dfl_ref.dtype)

        @pl.when(i == 0)
        def _():
            db_ref[...] = jnp.zeros_like(db_ref)

        db_ref[...] += jnp.sum(dfl, axis=0, keepdims=True)

    return _pcall(body, name="decay_bwd", grid=(LP // tc,),
                  in_specs=[pl.BlockSpec((tc, LANE), lambda i: (i, _cb(OFF_FL, LANE))),
                            pl.BlockSpec((1, LANE), lambda i: (0, 0)), pl.BlockSpec((LP, LANE), lambda i: (0, 0))],
                  out_specs=[pl.BlockSpec((tc, LANE), lambda i: (i, 0)), pl.BlockSpec((1, LANE), lambda i: (0, 0))],
                  out_shape=[jax.ShapeDtypeStruct((LP, LANE), BF16), jax.ShapeDtypeStruct((1, LANE), F32)],
                  compiler_params=_params(("arbitrary",)))(p_all, b_pad, dc)


def _conv_specs(layer):
    c0 = _cb(OFF_CONV, CW)
    step = BW // CW
    return [pl.BlockSpec((LP, CW), lambda j: (0, c0 + j)), pl.BlockSpec((LP, CW), lambda j: (0, c0 + step + j)),
            pl.BlockSpec((LP, CW), lambda j: (0, c0 + 2 * step + j)),
            pl.BlockSpec((None, 3, CW), lambda j: (layer, 0, j))]


def _shift_down(x, k, t):
    return jnp.where(t >= k, pltpu.roll(x, k, 0), 0.0)


def _shift_up(x, k, t):
    return jnp.where(t < LP - k, pltpu.roll(x, LP - k, 0), 0.0)


def _conv_fwd(p_all, cw, layer):
    def body(b_ref, c_ref, x_ref, w_ref, o_ref):
        t = lax.broadcasted_iota(jnp.int32, (LP, CW), 0)
        uu = c_ref[...] * x_ref[...]
        w = w_ref[...]
        u = w[2:3] * uu + w[1:2] * _shift_down(uu, 1, t) + w[0:1] * _shift_down(uu, 2, t)
        o_ref[...] = (b_ref[...] * u).astype(o_ref.dtype)

    return _pcall(body, name="conv_fwd", grid=(BW // CW,), in_specs=_conv_specs(layer),
                  out_specs=pl.BlockSpec((LP, CW), lambda j: (0, j)), out_shape=jax.ShapeDtypeStruct((LP, BW), BF16),
                  compiler_params=_params(("parallel",)))(p_all, p_all, p_all, cw)


def _conv_bwd(p_all, cw, layer, do):
    def body(b_ref, c_ref, x_ref, w_ref, do_ref, d_ref, dw_ref):
        t = lax.broadcasted_iota(jnp.int32, (LP, CW), 0)
        cc, xx, w, dout = c_ref[...], x_ref[...], w_ref[...], do_ref[...]
        uu = cc * xx
        s1, s2 = _shift_down(uu, 1, t), _shift_down(uu, 2, t)
        u = w[2:3] * uu + w[1:2] * s1 + w[0:1] * s2
        du = dout * b_ref[...]
        duu = w[2:3] * du + w[1:2] * _shift_up(du, 1, t) + w[0:1] * _shift_up(du, 2, t)
        d_ref[0] = (dout * u).astype(d_ref.dtype)
        d_ref[1] = (duu * xx).astype(d_ref.dtype)
        d_ref[2] = (duu * cc).astype(d_ref.dtype)
        dw_ref[0:1, :] = jnp.sum(du * s2, axis=0, keepdims=True)
        dw_ref[1:2, :] = jnp.sum(du * s1, axis=0, keepdims=True)
        dw_ref[2:3, :] = jnp.sum(du * uu, axis=0, keepdims=True)

    return _pcall(body, name="conv_bwd", grid=(BW // CW,),
                  in_specs=_conv_specs(layer) + [pl.BlockSpec((LP, CW), lambda j: (0, j))],
                  out_specs=[pl.BlockSpec((3, LP, CW), lambda j: (0, 0, j)), pl.BlockSpec((3, CW), lambda j: (0, j))],
                  out_shape=[jax.ShapeDtypeStruct((3, LP, BW), BF16), jax.ShapeDtypeStruct((3, BW), F32)],
                  compiler_params=_params(("parallel",)))(p_all, p_all, p_all, cw, do)


def _place():
    return lax.axis_index("x"), lax.axis_index("y"), lax.axis_index("c")


def _dest_slice(ref, kind, d):
    if kind == "raw":
        return ref.at[d]
    nd = len(ref.shape)
    if kind == "col":
        w = ref.shape[-1] // N_DEV
        return ref.at[(slice(None),) * (nd - 1) + (pl.ds(d * w, w),)]
    r = ref.shape[-2] // N_DEV
    return ref.at[(slice(None),) * (nd - 2) + (pl.ds(d * r, r), slice(None))]


def _unsharded_shape(shape, kind, lead=N_DEV):
    if kind == "raw":
        return (lead,) + shape
    if kind == "col":
        return shape[:-1] + (N_DEV * shape[-1],)
    return shape[:-2] + (N_DEV * shape[-2], shape[-1])


def _all_gather(arrs, kinds, name):
    n = len(arrs)

    def body(*refs):
        ins, outs = refs[:n], refs[n:2 * n]
        send, recv, loc = refs[2 * n:]
        x, y, c = _place()
        sib = (x, y, 1 - c)
        chips = [(1 - x, y), (x, 1 - y), (1 - x, 1 - y)]

        def idx(px, py, pc):
            return 4 * px + 2 * py + pc

        def copy(j, k, block, to, src=None):
            dst = _dest_slice(outs[k], kinds[k], idx(*block))
            return pltpu.make_async_remote_copy(src_ref=dst if src is None else src, dst_ref=dst, send_sem=send.at[j, k],
                                                recv_sem=recv.at[j, k], device_id=to, device_id_type=MESH)

        me = (x, y, c)
        mine = [pltpu.make_async_copy(ins[k], _dest_slice(outs[k], kinds[k], idx(*me)), loc.at[k]) for k in range(n)]
        for cp in mine:
            cp.start()
        first = [copy(0, k, me, sib, src=ins[k]) for k in range(n)]
        first += [copy(1 + j, k, me, (*chip, c), src=ins[k]) for j, chip in enumerate(chips) for k in range(n)]
        for cp in first:
            cp.start()
        passed = []
        for j, chip in enumerate(chips):
            for k in range(n):
                copy(1 + j, k, (*chip, c), me, src=ins[k]).wait_recv()
                cp = copy(4 + j, k, (*chip, c), sib)
                cp.start()
                passed.append(cp)
        for k in range(n):
            copy(0, k, sib, me, src=ins[k]).wait_recv()
        for j, chip in enumerate(chips):
            for k in range(n):
                copy(4 + j, k, (*chip, 1 - c), me).wait_recv()
        for cp in first + passed:
            cp.wait_send()
        for cp in mine:
            cp.wait()

    return _pcall(body, name=name, in_specs=[_ANY] * n, out_specs=[_ANY] * n,
                  out_shape=[jax.ShapeDtypeStruct(_unsharded_shape(a.shape, kd), a.dtype) for a, kd in zip(arrs, kinds)],
                  scratch_shapes=[pltpu.SemaphoreType.DMA((7, n)), pltpu.SemaphoreType.DMA((7, n)),
                                  pltpu.SemaphoreType.DMA((n,))])(*arrs)


def _shard_shape(a, kind):
    if kind == "raw":
        return a.shape[2:]
    if kind == "col":
        return a.shape[:-1] + (a.shape[-1] // N_DEV,)
    return a.shape[:-2] + (a.shape[-2] // N_DEV, a.shape[-1])


def _swap_sibling(arrs, kinds, name):
    n = len(arrs)
    npieces = sum(1 if kd == "raw" else 4 for kd in kinds)

    def body(*refs):
        ins, outs = refs[:n], refs[n:2 * n]
        send, recv = refs[2 * n:]
        x, y, c = _place()
        pieces = []
        for k in range(n):
            if kinds[k] == "raw":
                pieces.append((ins[k].at[1 - c], outs[k]))
            else:
                pieces += [(_dest_slice(ins[k], kinds[k], 2 * p + 1 - c), outs[k].at[p]) for p in range(4)]
        cps = [pltpu.make_async_remote_copy(src_ref=src, dst_ref=dst, send_sem=send.at[i], recv_sem=recv.at[i],
                                            device_id=(x, y, 1 - c), device_id_type=MESH)
               for i, (src, dst) in enumerate(pieces)]
        for cp in cps:
            cp.start()
        for cp in cps:
            cp.wait()

    return _pcall(body, name=name, in_specs=[_ANY] * n, out_specs=[_ANY] * n,
                  out_shape=[jax.ShapeDtypeStruct((4,) + _shard_shape(a, kd), a.dtype) for a, kd in zip(arrs, kinds)],
                  scratch_shapes=[pltpu.SemaphoreType.DMA((npieces,)), pltpu.SemaphoreType.DMA((npieces,))])(*arrs)


def _swap_chips(arrs, name):
    n = len(arrs)

    def body(*refs):
        ins, outs = refs[:n], refs[n:2 * n]
        send, recv, loc = refs[2 * n:]
        x, y, c = _place()
        mychip = 2 * x + y
        mine = [pltpu.make_async_copy(ins[k].at[mychip], outs[k].at[mychip], loc.at[k]) for k in range(n)]
        for cp in mine:
            cp.start()
        cps = []
        for j, (cx, cy) in enumerate([(1 - x, y), (x, 1 - y), (1 - x, 1 - y)]):
            for k in range(n):
                cps.append(pltpu.make_async_remote_copy(
                    src_ref=ins[k].at[2 * cx + cy], dst_ref=outs[k].at[mychip], send_sem=send.at[j, k],
                    recv_sem=recv.at[j, k], device_id=(cx, cy, c), device_id_type=MESH))
        for cp in cps:
            cp.start()
        for cp in cps:
            cp.wait()
        for cp in mine:
            cp.wait()

    return _pcall(body, name=name, in_specs=[_ANY] * n, out_specs=[_ANY] * n,
                  out_shape=[jax.ShapeDtypeStruct(a.shape, a.dtype) for a in arrs],
                  scratch_shapes=[pltpu.SemaphoreType.DMA((3, n)), pltpu.SemaphoreType.DMA((3, n)),
                                  pltpu.SemaphoreType.DMA((n,))])(*arrs)


def _slice_shape(shape, kind):
    if kind == "raw":
        return shape[1:]
    if kind == "col":
        return shape[:-1] + (shape[-1] // N_DEV,)
    return shape[:-2] + (shape[-2] // N_DEV, shape[-1])


def _all_to_all(arrs, src_kinds, dst_kinds, name):
    n = len(arrs)

    def body(*refs):
        ins, outs = refs[:n], refs[n:2 * n]
        send, recv, loc = refs[2 * n:]
        x, y, c = _place()
        me = 4 * x + 2 * y + c
        mine = [pltpu.make_async_copy(_dest_slice(ins[k], src_kinds[k], me), _dest_slice(outs[k], dst_kinds[k], me),
                                      loc.at[k]) for k in range(n)]
        for cp in mine:
            cp.start()
        cps = []
        for r in range(1, N_DEV):
            px, py, pc = (1 - x if r & 4 else x), (1 - y if r & 2 else y), (1 - c if r & 1 else c)
            for k in range(n):
                cps.append(pltpu.make_async_remote_copy(
                    src_ref=_dest_slice(ins[k], src_kinds[k], 4 * px + 2 * py + pc),
                    dst_ref=_dest_slice(outs[k], dst_kinds[k], me), send_sem=send.at[r - 1, k],
                    recv_sem=recv.at[r - 1, k], device_id=(px, py, pc), device_id_type=MESH))
        for cp in cps:
            cp.start()
        for cp in cps:
            cp.wait()
        for cp in mine:
            cp.wait()

    out_shape = [jax.ShapeDtypeStruct(_unsharded_shape(_slice_shape(a.shape, sk), dk), a.dtype)
                 for a, sk, dk in zip(arrs, src_kinds, dst_kinds)]
    return _pcall(body, name=name, in_specs=[_ANY] * n, out_specs=[_ANY] * n, out_shape=out_shape,
                  scratch_shapes=[pltpu.SemaphoreType.DMA((N_DEV - 1, n)), pltpu.SemaphoreType.DMA((N_DEV - 1, n)),
                                  pltpu.SemaphoreType.DMA((n,))])(*arrs)


def _rows_tile(rows, cols=0):
    cap = 512 * 1024
    return _pick(rows, [t for t in (128, 64, 32, 16) if t * cols <= cap])


def _pair_sum(g, r1, core, kind, name):
    _, groups, rows, cols = r1.shape
    tr = _rows_tile(rows, cols)
    nr = rows // tr
    if kind == "raw":
        g_spec = pl.BlockSpec((None, None, None, tr, cols), lambda p, q, i, c: (c[0], p, q, i, 0))
    elif kind == "col":
        g_spec = pl.BlockSpec((None, tr, cols), lambda p, q, i, c: (q, i, 2 * p + c[0]))
    else:
        g_spec = pl.BlockSpec((None, tr, cols), lambda p, q, i, c: (q, (2 * p + c[0]) * nr + i, 0))
    r_spec = pl.BlockSpec((None, None, tr, cols), lambda p, q, i, c: (p, q, i, 0))

    def body(c_ref, g_ref, r_ref, o_ref):
        o_ref[...] = (g_ref[...].astype(F32) + r_ref[...].astype(F32)).astype(o_ref.dtype)

    return _pcall(
        body, name=name,
        grid_spec=pltpu.PrefetchScalarGridSpec(num_scalar_prefetch=1, grid=(4, groups, nr), in_specs=[g_spec, r_spec],
                                               out_specs=r_spec),
        out_shape=jax.ShapeDtypeStruct(r1.shape, r1.dtype),
        compiler_params=_params(("parallel", "parallel", "parallel")))(core, g, r1)


def _sum_parts(parts, name):
    npart, groups, rows, cols = parts.shape
    tr = _rows_tile(rows, cols)

    def body(p_ref, o_ref):
        g = p_ref[0].astype(F32)
        for k in range(1, npart):
            g = g + p_ref[k].astype(F32)
        o_ref[...] = g

    return _pcall(body, name=name, grid=(groups, rows // tr),
                  in_specs=[pl.BlockSpec((npart, None, tr, cols), lambda q, i: (0, q, i, 0))],
                  out_specs=pl.BlockSpec((None, tr, cols), lambda q, i: (q, i, 0)),
                  out_shape=jax.ShapeDtypeStruct((groups, rows, cols), F32),
                  compiler_params=_params(("parallel", "parallel")))(parts)


def _adamw(g, w, m, v):
    m = ADAM_B1 * m + (1.0 - ADAM_B1) * g
    v = ADAM_B2 * v + (1.0 - ADAM_B2) * jnp.square(g)
    m_hat = m / (1.0 - ADAM_B1 ** ADAM_STEP)
    v_hat = v / (1.0 - ADAM_B2 ** ADAM_STEP)
    return -ADAM_LR * (m_hat / (jnp.sqrt(v_hat) + ADAM_EPS) + ADAM_WD * w), m, v


def _sum_adamw(parts, w, m, v, name):
    npart, groups, rows, cols = parts.shape
    tr = _rows_tile(rows, cols)

    def body(p_ref, w_ref, m_ref, v_ref, g_ref, d_ref, nm_ref, nv_ref):
        g = p_ref[0].astype(F32)
        for k in range(1, npart):
            g = g + p_ref[k].astype(F32)
        g_ref[...] = g
        d_ref[...], nm_ref[...], nv_ref[...] = _adamw(g, w_ref[...], m_ref[...], v_ref[...])

    blk = pl.BlockSpec((None, tr, cols), lambda q, i: (q, i, 0))
    return _pcall(body, name=name, grid=(groups, rows // tr),
                  in_specs=[pl.BlockSpec((npart, None, tr, cols), lambda q, i: (0, q, i, 0)), blk, blk, blk],
                  out_specs=[blk] * 4, out_shape=[jax.ShapeDtypeStruct((groups, rows, cols), F32)] * 4,
                  compiler_params=_params(("parallel", "parallel")))(parts, w, m, v)


def _rot_cols(w):
    return jnp.concatenate([-w[..., ROPE // 2:], w[..., :ROPE // 2]], axis=-1)


def _rot_cols_t(dw):
    return jnp.concatenate([dw[..., ROPE // 2:], -dw[..., :ROPE // 2]], axis=-1)


_IN_SPLITS = np.cumsum([0, Q_RANK, KV_RANK, ROPE, BW, BW, BW, BW, BW, BW, HEADS, 3 * D_MODEL])


def _ext_w_in(w):
    o = _IN_SPLITS
    kpe = w[..., o[2]:o[3]]
    z = lambda n: jnp.zeros(w.shape[:-1] + (n,), w.dtype)
    return jnp.concatenate([w[..., o[10]:o[11]], w[..., o[3]:o[9]], w[..., o[0]:o[2]], kpe, z(LANE - ROPE),
                            _rot_cols(kpe), z(LANE - ROPE), w[..., o[9]:o[10]], z(2 * LANE - HEADS)], axis=-1)


def _unext_w_in(dw):
    kpe = dw[..., OFF_KPE:OFF_KPE + ROPE] + _rot_cols_t(dw[..., OFF_KROT:OFF_KROT + ROPE])
    return jnp.concatenate([dw[..., OFF_CQ:OFF_KPE], kpe, dw[..., OFF_CONV:OFF_CQ], dw[..., OFF_FL:OFF_FL + HEADS],
                            dw[..., OFF_GATE:OFF_CONV]], axis=-1)


def _ext_w_uq(w):
    w3 = w.reshape(w.shape[:-1] + (HEADS, HD + ROPE))
    pe = w3[..., HD:]
    z = jnp.zeros(w3.shape[:-1] + (LANE - ROPE,), w.dtype)
    return jnp.concatenate([w3[..., :HD], pe, z, _rot_cols(pe), z], axis=-1).reshape(w.shape[:-1] + (HEADS * QG,))


def _unext_w_uq(dw):
    d3 = dw.reshape(dw.shape[:-1] + (HEADS, QG))
    pe = d3[..., HD:HD + ROPE] + _rot_cols_t(d3[..., 2 * HD:2 * HD + ROPE])
    return jnp.concatenate([d3[..., :HD], pe], axis=-1).reshape(dw.shape[:-1] + (HEADS * (HD + ROPE),))


_BIG = (("meta", "col", F32), ("w_in", "row", BF16), ("w_uq", "row", BF16), ("w_ukv", "col", BF16),
        ("conv_w", "col", F32), ("w_branch", "col", BF16), ("w_out", "row", BF16), ("w_ffn_in", "col", BF16),
        ("w_ffn_out", "row", BF16))
_KINDS = tuple(kd for _, kd, _ in _BIG)
_SMALL = ("b_forget", "g_q_lat", "g_kv_lat", "g_mix_pre", "g_mix_post", "g_ffn_pre", "g_ffn_post")
_ORDER = ("meta", "w_in", "b_forget", "g_q_lat", "g_kv_lat", "w_uq", "w_ukv", "conv_w", "w_branch", "w_out",
          "w_ffn_in", "w_ffn_out", "g_mix_pre", "g_mix_post", "g_ffn_pre", "g_ffn_post")


def _unshard_cols(g):
    g = jnp.moveaxis(g, 0, -2)
    return g.reshape(g.shape[:-2] + (g.shape[-2] * g.shape[-1],))


def _as3d(a, lead=0):
    return a.reshape(a.shape[:lead] + (-1,) + a.shape[-2:])


def kernel(x, meta, w_in, b_forget, g_q_lat, g_kv_lat, w_uq, w_ukv, conv_w, w_branch, w_out, w_ffn_in, w_ffn_out, g_mix_pre, g_mix_post, g_ffn_pre, g_ffn_post, loss_target, m_meta, m_w_in, m_b_forget, m_g_q_lat, m_g_kv_lat, m_w_uq, m_w_ukv, m_conv_w, m_w_branch, m_w_out, m_w_ffn_in, m_w_ffn_out, m_g_mix_pre, m_g_mix_post, m_g_ffn_pre, m_g_ffn_post, v_meta, v_w_in, v_b_forget, v_g_q_lat, v_g_kv_lat, v_w_uq, v_w_ukv, v_conv_w, v_w_branch, v_w_out, v_w_ffn_in, v_w_ffn_out, v_g_mix_pre, v_g_mix_post, v_g_ffn_pre, v_g_ffn_post):
    given = dict(locals())
    core = lax.axis_index("c").astype(jnp.int32).reshape(1)

    rows_in, rows_uq = _all_to_all([w_in.astype(BF16), w_uq.astype(BF16)], ("row", "row"), ("raw", "raw"),
                                   "rows_of_column_shards")
    shards = {n: given[n].astype(dt) for n, _, dt in _BIG}
    shards["w_in"] = _ext_w_in(_unshard_cols(rows_in))
    shards["w_uq"] = _ext_w_uq(_unshard_cols(rows_uq))
    gathered = _all_gather([shards[n] for n, _, _ in _BIG], _KINDS, "gather_weights")
    full = {n: g for (n, _, _), g in zip(_BIG, gathered)}

    pos = jnp.arange(LP, dtype=F32)[:, None]
    inv_freq = 1.0 / (ROPE_THETA ** (jnp.arange(0, ROPE, 2, dtype=F32) / ROPE))
    ang = pos * inv_freq[None, :]
    zpad = jnp.zeros((LP, LANE - ROPE), F32)
    cosp = jnp.concatenate([jnp.cos(ang), jnp.cos(ang), zpad], axis=1)
    sinp = jnp.concatenate([jnp.sin(ang), jnp.sin(ang), zpad], axis=1)

    tail = jnp.zeros((LP - L_TOK, D_MODEL), F32)
    h = jnp.concatenate([full["meta"], x[0], tail], axis=0)
    ltp = jnp.concatenate([jnp.zeros((N_META, D_MODEL), F32), loss_target[0], tail], axis=0)
    row = jnp.arange(LP)[:, None]
    rmask = ((row >= N_META) & (row < L_TOK)).astype(F32)

    def vec(a, l):
        return a[l][None, :]

    wl = []
    for l in range(DEPTH):
        wl.append(dict(
            b_pad=jnp.concatenate([b_forget[l], jnp.zeros((LANE - HEADS,), F32)])[None, :],
            gq=vec(g_q_lat, l), gkv=vec(g_kv_lat, l), g1=vec(g_mix_pre, l), g2=vec(g_mix_post, l),
            g3=vec(g_ffn_pre, l), g4=vec(g_ffn_post, l)))

    def prep_ins(p_all, w):
        return [R(p_all, Q_RANK, _cb(OFF_CQ, Q_RANK)), R(p_all, KV_RANK, _cb(OFF_CKV, KV_RANK)),
                R(p_all, LANE, _cb(OFF_KPE, LANE)), R(p_all, LANE, _cb(OFF_KROT, LANE)), Pm(w["gq"]), Pm(w["gkv"]),
                R(cosp, LANE), R(sinp, LANE)]

    def merge_ins(p_all, ys):
        return [R(p_all, GW, _cb(OFF_GATE + n * D_MODEL, GW), 1) for n in range(3)] + [R(yv, GW, 0, 1) for yv in ys]

    def b16(total, w, cstep=0):
        return OR(total, w, cstep, BF16)

    (hn,) = _rw(lambda a, g: (_rms(a, g),), [R(h, D_MODEL), Pm(wl[0]["g1"])], [b16(D_MODEL, D_MODEL)], name="rms_in")
    saved = []
    for l in range(DEPTH):
        w = wl[l]
        s = dict(h=h, hn=hn)
        p_all = _mm(hn, full["w_in"], bidx=(l,), name="proj_in")
        cqn, ckvn, kper = _rw(_mla_prep, prep_ins(p_all, w),
                              [b16(Q_RANK, Q_RANK), b16(KV_RANK, KV_RANK), OR(LANE, LANE)], name="mla_prep")
        q = _mm(cqn, full["w_uq"], bidx=(l,), name="proj_q")
        kv = _mm(ckvn, full["w_ukv"], bidx=(l,), name="proj_kv")
        o_a = _mla_fwd(q, kv, kper, cosp, sinp)
        o_b = _conv_fwd(p_all, full["conv_w"], l)
        cdec = _decay_fwd(p_all, w["b_pad"])
        cq3 = cdec[:, :HEADS].T[:, :, None]
        ck3 = cdec[:, :HEADS].T[:, None, :]
        o_c = _fox_fwd(p_all, cq3, ck3)
        outs = (o_a, o_b, o_c)
        ys = [_mm(outs[n], full["w_branch"], bidx=(l, n), name="proj_branch") for n in range(3)]
        (merged,) = _rw(_merge, merge_ins(p_all, ys), [b16(D_MODEL, GW, 1)], ncol=D_MODEL // GW, name="merge")
        mix = _mm(merged, full["w_out"], bidx=(l,), name="proj_out")
        h2, hn2 = _rw(_resid_norm, [R(h, D_MODEL), R(mix, D_MODEL), Pm(w["g2"]), Pm(w["g3"])],
                      [OR(D_MODEL, D_MODEL), b16(D_MODEL, D_MODEL)], name="resid_norm")
        gu = _mm(hn2, full["w_ffn_in"], bidx=(l,), name="ffn_in")
        (act,) = _rw(_swiglu, [R(gu, 2 * D_FF)], [b16(D_FF, D_FF)], tm=TM_FF, name="swiglu")
        f = _mm(act, full["w_ffn_out"], bidx=(l,), name="ffn_out")
        s.update(p_all=p_all, cqn=cqn, ckvn=ckvn, kper=kper, q=q, kv=kv, outs=outs, cq3=cq3, ck3=ck3, ys=ys,
                 merged=merged, mix=mix, h2=h2, hn2=hn2, gu=gu, act=act, f=f)
        saved.append(s)
        if l + 1 < DEPTH:
            h, hn = _rw(_resid_norm, [R(h2, D_MODEL), R(f, D_MODEL), Pm(w["g4"]), Pm(wl[l + 1]["g1"])],
                        [OR(D_MODEL, D_MODEL), b16(D_MODEL, D_MODEL)], name="resid_norm")

    grads = {n: [None] * DEPTH for n in _SMALL + ("conv_w",)}
    gst = {n: lax.empty(full[n].shape, dt) for n, _, dt in _BIG if n not in ("meta", "conv_w")}
    s, w = saved[-1], wl[-1]
    dh2, df, dg4, loss_acc = _rw(
        _loss_bwd, [R(s["h2"], D_MODEL), R(s["f"], D_MODEL), Pm(w["g4"]), R(ltp, D_MODEL), R(rmask, 1)],
        [OR(D_MODEL, D_MODEL), b16(D_MODEL, D_MODEL), OA((1, D_MODEL)), OA((1, LANE))], name="loss_bwd")
    loss = lax.psum(loss_acc[0, 0], ("x", "y", "c"))
    grads["g_ffn_post"][DEPTH - 1] = dg4[0]
    for l in reversed(range(DEPTH)):
        s, w = saved[l], wl[l]
        p_all = s["p_all"]
        dact = _mm(df, full["w_ffn_out"], tb=True, bidx=(l,), name="d_act")
        gst["w_ffn_out"] = _mm(s["act"], df, ta=True, stack=(gst["w_ffn_out"], (l,)), name="dw_ffn_out")
        (dgu,) = _rw(_swiglu_bwd, [R(s["gu"], 2 * D_FF), R(dact, D_FF)], [b16(2 * D_FF, 2 * D_FF)], tm=TM_FF,
                     name="swiglu_bwd")
        dhn2 = _mm(dgu, full["w_ffn_in"], tb=True, bidx=(l,), name="d_hn2")
        gst["w_ffn_in"] = _mm(s["hn2"], dgu, ta=True, stack=(gst["w_ffn_in"], (l,)), name="dw_ffn_in")
        dh, dmix, dg2, dg3 = _rw(
            _resid_norm_bwd, [R(s["h"], D_MODEL), R(s["mix"], D_MODEL), Pm(w["g2"]), Pm(w["g3"]), R(dh2, D_MODEL),
                              R(dhn2, D_MODEL)],
            [OR(D_MODEL, D_MODEL), b16(D_MODEL, D_MODEL), OA((1, D_MODEL)), OA((1, D_MODEL))], name="resid_norm_bwd")
        grads["g_mix_post"][l], grads["g_ffn_pre"][l] = dg2[0], dg3[0]
        dmerged = _mm(dmix, full["w_out"], tb=True, bidx=(l,), name="d_merged")
        gst["w_out"] = _mm(s["merged"], dmix, ta=True, stack=(gst["w_out"], (l,)), name="dw_out")
        mb = _rw(_merge_bwd, merge_ins(p_all, s["ys"]) + [R(dmerged, GW, 0, 1)], [b16(D_MODEL, GW, 1)] * 6,
                 ncol=D_MODEL // GW, name="merge_bwd")
        dgate, dys = mb[:3], mb[3:]
        dos = [_mm(dys[n], full["w_branch"], tb=True, bidx=(l, n), name="d_branch") for n in range(3)]
        for n in range(3):
            gst["w_branch"] = _mm(s["outs"][n], dys[n], ta=True, stack=(gst["w_branch"], (l, n)), name="dw_branch")
        dfq, dfk, dfv, dcq3, dck3 = _fox_bwd(p_all, s["cq3"], s["ck3"], dos[2])
        dc = jnp.concatenate([dcq3[:, :, 0].T + dck3[:, 0, :].T, jnp.zeros((LP, LANE - HEADS), F32)], axis=1)
        dfl, db = _decay_bwd(p_all, w["b_pad"], dc)
        grads["b_forget"][l] = db[0, :HEADS]
        dconv, dcw = _conv_bwd(p_all, full["conv_w"], l, dos[1])
        grads["conv_w"][l] = dcw
        dq, dkv, dkper = _mla_bwd(s["q"], s["kv"], s["kper"], cosp, sinp, dos[0])
        dcqn = _mm(dq, full["w_uq"], tb=True, bidx=(l,), name="d_cqn")
        gst["w_uq"] = _mm(s["cqn"], dq, ta=True, stack=(gst["w_uq"], (l,)), name="dw_uq")
        dckvn = _mm(dkv, full["w_ukv"], tb=True, bidx=(l,), name="d_ckvn")
        gst["w_ukv"] = _mm(s["ckvn"], dkv, ta=True, stack=(gst["w_ukv"], (l,)), name="dw_ukv")
        dcq, dckv, dkpe, dkrot, dgq, dgkv = _rw(
            _mla_prep_bwd, prep_ins(p_all, w) + [R(dcqn, Q_RANK), R(dckvn, KV_RANK), R(dkper, LANE)],
            [b16(Q_RANK, Q_RANK), b16(KV_RANK, KV_RANK), b16(LANE, LANE), b16(LANE, LANE), OA((1, Q_RANK)),
             OA((1, KV_RANK))], name="mla_prep_bwd")
        grads["g_q_lat"][l], grads["g_kv_lat"][l] = dgq[0], dgkv[0]
        dp = jnp.concatenate([*dgate, dconv[0], dconv[1], dconv[2], dfq, dfk.astype(BF16), dfv.astype(BF16), dcq, dckv,
                              dkpe, dkrot, dfl, jnp.zeros((LP, LANE), BF16)], axis=1)
        dhn = _mm(dp, full["w_in"], tb=True, bidx=(l,), name="d_hn")
        gst["w_in"] = _mm(s["hn"], dp, ta=True, stack=(gst["w_in"], (l,)), name="dw_in")
        if l > 0:
            sp, wp = saved[l - 1], wl[l - 1]
            dh2, df, dg4, dg1 = _rw(
                _resid_norm_bwd, [R(sp["h2"], D_MODEL), R(sp["f"], D_MODEL), Pm(wp["g4"]), Pm(w["g1"]),
                                  R(dh, D_MODEL), R(dhn, D_MODEL)],
                [OR(D_MODEL, D_MODEL), b16(D_MODEL, D_MODEL), OA((1, D_MODEL)), OA((1, D_MODEL))], name="resid_norm_bwd")
            grads["g_ffn_post"][l - 1], grads["g_mix_pre"][l] = dg4[0], dg1[0]
        else:
            dh0, dg1 = _rw(_rms_bwd, [R(s["h"], D_MODEL), Pm(w["g1"]), R(dhn, D_MODEL), R(dh, D_MODEL)],
                           [OR(D_MODEL, D_MODEL), OA((1, D_MODEL))], name="rms_in_bwd")
            grads["g_mix_pre"][0] = dg1[0]
    grad_x = dh0[N_META:L_TOK][None]
    gfull = {n: jnp.stack(grads[n]) for n in grads}
    gfull["meta"] = dh0[:N_META]

    partial = [gst[n] if n in gst else gfull[n] for n, _, _ in _BIG]
    from_sib = _swap_sibling(partial, _KINDS, "scatter_sibling")
    chip_sums = [_pair_sum(_as3d(g), _as3d(r, 1), core, kd, "pair_sum_" + n)
                 for (n, kd, _), g, r in zip(_BIG, partial, from_sib)]
    parts = {n: p for (n, _, _), p in zip(_BIG, _swap_chips(chip_sums, "scatter_chips"))}

    def by_dest(a):
        return jnp.moveaxis(a.reshape(a.shape[:-1] + (N_DEV, a.shape[-1] // N_DEV)), -2, 0).astype(BF16)

    sum_in = _unext_w_in(_sum_parts(parts["w_in"], "sum_w_in"))
    sum_uq = _unext_w_uq(_sum_parts(parts["w_uq"], "sum_w_uq"))
    cols_in, cols_uq = _all_to_all([by_dest(sum_in), by_dest(sum_uq)], ("raw", "raw"), ("row", "row"),
                                   "columns_of_row_sums")
    parts["w_in"], parts["w_uq"] = _as3d(cols_in)[None], _as3d(cols_uq)[None]
    out = {}
    for n, _, _ in _BIG:
        res = _sum_adamw(parts[n], _as3d(given[n]), _as3d(given["m_" + n]), _as3d(given["v_" + n]), "adamw_" + n)
        out[n] = [r.reshape(given[n].shape) for r in res]

    def pack(d):
        flat = jnp.concatenate([d[n].reshape(-1) for n in _SMALL])
        return jnp.concatenate([flat, jnp.zeros((-flat.shape[0]) % (8 * LANE), F32)]).reshape(-1, LANE)

    (small_parts,) = _all_gather([pack(gfull)], ("raw",), "gather_small_grads")
    res = _sum_adamw(small_parts[:, None], pack(given)[None], pack({n: given["m_" + n] for n in _SMALL})[None],
                     pack({n: given["v_" + n] for n in _SMALL})[None], "adamw_small")
    off = 0
    for n in _SMALL:
        size = int(np.prod(given[n].shape))
        out[n] = [r.reshape(-1)[off:off + size].reshape(given[n].shape) for r in res]
        off += size

    return (loss, grad_x, *[out[n][0] for n in _ORDER], *[out[n][1] for n in _ORDER], *[out[n][2] for n in _ORDER],
            *[out[n][3] for n in _ORDER])
```

```python
import functools

import numpy as np
import jax
import jax.numpy as jnp
from jax import lax
from jax.experimental import pallas as pl
from jax.experimental.pallas import tpu as pltpu

D_MODEL = 2048
SEQ = 2048
DEPTH = 4
Q_RANK = 512
KV_RANK = 512
D_FF = 5632
N_META = 16
HEADS = 8
HD = 128
ROPE = 64
BW = HEADS * HD
EPS = 1e-6
NEG_INF = -1e30
ROPE_THETA = 10000.0
N_DEV = 8
LANE = 128
L_TOK = N_META + SEQ
LP = -(-L_TOK // LANE) * LANE
D_IN = Q_RANK + KV_RANK + ROPE + 6 * BW + HEADS + 3 * D_MODEL
MLA_SCALE = (HD + ROPE) ** -0.5
FOX_SCALE = HD ** -0.5
ADAM_LR, ADAM_B1, ADAM_B2, ADAM_EPS, ADAM_WD, ADAM_STEP = 0.001, 0.9, 0.999, 1e-08, 0.01, 10
VMEM_LIMIT = 48 * 1024 * 1024

F32 = jnp.float32
BF16 = jnp.bfloat16
MESH = pl.DeviceIdType.MESH

OFF_GATE = 0
OFF_CONV = 3 * D_MODEL
OFF_FOX = OFF_CONV + 3 * BW
OFF_CQ = OFF_FOX + 3 * BW
OFF_CKV = OFF_CQ + Q_RANK
OFF_KPE = OFF_CKV + KV_RANK
OFF_KROT = OFF_KPE + LANE
OFF_FL = OFF_KROT + LANE
W_ALL = OFF_FL + 2 * LANE
GW = 512 if D_MODEL % 512 == 0 else 256
TM_FF = 64
CW = 128
QG = 3 * LANE


def _pick(n, prefs):
    for p in prefs:
        if n % p == 0:
            return p
    return n


TM_ROW = 128
TQ = _pick(LP, (272, 128))


def _cb(off, w):
    assert off % w == 0, (off, w)
    return off // w


def _pcall(body, **kw):
    return pl.pallas_call(body, **kw)


def _params(sem):
    return pltpu.CompilerParams(dimension_semantics=sem, vmem_limit_bytes=VMEM_LIMIT)


def _bf(x):
    return x.astype(BF16)


def _dot(a, b, dims, **kw):
    return lax.dot_general(_bf(a), _bf(b), (dims, ((), ())), preferred_element_type=F32, **kw)


_NN = ((1,), (0,))
_NT = ((1,), (1,))
_TN = ((0,), (0,))
_ANY = pl.BlockSpec(memory_space=pl.ANY)


MM_VMEM_BUDGET = 38 * 1024 * 1024
HBM_BYTES_PER_STEP = 1 << 20


def _divisors(n, prefs):
    return [p for p in prefs if n % p == 0] or [n]


def _mm_tiles(m, n, kd, a_bytes, b_bytes, o_bytes):
    best = None
    for tm in _divisors(m, (2176, 2048, 1088, 1024, 544, 512, 272, 256, 128)):
        for tn in _divisors(n, (2048, 1536, 1024, 768, 512, 384, 256, 128)):
            for tk in _divisors(kd, (2816, 2304, 2176, 2048, 1536, 1408, 1024, 768, 544, 512, 384, 272, 256, 128)):
                nk = kd // tk
                vmem = 2 * (tm * tk * a_bytes + tk * tn * b_bytes + tm * tn * o_bytes) + 2 * tm * tn * 4
                vmem += (tm * tk * 2 if a_bytes == 4 else 0) + (tk * tn * 2 if b_bytes == 4 else 0)
                if vmem > MM_VMEM_BUDGET:
                    continue
                steps = (m // tm) * (n // tn) * nk
                cost = (m * kd * a_bytes * (1 if nk == 1 else n // tn) + kd * n * b_bytes * (m // tm)
                        + steps * HBM_BYTES_PER_STEP)
                if best is None or cost < best[0]:
                    best = (cost, tm, tn, tk)
    assert best is not None, (m, n, kd)
    return best[1:]


def _mm(a, b, *, ta=False, tb=False, bidx=(), stack=None, out_dtype=F32, deps=(), name):
    if ta:
        kd, m = a.shape
    else:
        m, kd = a.shape
    nlead = len(bidx)
    if tb:
        n, kd2 = b.shape[nlead:]
    else:
        kd2, n = b.shape[nlead:]
    assert kd == kd2, (a.shape, b.shape, ta, tb)
    if stack is not None:
        out_dtype = stack[0].dtype
    tm, tn, tk = _mm_tiles(m, n, kd, a.dtype.itemsize, b.dtype.itemsize, jnp.dtype(out_dtype).itemsize)
    nk = kd // tk
    dims = _TN if ta else (_NT if tb else _NN)

    def body(a_ref, b_ref, *rest):
        o_ref, acc_ref = rest[-2:]
        if nk == 1:
            o_ref[...] = _dot(a_ref[...], b_ref[...], dims).astype(o_ref.dtype)
            return
        k = pl.program_id(2)

        @pl.when(k == 0)
        def _():
            acc_ref[...] = jnp.zeros_like(acc_ref)

        acc_ref[...] += _dot(a_ref[...], b_ref[...], dims)

        @pl.when(k == nk - 1)
        def _():
            o_ref[...] = acc_ref[...].astype(o_ref.dtype)

    lead = (None,) * nlead
    a_spec = pl.BlockSpec((tk, tm), lambda i, j, k: (k, i)) if ta else pl.BlockSpec((tm, tk), lambda i, j, k: (i, k))
    if tb:
        b_spec = pl.BlockSpec(lead + (tn, tk), lambda i, j, k: bidx + (j, k))
    else:
        b_spec = pl.BlockSpec(lead + (tk, tn), lambda i, j, k: bidx + (k, j))
    in_specs, args, extra = [a_spec, b_spec], [a, b], {}
    if stack is None:
        out_spec = pl.BlockSpec((tm, tn), lambda i, j, k: (i, j))
        out_shape = jax.ShapeDtypeStruct((m, n), out_dtype)
    else:
        buf, sidx = stack
        assert buf.shape[len(sidx):] == (m, n), (buf.shape, sidx, m, n)
        in_specs.append(_ANY)
        args.append(buf)
        extra = dict(input_output_aliases={2: 0})
        out_spec = pl.BlockSpec((None,) * len(sidx) + (tm, tn), lambda i, j, k: sidx + (i, j))
        out_shape = jax.ShapeDtypeStruct(buf.shape, buf.dtype)
    in_specs += [_ANY] * len(deps)
    args += list(deps)
    return _pcall(
        body, name=name, grid=(m // tm, n // tn, nk), in_specs=in_specs, out_specs=out_spec, out_shape=out_shape,
        scratch_shapes=[pltpu.VMEM((tm, tn) if nk > 1 else (8, LANE), F32)],
        compiler_params=_params(("parallel", "parallel", "arbitrary")), **extra)(*args)


class R:
    def __init__(self, arr, w, cb0=0, cstep=0):
        self.arr, self.w, self.cb0, self.cstep = arr, w, cb0, cstep


class Pm:
    def __init__(self, arr):
        self.arr = arr


class OR:
    def __init__(self, total, w, cstep=0, dtype=F32):
        self.total, self.w, self.cstep, self.dtype = total, w, cstep, dtype


class OA:
    def __init__(self, shape):
        self.shape = shape


def _rw(fn, ins, outs, *, name, ncol=1, tm=None):
    tm = TM_ROW if tm is None else tm
    nrow = LP // tm
    n_in = len(ins)

    def body(*refs):
        j, i = pl.program_id(0), pl.program_id(1)
        res = fn(*[r[...] for r in refs[:n_in]])
        for o, ref, val in zip(outs, refs[n_in:], res):
            if isinstance(o, OR):
                ref[...] = val.astype(ref.dtype)
            else:
                @pl.when((i == 0) & (j == 0))
                def _(ref=ref):
                    ref[...] = jnp.zeros_like(ref)

                ref[...] += val

    in_specs = []
    for s in ins:
        if isinstance(s, R):
            in_specs.append(pl.BlockSpec((tm, s.w), lambda j, i, s=s: (i, s.cb0 + j * s.cstep)))
        else:
            in_specs.append(pl.BlockSpec(s.arr.shape, lambda j, i, nd=s.arr.ndim: (0,) * nd))
    out_specs, out_shape = [], []
    for o in outs:
        if isinstance(o, OR):
            out_specs.append(pl.BlockSpec((tm, o.w), lambda j, i, o=o: (i, j * o.cstep)))
            out_shape.append(jax.ShapeDtypeStruct((LP, o.total), o.dtype))
        else:
            out_specs.append(pl.BlockSpec(o.shape, lambda j, i: (0, 0)))
            out_shape.append(jax.ShapeDtypeStruct(o.shape, F32))
    return _pcall(body, name=name, grid=(ncol, nrow), in_specs=in_specs, out_specs=out_specs, out_shape=out_shape,
                  compiler_params=_params(("arbitrary", "arbitrary")))(*[s.arr for s in ins])


def _rms(x, g):
    return x * lax.rsqrt(jnp.mean(x * x, axis=-1, keepdims=True) + EPS) * g


def _resid_norm(h, z, ga, gb):
    h2 = h + _rms(z, ga)
    return h2, _rms(h2, gb)


def _resid_norm_bwd(h, z, ga, gb, dh2, dhn2):
    _, vjp = jax.vjp(_resid_norm, h, z, ga, gb)
    return vjp((dh2, dhn2))


def _rms_bwd(h, g, dhn, dh_in):
    _, vjp = jax.vjp(_rms, h, g)
    dh, dg = vjp(dhn)
    return dh + dh_in, dg


def _loss_fn(h2, f, g4, lt, rmask):
    h3 = h2 + _rms(f, g4)
    err = jnp.square(h3 - lt)
    return 0.5 * jnp.sum(jnp.mean(err, axis=-1, keepdims=True) * rmask)


def _loss_bwd(h2, f, g4, lt, rmask):
    val, (dh2, df, dg4) = jax.value_and_grad(_loss_fn, argnums=(0, 1, 2))(h2, f, g4, lt, rmask)
    return dh2, df, dg4, jnp.broadcast_to(val, (1, LANE))


def _mla_prep(cq, ckv, kpe, krot, gq, gkv, cosp, sinp):
    return _rms(cq, gq), _rms(ckv, gkv), kpe * cosp + krot * sinp


def _mla_prep_bwd(cq, ckv, kpe, krot, gq, gkv, cosp, sinp, dcqn, dckvn, dkper):
    _, vjp = jax.vjp(lambda a, b, c, d, e, f: _mla_prep(a, b, c, d, e, f, cosp, sinp), cq, ckv, kpe, krot, gq, gkv)
    return vjp((dcqn, dckvn, dkper))


def _merge(g0, g1, g2, y0, y1, y2):
    return (jax.nn.sigmoid(g0) * y0 + jax.nn.sigmoid(g1) * y1 + jax.nn.sigmoid(g2) * y2,)


def _merge_bwd(g0, g1, g2, y0, y1, y2, dm):
    _, vjp = jax.vjp(_merge, g0, g1, g2, y0, y1, y2)
    return vjp((dm,))


def _swiglu(gu):
    g, u = gu[:, :D_FF], gu[:, D_FF:]
    return (g * jax.nn.sigmoid(g) * u,)


def _swiglu_bwd(gu, dact):
    _, vjp = jax.vjp(_swiglu, gu)
    return vjp((dact,))


def _causal_probs(s, iq):
    qpos = iq * TQ + lax.broadcasted_iota(jnp.int32, s.shape, 0)
    kpos = lax.broadcasted_iota(jnp.int32, s.shape, 1)
    s = jnp.where(kpos <= qpos, s, NEG_INF)
    e = jnp.exp(s - jnp.max(s, axis=-1, keepdims=True))
    return e / jnp.sum(e, axis=-1, keepdims=True)


def _softmax_vjp(p, dp):
    return p * (dp - jnp.sum(p * dp, axis=-1, keepdims=True))


def _mla_specs():
    return [pl.BlockSpec((TQ, QG), lambda h, i: (i, h)),
            pl.BlockSpec((LP, 2 * HD), lambda h, i: (0, h)),
            pl.BlockSpec((LP, LANE), lambda h, i: (0, 0)),
            pl.BlockSpec((TQ, LANE), lambda h, i: (i, 0)),
            pl.BlockSpec((TQ, LANE), lambda h, i: (i, 0))]


def _mla_parts(q_ref, kv_ref, cos_ref, sin_ref):
    q, kv = q_ref[...], kv_ref[...]
    qn = q[:, :HD]
    qp = q[:, HD:2 * HD] * cos_ref[...] + q[:, 2 * HD:] * sin_ref[...]
    return qn, qp, kv[:, :HD], kv[:, HD:]


def _mla_fwd(q, kv, kper, cosp, sinp):
    def body(q_ref, kv_ref, kp_ref, cos_ref, sin_ref, o_ref):
        qn, qp, kn, v = _mla_parts(q_ref, kv_ref, cos_ref, sin_ref)
        s = (_dot(qn, kn, _NT) + _dot(qp, kp_ref[...], _NT)) * MLA_SCALE
        o_ref[...] = _dot(_causal_probs(s, pl.program_id(1)), v, _NN).astype(o_ref.dtype)

    return _pcall(body, name="mla_fwd", grid=(HEADS, LP // TQ), in_specs=_mla_specs(),
                  out_specs=pl.BlockSpec((TQ, HD), lambda h, i: (i, h)),
                  out_shape=jax.ShapeDtypeStruct((LP, BW), BF16),
                  compiler_params=_params(("parallel", "parallel")))(q, kv, kper, cosp, sinp)


def _mla_bwd(q, kv, kper, cosp, sinp, do):
    def body(q_ref, kv_ref, kp_ref, cos_ref, sin_ref, do_ref, dq_ref, dkv_ref, dkp_ref):
        h, iq = pl.program_id(0), pl.program_id(1)
        qn, qp, kn, v = _mla_parts(q_ref, kv_ref, cos_ref, sin_ref)
        kp, dout = kp_ref[...], do_ref[...]
        p = _causal_probs((_dot(qn, kn, _NT) + _dot(qp, kp, _NT)) * MLA_SCALE, iq)
        ds = _softmax_vjp(p, _dot(dout, v, _NT)) * MLA_SCALE
        dqp = _dot(ds, kp, _NN)
        dq = jnp.concatenate([_dot(ds, kn, _NN), dqp * cos_ref[...], dqp * sin_ref[...]], axis=1)
        dq_ref[...] = dq.astype(dq_ref.dtype)

        @pl.when(iq == 0)
        def _():
            dkv_ref[...] = jnp.zeros_like(dkv_ref)

        dkv_ref[...] += jnp.concatenate([_dot(ds, qn, _TN), _dot(p, dout, _TN)], axis=1)

        @pl.when((iq == 0) & (h == 0))
        def _():
            dkp_ref[...] = jnp.zeros_like(dkp_ref)

        dkp_ref[...] += _dot(ds, qp, _TN)

    return _pcall(
        body, name="mla_bwd", grid=(HEADS, LP // TQ),
        in_specs=_mla_specs() + [pl.BlockSpec((TQ, HD), lambda h, i: (i, h))],
        out_specs=[pl.BlockSpec((TQ, QG), lambda h, i: (i, h)), pl.BlockSpec((LP, 2 * HD), lambda h, i: (0, h)),
                   pl.BlockSpec((LP, LANE), lambda h, i: (0, 0))],
        out_shape=[jax.ShapeDtypeStruct((LP, HEADS * QG), BF16), jax.ShapeDtypeStruct((LP, 2 * BW), F32),
                   jax.ShapeDtypeStruct((LP, LANE), F32)],
        compiler_params=_params(("arbitrary", "arbitrary")))(q, kv, kper, cosp, sinp, do)


def _fox_specs():
    cq, ck, cv = _cb(OFF_FOX, HD), _cb(OFF_FOX + BW, HD), _cb(OFF_FOX + 2 * BW, HD)
    return [pl.BlockSpec((TQ, HD), lambda h, i: (i, cq + h)),
            pl.BlockSpec((LP, HD), lambda h, i: (0, ck + h)),
            pl.BlockSpec((LP, HD), lambda h, i: (0, cv + h)),
            pl.BlockSpec((1, TQ, 1), lambda h, i: (h, i, 0)),
            pl.BlockSpec((1, 1, LP), lambda h, i: (h, 0, 0))]


def _fox_probs(q_ref, k_ref, cq_ref, ck_ref, iq):
    s = _dot(q_ref[...], k_ref[...], _NT) * FOX_SCALE + (cq_ref[0] - ck_ref[0])
    return _causal_probs(s, iq)


def _fox_fwd(p_all, cq3, ck3):
    def body(q_ref, k_ref, v_ref, cq_ref, ck_ref, o_ref):
        p = _fox_probs(q_ref, k_ref, cq_ref, ck_ref, pl.program_id(1))
        o_ref[...] = _dot(p, v_ref[...], _NN).astype(o_ref.dtype)

    return _pcall(body, name="fox_fwd", grid=(HEADS, LP // TQ), in_specs=_fox_specs(),
                  out_specs=pl.BlockSpec((TQ, HD), lambda h, i: (i, h)),
                  out_shape=jax.ShapeDtypeStruct((LP, BW), BF16),
                  compiler_params=_params(("parallel", "parallel")))(p_all, p_all, p_all, cq3, ck3)


def _fox_bwd(p_all, cq3, ck3, do):
    def body(q_ref, k_ref, v_ref, cq_ref, ck_ref, do_ref, dq_ref, dk_ref, dv_ref, dcq_ref, dck_ref):
        iq = pl.program_id(1)
        p = _fox_probs(q_ref, k_ref, cq_ref, ck_ref, iq)
        dout = do_ref[...]
        ds = _softmax_vjp(p, _dot(dout, v_ref[...], _NT))
        dss = ds * FOX_SCALE
        dq_ref[...] = _dot(dss, k_ref[...], _NN).astype(dq_ref.dtype)
        dcq_ref[0] = jnp.sum(ds, axis=1, keepdims=True)

        @pl.when(iq == 0)
        def _():
            dk_ref[...] = jnp.zeros_like(dk_ref)
            dv_ref[...] = jnp.zeros_like(dv_ref)
            dck_ref[...] = jnp.zeros_like(dck_ref)

        dk_ref[...] += _dot(dss, q_ref[...], _TN)
        dv_ref[...] += _dot(p, dout, _TN)
        dck_ref[0] -= jnp.sum(ds, axis=0, keepdims=True)

    head_rows = pl.BlockSpec((TQ, HD), lambda h, i: (i, h))
    head_all = pl.BlockSpec((LP, HD), lambda h, i: (0, h))
    return _pcall(
        body, name="fox_bwd", grid=(HEADS, LP // TQ), in_specs=_fox_specs() + [head_rows],
        out_specs=[head_rows, head_all, head_all, pl.BlockSpec((1, TQ, 1), lambda h, i: (h, i, 0)),
                   pl.BlockSpec((1, 1, LP), lambda h, i: (h, 0, 0))],
        out_shape=[jax.ShapeDtypeStruct((LP, BW), BF16)] + [jax.ShapeDtypeStruct((LP, BW), F32)] * 2
        + [jax.ShapeDtypeStruct((HEADS, LP, 1), F32), jax.ShapeDtypeStruct((HEADS, 1, LP), F32)],
        compiler_params=_params(("arbitrary", "arbitrary")))(p_all, p_all, p_all, cq3, ck3, do)


def _log_sigmoid(x):
    return jnp.minimum(x, 0.0) - jnp.log(1.0 + jnp.exp(-jnp.abs(x)))


def _decay_fwd(p_all, b_pad):
    tc = TM_ROW

    def body(fl_ref, b_ref, c_ref):
        lf = _log_sigmoid(fl_ref[...] + b_ref[...])
        r = pl.program_id(0) * tc + lax.broadcasted_iota(jnp.int32, (tc, LP), 0)
        s = lax.broadcasted_iota(jnp.int32, (tc, LP), 1)
        c_ref[...] = jnp.dot((s <= r).astype(F32), lf, precision=lax.Precision.HIGHEST, preferred_element_type=F32)

    return _pcall(body, name="decay_fwd", grid=(LP // tc,),
                  in_specs=[pl.BlockSpec((LP, LANE), lambda i: (0, _cb(OFF_FL, LANE))),
                            pl.BlockSpec((1, LANE), lambda i: (0, 0))],
                  out_specs=pl.BlockSpec((tc, LANE), lambda i: (i, 0)),
                  out_shape=jax.ShapeDtypeStruct((LP, LANE), F32), compiler_params=_params(("parallel",)))(p_all, b_pad)


def _decay_bwd(p_all, b_pad, dc):
    tc = TM_ROW

    def body(fl_ref, b_ref, dc_ref, dfl_ref, db_ref):
        i = pl.program_id(0)
        r = i * tc + lax.broadcasted_iota(jnp.int32, (tc, LP), 0)
        t = lax.broadcasted_iota(jnp.int32, (tc, LP), 1)
        dlf = jnp.dot((t >= r).astype(F32), dc_ref[...], precision=lax.Precision.HIGHEST, preferred_element_type=F32)
        dfl = dlf * jax.nn.sigmoid(-(fl_ref[...] + b_ref[...]))
        dfl_ref[...] = dfl.astype(dfl_ref.dtype)

        @pl.when(i == 0)
        def _():
            db_ref[...] = jnp.zeros_like(db_ref)

        db_ref[...] += jnp.sum(dfl, axis=0, keepdims=True)

    return _pcall(body, name="decay_bwd", grid=(LP // tc,),
                  in_specs=[pl.BlockSpec((tc, LANE), lambda i: (i, _cb(OFF_FL, LANE))),
                            pl.BlockSpec((1, LANE), lambda i: (0, 0)), pl.BlockSpec((LP, LANE), lambda i: (0, 0))],
                  out_specs=[pl.BlockSpec((tc, LANE), lambda i: (i, 0)), pl.BlockSpec((1, LANE), lambda i: (0, 0))],
                  out_shape=[jax.ShapeDtypeStruct((LP, LANE), BF16), jax.ShapeDtypeStruct((1, LANE), F32)],
                  compiler_params=_params(("arbitrary",)))(p_all, b_pad, dc)


def _conv_specs(layer):
    c0 = _cb(OFF_CONV, CW)
    step = BW // CW
    return [pl.BlockSpec((LP, CW), lambda j: (0, c0 + j)), pl.BlockSpec((LP, CW), lambda j: (0, c0 + step + j)),
            pl.BlockSpec((LP, CW), lambda j: (0, c0 + 2 * step + j)),
            pl.BlockSpec((None, 3, CW), lambda j: (layer, 0, j))]


def _shift_down(x, k, t):
    return jnp.where(t >= k, pltpu.roll(x, k, 0), 0.0)


def _shift_up(x, k, t):
    return jnp.where(t < LP - k, pltpu.roll(x, LP - k, 0), 0.0)


def _conv_fwd(p_all, cw, layer):
    def body(b_ref, c_ref, x_ref, w_ref, o_ref):
        t = lax.broadcasted_iota(jnp.int32, (LP, CW), 0)
        uu = c_ref[...] * x_ref[...]
        w = w_ref[...]
        u = w[2:3] * uu + w[1:2] * _shift_down(uu, 1, t) + w[0:1] * _shift_down(uu, 2, t)
        o_ref[...] = (b_ref[...] * u).astype(o_ref.dtype)

    return _pcall(body, name="conv_fwd", grid=(BW // CW,), in_specs=_conv_specs(layer),
                  out_specs=pl.BlockSpec((LP, CW), lambda j: (0, j)), out_shape=jax.ShapeDtypeStruct((LP, BW), BF16),
                  compiler_params=_params(("parallel",)))(p_all, p_all, p_all, cw)


def _conv_bwd(p_all, cw, layer, do):
    def body(b_ref, c_ref, x_ref, w_ref, do_ref, d_ref, dw_ref):
        t = lax.broadcasted_iota(jnp.int32, (LP, CW), 0)
        cc, xx, w, dout = c_ref[...], x_ref[...], w_ref[...], do_ref[...]
        uu = cc * xx
        s1, s2 = _shift_down(uu, 1, t), _shift_down(uu, 2, t)
        u = w[2:3] * uu + w[1:2] * s1 + w[0:1] * s2
        du = dout * b_ref[...]
        duu = w[2:3] * du + w[1:2] * _shift_up(du, 1, t) + w[0:1] * _shift_up(du, 2, t)
        d_ref[0] = (dout * u).astype(d_ref.dtype)
        d_ref[1] = (duu * xx).astype(d_ref.dtype)
        d_ref[2] = (duu * cc).astype(d_ref.dtype)
        dw_ref[0:1, :] = jnp.sum(du * s2, axis=0, keepdims=True)
        dw_ref[1:2, :] = jnp.sum(du * s1, axis=0, keepdims=True)
        dw_ref[2:3, :] = jnp.sum(du * uu, axis=0, keepdims=True)

    return _pcall(body, name="conv_bwd", grid=(BW // CW,),
                  in_specs=_conv_specs(layer) + [pl.BlockSpec((LP, CW), lambda j: (0, j))],
                  out_specs=[pl.BlockSpec((3, LP, CW), lambda j: (0, 0, j)), pl.BlockSpec((3, CW), lambda j: (0, j))],
                  out_shape=[jax.ShapeDtypeStruct((3, LP, BW), BF16), jax.ShapeDtypeStruct((3, BW), F32)],
                  compiler_params=_params(("parallel",)))(p_all, p_all, p_all, cw, do)


def _place():
    return lax.axis_index("x"), lax.axis_index("y"), lax.axis_index("c")


def _dest_slice(ref, kind, d):
    if kind == "raw":
        return ref.at[d]
    nd = len(ref.shape)
    if kind == "col":
        w = ref.shape[-1] // N_DEV
        return ref.at[(slice(None),) * (nd - 1) + (pl.ds(d * w, w),)]
    r = ref.shape[-2] // N_DEV
    return ref.at[(slice(None),) * (nd - 2) + (pl.ds(d * r, r), slice(None))]


def _unsharded_shape(shape, kind, lead=N_DEV):
    if kind == "raw":
        return (lead,) + shape
    if kind == "col":
        return shape[:-1] + (N_DEV * shape[-1],)
    return shape[:-2] + (N_DEV * shape[-2], shape[-1])


def _all_gather(arrs, kinds, name):
    n = len(arrs)

    def body(*refs):
        ins, outs = refs[:n], refs[n:2 * n]
        send, recv, loc = refs[2 * n:]
        x, y, c = _place()
        sib = (x, y, 1 - c)
        chips = [(1 - x, y), (x, 1 - y), (1 - x, 1 - y)]

        def idx(px, py, pc):
            return 4 * px + 2 * py + pc

        def copy(j, k, block, to, src=None):
            dst = _dest_slice(outs[k], kinds[k], idx(*block))
            return pltpu.make_async_remote_copy(src_ref=dst if src is None else src, dst_ref=dst, send_sem=send.at[j, k],
                                                recv_sem=recv.at[j, k], device_id=to, device_id_type=MESH)

        me = (x, y, c)
        mine = [pltpu.make_async_copy(ins[k], _dest_slice(outs[k], kinds[k], idx(*me)), loc.at[k]) for k in range(n)]
        for cp in mine:
            cp.start()
        first = [copy(0, k, me, sib, src=ins[k]) for k in range(n)]
        first += [copy(1 + j, k, me, (*chip, c), src=ins[k]) for j, chip in enumerate(chips) for k in range(n)]
        for cp in first:
            cp.start()
        passed = []
        for j, chip in enumerate(chips):
            for k in range(n):
                copy(1 + j, k, (*chip, c), me, src=ins[k]).wait_recv()
                cp = copy(4 + j, k, (*chip, c), sib)
                cp.start()
                passed.append(cp)
        for k in range(n):
            copy(0, k, sib, me, src=ins[k]).wait_recv()
        for j, chip in enumerate(chips):
            for k in range(n):
                copy(4 + j, k, (*chip, 1 - c), me).wait_recv()
        for cp in first + passed:
            cp.wait_send()
        for cp in mine:
            cp.wait()

    return _pcall(body, name=name, in_specs=[_ANY] * n, out_specs=[_ANY] * n,
                  out_shape=[jax.ShapeDtypeStruct(_unsharded_shape(a.shape, kd), a.dtype) for a, kd in zip(arrs, kinds)],
                  scratch_shapes=[pltpu.SemaphoreType.DMA((7, n)), pltpu.SemaphoreType.DMA((7, n)),
                                  pltpu.SemaphoreType.DMA((n,))])(*arrs)


def _shard_shape(a, kind):
    if kind == "raw":
        return a.shape[2:]
    if kind == "col":
        return a.shape[:-1] + (a.shape[-1] // N_DEV,)
    return a.shape[:-2] + (a.shape[-2] // N_DEV, a.shape[-1])


def _swap_sibling(arrs, kinds, name):
    n = len(arrs)
    npieces = sum(1 if kd == "raw" else 4 for kd in kinds)

    def body(*refs):
        ins, outs = refs[:n], refs[n:2 * n]
        send, recv = refs[2 * n:]
        x, y, c = _place()
        pieces = []
        for k in range(n):
            if kinds[k] == "raw":
                pieces.append((ins[k].at[1 - c], outs[k]))
            else:
                pieces += [(_dest_slice(ins[k], kinds[k], 2 * p + 1 - c), outs[k].at[p]) for p in range(4)]
        cps = [pltpu.make_async_remote_copy(src_ref=src, dst_ref=dst, send_sem=send.at[i], recv_sem=recv.at[i],
                                            device_id=(x, y, 1 - c), device_id_type=MESH)
               for i, (src, dst) in enumerate(pieces)]
        for cp in cps:
            cp.start()
        for cp in cps:
            cp.wait()

    return _pcall(body, name=name, in_specs=[_ANY] * n, out_specs=[_ANY] * n,
                  out_shape=[jax.ShapeDtypeStruct((4,) + _shard_shape(a, kd), a.dtype) for a, kd in zip(arrs, kinds)],
                  scratch_shapes=[pltpu.SemaphoreType.DMA((npieces,)), pltpu.SemaphoreType.DMA((npieces,))])(*arrs)


def _swap_chips(arrs, name):
    n = len(arrs)

    def body(*refs):
        ins, outs = refs[:n], refs[n:2 * n]
        send, recv, loc = refs[2 * n:]
        x, y, c = _place()
        mychip = 2 * x + y
        mine = [pltpu.make_async_copy(ins[k].at[mychip], outs[k].at[mychip], loc.at[k]) for k in range(n)]
        for cp in mine:
            cp.start()
        cps = []
        for j, (cx, cy) in enumerate([(1 - x, y), (x, 1 - y), (1 - x, 1 - y)]):
            for k in range(n):
                cps.append(pltpu.make_async_remote_copy(
                    src_ref=ins[k].at[2 * cx + cy], dst_ref=outs[k].at[mychip], send_sem=send.at[j, k],
                    recv_sem=recv.at[j, k], device_id=(cx, cy, c), device_id_type=MESH))
        for cp in cps:
            cp.start()
        for cp in cps:
            cp.wait()
        for cp in mine:
            cp.wait()

    return _pcall(body, name=name, in_specs=[_ANY] * n, out_specs=[_ANY] * n,
                  out_shape=[jax.ShapeDtypeStruct(a.shape, a.dtype) for a in arrs],
                  scratch_shapes=[pltpu.SemaphoreType.DMA((3, n)), pltpu.SemaphoreType.DMA((3, n)),
                                  pltpu.SemaphoreType.DMA((n,))])(*arrs)


_HBM = pl.BlockSpec(memory_space=pltpu.HBM)
_SEM = pl.BlockSpec(memory_space=pltpu.SEMAPHORE)
_SIDE_EFFECT = pltpu.SideEffectType.DATAFLOW_SIDE_EFFECTING


def _split_copies(plan, refs, ns, nd):
    send, recv = refs[ns + nd], refs[ns + nd + 1]
    return [pltpu.make_async_remote_copy(src_ref=src, dst_ref=dst, send_sem=send.at[i], recv_sem=recv.at[i],
                                         device_id=dev, device_id_type=MESH)
            for i, (src, dst, dev) in enumerate(plan(refs[:ns], refs[ns:ns + nd]))]


def _split_start(plan, ncopies, srcs, dsts, name):
    ns, nd = len(srcs), len(dsts)

    def body(*refs):
        copies = _split_copies(plan, refs, ns, nd)
        assert len(copies) == ncopies
        for cp in copies:
            cp.start()
        refs[-1][...] = jnp.zeros_like(refs[-1])

    bufs = [pltpu.with_memory_space_constraint(a, pltpu.HBM) for a in list(srcs) + list(dsts)]
    res = _pcall(
        body, name=name, in_specs=[_HBM] * (ns + nd),
        out_specs=[_SEM, _SEM] + [_HBM] * (ns + nd) + [pl.BlockSpec(memory_space=pltpu.VMEM)],
        out_shape=[pltpu.SemaphoreType.DMA((ncopies,)), pltpu.SemaphoreType.DMA((ncopies,))]
        + [pltpu.HBM(a.shape, a.dtype) for a in bufs] + [jax.ShapeDtypeStruct((8, LANE), F32)],
        input_output_aliases={i: 2 + i for i in range(ns + nd)},
        compiler_params=pltpu.CompilerParams(has_side_effects=_SIDE_EFFECT))(*bufs)
    return res[0], res[1], list(res[2:2 + ns]), list(res[2 + ns:2 + ns + nd]), res[-1]


def _split_wait(plan, send, recv, srcs, dsts, after, name):
    ns, nd = len(srcs), len(dsts)

    def body(*refs):
        for cp in _split_copies(plan, refs, ns, nd):
            cp.wait_send()
            cp.wait_recv()

    res = _pcall(
        body, name=name, in_specs=[_HBM] * (ns + nd) + [_SEM, _SEM, _ANY], out_specs=[_HBM] * (ns + nd),
        out_shape=[pltpu.HBM(a.shape, a.dtype) for a in list(srcs) + list(dsts)],
        input_output_aliases={i: i for i in range(ns + nd)},
        compiler_params=pltpu.CompilerParams(has_side_effects=_SIDE_EFFECT))(*srcs, *dsts, send, recv, after)
    return list(res[:ns]), list(res[ns:])


def _sibling_plan(kinds):
    def plan(srcs, dsts):
        x, y, c = _place()
        return [(_dest_slice(srcs[k], kinds[k], 2 * p + 1 - c), dsts[k].at[p], (x, y, 1 - c))
                for k in range(len(kinds)) for p in range(4)]
    return plan


def _chips_plan(n):
    def plan(srcs, dsts):
        x, y, c = _place()
        return [(srcs[k].at[2 * cx + cy], dsts[k].at[j], (cx, cy, c))
                for j, (cx, cy) in enumerate([(1 - x, y), (x, 1 - y), (1 - x, 1 - y)]) for k in range(n)]
    return plan


def _slice_shape(shape, kind):
    if kind == "raw":
        return shape[1:]
    if kind == "col":
        return shape[:-1] + (shape[-1] // N_DEV,)
    return shape[:-2] + (shape[-2] // N_DEV, shape[-1])


def _all_to_all(arrs, src_kinds, dst_kinds, name):
    n = len(arrs)

    def body(*refs):
        ins, outs = refs[:n], refs[n:2 * n]
        send, recv, loc = refs[2 * n:]
        x, y, c = _place()
        me = 4 * x + 2 * y + c
        mine = [pltpu.make_async_copy(_dest_slice(ins[k], src_kinds[k], me), _dest_slice(outs[k], dst_kinds[k], me),
                                      loc.at[k]) for k in range(n)]
        for cp in mine:
            cp.start()
        cps = []
        for r in range(1, N_DEV):
            px, py, pc = (1 - x if r & 4 else x), (1 - y if r & 2 else y), (1 - c if r & 1 else c)
            for k in range(n):
                cps.append(pltpu.make_async_remote_copy(
                    src_ref=_dest_slice(ins[k], src_kinds[k], 4 * px + 2 * py + pc),
                    dst_ref=_dest_slice(outs[k], dst_kinds[k], me), send_sem=send.at[r - 1, k],
                    recv_sem=recv.at[r - 1, k], device_id=(px, py, pc), device_id_type=MESH))
        for cp in cps:
            cp.start()
        for cp in cps:
            cp.wait()
        for cp in mine:
            cp.wait()

    out_shape = [jax.ShapeDtypeStruct(_unsharded_shape(_slice_shape(a.shape, sk), dk), a.dtype)
                 for a, sk, dk in zip(arrs, src_kinds, dst_kinds)]
    return _pcall(body, name=name, in_specs=[_ANY] * n, out_specs=[_ANY] * n, out_shape=out_shape,
                  scratch_shapes=[pltpu.SemaphoreType.DMA((N_DEV - 1, n)), pltpu.SemaphoreType.DMA((N_DEV - 1, n)),
                                  pltpu.SemaphoreType.DMA((n,))])(*arrs)


def _rows_tile(rows, cols=0):
    cap = 512 * 1024
    return _pick(rows, [t for t in (128, 64, 32, 16) if t * cols <= cap])


def _pair_sum(g, r1, core, kind, name):
    _, groups, rows, cols = r1.shape
    tr = _rows_tile(rows, cols)
    nr = rows // tr
    if kind == "raw":
        g_spec = pl.BlockSpec((None, None, None, tr, cols), lambda p, q, i, c: (c[0], p, q, i, 0))
    elif kind == "col":
        g_spec = pl.BlockSpec((None, tr, cols), lambda p, q, i, c: (q, i, 2 * p + c[0]))
    else:
        g_spec = pl.BlockSpec((None, tr, cols), lambda p, q, i, c: (q, (2 * p + c[0]) * nr + i, 0))
    r_spec = pl.BlockSpec((None, None, tr, cols), lambda p, q, i, c: (p, q, i, 0))

    def body(c_ref, g_ref, r_ref, o_ref):
        o_ref[...] = (g_ref[...].astype(F32) + r_ref[...].astype(F32)).astype(o_ref.dtype)

    return _pcall(
        body, name=name,
        grid_spec=pltpu.PrefetchScalarGridSpec(num_scalar_prefetch=1, grid=(4, groups, nr), in_specs=[g_spec, r_spec],
                                               out_specs=r_spec),
        out_shape=jax.ShapeDtypeStruct(r1.shape, r1.dtype),
        compiler_params=_params(("parallel", "parallel", "parallel")))(core, g, r1)


def _sum_parts(parts, name):
    npart, groups, rows, cols = parts.shape
    tr = _rows_tile(rows, cols)

    def body(p_ref, o_ref):
        g = p_ref[0].astype(F32)
        for k in range(1, npart):
            g = g + p_ref[k].astype(F32)
        o_ref[...] = g

    return _pcall(body, name=name, grid=(groups, rows // tr),
                  in_specs=[pl.BlockSpec((npart, None, tr, cols), lambda q, i: (0, q, i, 0))],
                  out_specs=pl.BlockSpec((None, tr, cols), lambda q, i: (q, i, 0)),
                  out_shape=jax.ShapeDtypeStruct((groups, rows, cols), F32),
                  compiler_params=_params(("parallel", "parallel")))(parts)


def _adamw(g, w, m, v):
    m = ADAM_B1 * m + (1.0 - ADAM_B1) * g
    v = ADAM_B2 * v + (1.0 - ADAM_B2) * jnp.square(g)
    m_hat = m / (1.0 - ADAM_B1 ** ADAM_STEP)
    v_hat = v / (1.0 - ADAM_B2 ** ADAM_STEP)
    return -ADAM_LR * (m_hat / (jnp.sqrt(v_hat) + ADAM_EPS) + ADAM_WD * w), m, v


def _sum_adamw(parts, w, m, v, name):
    npart, groups, rows, cols = parts.shape
    tr = _rows_tile(rows, cols)

    def body(p_ref, w_ref, m_ref, v_ref, g_ref, d_ref, nm_ref, nv_ref):
        g = p_ref[0].astype(F32)
        for k in range(1, npart):
            g = g + p_ref[k].astype(F32)
        g_ref[...] = g
        d_ref[...], nm_ref[...], nv_ref[...] = _adamw(g, w_ref[...], m_ref[...], v_ref[...])

    blk = pl.BlockSpec((None, tr, cols), lambda q, i: (q, i, 0))
    return _pcall(body, name=name, grid=(groups, rows // tr),
                  in_specs=[pl.BlockSpec((npart, None, tr, cols), lambda q, i: (0, q, i, 0)), blk, blk, blk],
                  out_specs=[blk] * 4, out_shape=[jax.ShapeDtypeStruct((groups, rows, cols), F32)] * 4,
                  compiler_params=_params(("parallel", "parallel")))(parts, w, m, v)


def _layer_sum_specs(own, layer):
    _, groups, rows, cols = own.shape
    tr = _rows_tile(rows, cols)
    own_spec = pl.BlockSpec((None, None, tr, cols), lambda q, i, ch: (ch[0], q, i, 0))
    theirs_spec = pl.BlockSpec((3, None, tr, cols), lambda q, i, ch: (0, q, i, 0))
    stacked = pl.BlockSpec((None, tr, cols), lambda q, i, ch: (layer * groups + q, i, 0))
    return (groups, rows // tr), own_spec, theirs_spec, stacked


def _layer_sum(own_ref, theirs_ref):
    g = own_ref[...].astype(F32)
    for k in range(3):
        g = g + theirs_ref[k].astype(F32)
    return g


def _sum_parts_layer(own, theirs, chip, buf, layer, name):
    grid, own_spec, theirs_spec, stacked = _layer_sum_specs(own, layer)

    def body(ch_ref, own_ref, theirs_ref, buf_ref, o_ref):
        o_ref[...] = _layer_sum(own_ref, theirs_ref)

    return _pcall(
        body, name=name,
        grid_spec=pltpu.PrefetchScalarGridSpec(num_scalar_prefetch=1, grid=grid, in_specs=[own_spec, theirs_spec, _ANY],
                                               out_specs=stacked),
        out_shape=jax.ShapeDtypeStruct(buf.shape, buf.dtype), input_output_aliases={3: 0},
        compiler_params=_params(("parallel", "parallel")))(chip, own, theirs, buf)


def _sum_adamw_layer(own, theirs, chip, w, m, v, outs, layer, deps, name):
    grid, own_spec, theirs_spec, stacked = _layer_sum_specs(own, layer)

    def body(ch_ref, own_ref, theirs_ref, w_ref, m_ref, v_ref, *rest):
        g_ref, d_ref, nm_ref, nv_ref = rest[-4:]
        g = _layer_sum(own_ref, theirs_ref)
        g_ref[...] = g
        d_ref[...], nm_ref[...], nv_ref[...] = _adamw(g, w_ref[...], m_ref[...], v_ref[...])

    return _pcall(
        body, name=name,
        grid_spec=pltpu.PrefetchScalarGridSpec(
            num_scalar_prefetch=1, grid=grid,
            in_specs=[own_spec, theirs_spec, stacked, stacked, stacked] + [_ANY] * (4 + len(deps)), out_specs=[stacked] * 4),
        out_shape=[jax.ShapeDtypeStruct(o.shape, o.dtype) for o in outs],
        input_output_aliases={6 + i: i for i in range(4)},
        compiler_params=_params(("parallel", "parallel")))(chip, own, theirs, w, m, v, *outs, *deps)


def _rot_cols(w):
    return jnp.concatenate([-w[..., ROPE // 2:], w[..., :ROPE // 2]], axis=-1)


def _rot_cols_t(dw):
    return jnp.concatenate([dw[..., ROPE // 2:], -dw[..., :ROPE // 2]], axis=-1)


_IN_SPLITS = np.cumsum([0, Q_RANK, KV_RANK, ROPE, BW, BW, BW, BW, BW, BW, HEADS, 3 * D_MODEL])


def _ext_w_in(w):
    o = _IN_SPLITS
    kpe = w[..., o[2]:o[3]]
    z = lambda n: jnp.zeros(w.shape[:-1] + (n,), w.dtype)
    return jnp.concatenate([w[..., o[10]:o[11]], w[..., o[3]:o[9]], w[..., o[0]:o[2]], kpe, z(LANE - ROPE),
                            _rot_cols(kpe), z(LANE - ROPE), w[..., o[9]:o[10]], z(2 * LANE - HEADS)], axis=-1)


def _unext_w_in(dw):
    kpe = dw[..., OFF_KPE:OFF_KPE + ROPE] + _rot_cols_t(dw[..., OFF_KROT:OFF_KROT + ROPE])
    return jnp.concatenate([dw[..., OFF_CQ:OFF_KPE], kpe, dw[..., OFF_CONV:OFF_CQ], dw[..., OFF_FL:OFF_FL + HEADS],
                            dw[..., OFF_GATE:OFF_CONV]], axis=-1)


def _ext_w_uq(w):
    w3 = w.reshape(w.shape[:-1] + (HEADS, HD + ROPE))
    pe = w3[..., HD:]
    z = jnp.zeros(w3.shape[:-1] + (LANE - ROPE,), w.dtype)
    return jnp.concatenate([w3[..., :HD], pe, z, _rot_cols(pe), z], axis=-1).reshape(w.shape[:-1] + (HEADS * QG,))


def _unext_w_uq(dw):
    d3 = dw.reshape(dw.shape[:-1] + (HEADS, QG))
    pe = d3[..., HD:HD + ROPE] + _rot_cols_t(d3[..., 2 * HD:2 * HD + ROPE])
    return jnp.concatenate([d3[..., :HD], pe], axis=-1).reshape(dw.shape[:-1] + (HEADS * (HD + ROPE),))


_BIG = (("meta", "col", F32), ("w_in", "row", BF16), ("w_uq", "row", BF16), ("w_ukv", "col", BF16),
        ("conv_w", "col", F32), ("w_branch", "col", BF16), ("w_out", "row", BF16), ("w_ffn_in", "col", BF16),
        ("w_ffn_out", "row", BF16))
_KINDS = tuple(kd for _, kd, _ in _BIG)
_SMALL = ("b_forget", "g_q_lat", "g_kv_lat", "g_mix_pre", "g_mix_post", "g_ffn_pre", "g_ffn_post")
_ORDER = ("meta", "w_in", "b_forget", "g_q_lat", "g_kv_lat", "w_uq", "w_ukv", "conv_w", "w_branch", "w_out",
          "w_ffn_in", "w_ffn_out", "g_mix_pre", "g_mix_post", "g_ffn_pre", "g_ffn_post")


def _unshard_cols(g):
    g = jnp.moveaxis(g, 0, -2)
    return g.reshape(g.shape[:-2] + (g.shape[-2] * g.shape[-1],))


def _as3d(a, lead=0):
    return a.reshape(a.shape[:lead] + (-1,) + a.shape[-2:])


def kernel(x, meta, w_in, b_forget, g_q_lat, g_kv_lat, w_uq, w_ukv, conv_w, w_branch, w_out, w_ffn_in, w_ffn_out, g_mix_pre, g_mix_post, g_ffn_pre, g_ffn_post, loss_target, m_meta, m_w_in, m_b_forget, m_g_q_lat, m_g_kv_lat, m_w_uq, m_w_ukv, m_conv_w, m_w_branch, m_w_out, m_w_ffn_in, m_w_ffn_out, m_g_mix_pre, m_g_mix_post, m_g_ffn_pre, m_g_ffn_post, v_meta, v_w_in, v_b_forget, v_g_q_lat, v_g_kv_lat, v_w_uq, v_w_ukv, v_conv_w, v_w_branch, v_w_out, v_w_ffn_in, v_w_ffn_out, v_g_mix_pre, v_g_mix_post, v_g_ffn_pre, v_g_ffn_post):
    given = dict(locals())
    core = lax.axis_index("c").astype(jnp.int32).reshape(1)

    rows_in, rows_uq = _all_to_all([w_in.astype(BF16), w_uq.astype(BF16)], ("row", "row"), ("raw", "raw"),
                                   "rows_of_column_shards")
    shards = {n: given[n].astype(dt) for n, _, dt in _BIG}
    shards["w_in"] = _ext_w_in(_unshard_cols(rows_in))
    shards["w_uq"] = _ext_w_uq(_unshard_cols(rows_uq))
    gathered = _all_gather([shards[n] for n, _, _ in _BIG], _KINDS, "gather_weights")
    full = {n: g for (n, _, _), g in zip(_BIG, gathered)}

    pos = jnp.arange(LP, dtype=F32)[:, None]
    inv_freq = 1.0 / (ROPE_THETA ** (jnp.arange(0, ROPE, 2, dtype=F32) / ROPE))
    ang = pos * inv_freq[None, :]
    zpad = jnp.zeros((LP, LANE - ROPE), F32)
    cosp = jnp.concatenate([jnp.cos(ang), jnp.cos(ang), zpad], axis=1)
    sinp = jnp.concatenate([jnp.sin(ang), jnp.sin(ang), zpad], axis=1)

    tail = jnp.zeros((LP - L_TOK, D_MODEL), F32)
    h = jnp.concatenate([full["meta"], x[0], tail], axis=0)
    ltp = jnp.concatenate([jnp.zeros((N_META, D_MODEL), F32), loss_target[0], tail], axis=0)
    row = jnp.arange(LP)[:, None]
    rmask = ((row >= N_META) & (row < L_TOK)).astype(F32)

    def vec(a, l):
        return a[l][None, :]

    wl = []
    for l in range(DEPTH):
        wl.append(dict(
            b_pad=jnp.concatenate([b_forget[l], jnp.zeros((LANE - HEADS,), F32)])[None, :],
            gq=vec(g_q_lat, l), gkv=vec(g_kv_lat, l), g1=vec(g_mix_pre, l), g2=vec(g_mix_post, l),
            g3=vec(g_ffn_pre, l), g4=vec(g_ffn_post, l)))

    def prep_ins(p_all, w):
        return [R(p_all, Q_RANK, _cb(OFF_CQ, Q_RANK)), R(p_all, KV_RANK, _cb(OFF_CKV, KV_RANK)),
                R(p_all, LANE, _cb(OFF_KPE, LANE)), R(p_all, LANE, _cb(OFF_KROT, LANE)), Pm(w["gq"]), Pm(w["gkv"]),
                R(cosp, LANE), R(sinp, LANE)]

    def merge_ins(p_all, ys):
        return [R(p_all, GW, _cb(OFF_GATE + n * D_MODEL, GW), 1) for n in range(3)] + [R(yv, GW, 0, 1) for yv in ys]

    def b16(total, w, cstep=0):
        return OR(total, w, cstep, BF16)

    (hn,) = _rw(lambda a, g: (_rms(a, g),), [R(h, D_MODEL), Pm(wl[0]["g1"])], [b16(D_MODEL, D_MODEL)], name="rms_in")
    saved = []
    for l in range(DEPTH):
        w = wl[l]
        s = dict(h=h, hn=hn)
        p_all = _mm(hn, full["w_in"], bidx=(l,), name="proj_in")
        cqn, ckvn, kper = _rw(_mla_prep, prep_ins(p_all, w),
                              [b16(Q_RANK, Q_RANK), b16(KV_RANK, KV_RANK), OR(LANE, LANE)], name="mla_prep")
        q = _mm(cqn, full["w_uq"], bidx=(l,), name="proj_q")
        kv = _mm(ckvn, full["w_ukv"], bidx=(l,), name="proj_kv")
        o_a = _mla_fwd(q, kv, kper, cosp, sinp)
        o_b = _conv_fwd(p_all, full["conv_w"], l)
        cdec = _decay_fwd(p_all, w["b_pad"])
        cq3 = cdec[:, :HEADS].T[:, :, None]
        ck3 = cdec[:, :HEADS].T[:, None, :]
        o_c = _fox_fwd(p_all, cq3, ck3)
        outs = (o_a, o_b, o_c)
        ys = [_mm(outs[n], full["w_branch"], bidx=(l, n), name="proj_branch") for n in range(3)]
        (merged,) = _rw(_merge, merge_ins(p_all, ys), [b16(D_MODEL, GW, 1)], ncol=D_MODEL // GW, name="merge")
        mix = _mm(merged, full["w_out"], bidx=(l,), name="proj_out")
        h2, hn2 = _rw(_resid_norm, [R(h, D_MODEL), R(mix, D_MODEL), Pm(w["g2"]), Pm(w["g3"])],
                      [OR(D_MODEL, D_MODEL), b16(D_MODEL, D_MODEL)], name="resid_norm")
        gu = _mm(hn2, full["w_ffn_in"], bidx=(l,), name="ffn_in")
        (act,) = _rw(_swiglu, [R(gu, 2 * D_FF)], [b16(D_FF, D_FF)], tm=TM_FF, name="swiglu")
        f = _mm(act, full["w_ffn_out"], bidx=(l,), name="ffn_out")
        s.update(p_all=p_all, cqn=cqn, ckvn=ckvn, kper=kper, q=q, kv=kv, outs=outs, cq3=cq3, ck3=ck3, ys=ys,
                 merged=merged, mix=mix, h2=h2, hn2=hn2, gu=gu, act=act, f=f)
        saved.append(s)
        if l + 1 < DEPTH:
            h, hn = _rw(_resid_norm, [R(h2, D_MODEL), R(f, D_MODEL), Pm(w["g4"]), Pm(wl[l + 1]["g1"])],
                        [OR(D_MODEL, D_MODEL), b16(D_MODEL, D_MODEL)], name="resid_norm")

    grads = {n: [None] * DEPTH for n in _SMALL + ("conv_w",)}
    mats = tuple(n for n, _, _ in _BIG if n not in ("meta", "conv_w"))
    mat_kinds = tuple(kd for n, kd, _ in _BIG if n in mats)
    via_rows = ("w_in", "w_uq")
    chip = (2 * lax.axis_index("x") + lax.axis_index("y")).astype(jnp.int32).reshape(1)
    sib_plan, chips_plan = _sibling_plan(mat_kinds), _chips_plan(len(mats))
    updates = {n: [lax.empty(_as3d(given[n]).shape, F32) for _ in range(4)] for n in mats if n not in via_rows}
    row_sums = {n: lax.empty((DEPTH, full[n].shape[1] // N_DEV, full[n].shape[2]), F32) for n in via_rows}
    landed = []
    flying = None

    def finish(layer, own, theirs, deps):
        for n, o, t in zip(mats, own, theirs):
            if n in via_rows:
                row_sums[n] = _sum_parts_layer(o, t, chip, row_sums[n], layer, "sum_" + n)
            else:
                updates[n] = _sum_adamw_layer(o, t, chip, _as3d(given[n]), _as3d(given["m_" + n]),
                                              _as3d(given["v_" + n]), updates[n], layer, deps, "adamw_" + n)

    def pair_sums(layer, srcs, from_sib):
        sums = [_pair_sum(_as3d(g), _as3d(r, 1), core, kd, "pair_sum_" + n)
                for n, kd, g, r in zip(mats, mat_kinds, srcs, from_sib)]
        return _split_start(chips_plan, 3 * len(mats), sums, [lax.empty((3,) + p.shape[1:], BF16) for p in sums],
                            "scatter_chips_start_%d" % layer)

    s, w = saved[-1], wl[-1]
    dh2, df, dg4, loss_acc = _rw(
        _loss_bwd, [R(s["h2"], D_MODEL), R(s["f"], D_MODEL), Pm(w["g4"]), R(ltp, D_MODEL), R(rmask, 1)],
        [OR(D_MODEL, D_MODEL), b16(D_MODEL, D_MODEL), OA((1, D_MODEL)), OA((1, LANE))], name="loss_bwd")
    loss = lax.psum(loss_acc[0, 0], ("x", "y", "c"))
    grads["g_ffn_post"][DEPTH - 1] = dg4[0]
    for l in reversed(range(DEPTH)):
        s, w = saved[l], wl[l]
        p_all = s["p_all"]
        gl = {}
        dact = _mm(df, full["w_ffn_out"], tb=True, bidx=(l,), deps=(flying["token"],) if flying else (), name="d_act")
        gl["w_ffn_out"] = _mm(s["act"], df, ta=True, out_dtype=BF16, name="dw_ffn_out")
        (dgu,) = _rw(_swiglu_bwd, [R(s["gu"], 2 * D_FF), R(dact, D_FF)], [b16(2 * D_FF, 2 * D_FF)], tm=TM_FF,
                     name="swiglu_bwd")
        if flying:
            srcs, from_sib = _split_wait(sib_plan, flying["send"], flying["recv"], flying["srcs"], flying["dsts"], dgu,
                                         "scatter_sibling_wait_%d" % (l + 1))
            send, recv, sums, slots, token = pair_sums(l + 1, srcs, from_sib)
            flying = dict(send=send, recv=recv, srcs=sums, dsts=slots, token=token)
        dhn2 = _mm(dgu, full["w_ffn_in"], tb=True, bidx=(l,), deps=(flying["token"],) if flying else (), name="d_hn2")
        gl["w_ffn_in"] = _mm(s["hn2"], dgu, ta=True, out_dtype=BF16, name="dw_ffn_in")
        dh, dmix, dg2, dg3 = _rw(
            _resid_norm_bwd, [R(s["h"], D_MODEL), R(s["mix"], D_MODEL), Pm(w["g2"]), Pm(w["g3"]), R(dh2, D_MODEL),
                              R(dhn2, D_MODEL)],
            [OR(D_MODEL, D_MODEL), b16(D_MODEL, D_MODEL), OA((1, D_MODEL)), OA((1, D_MODEL))], name="resid_norm_bwd")
        grads["g_mix_post"][l], grads["g_ffn_pre"][l] = dg2[0], dg3[0]
        dmerged = _mm(dmix, full["w_out"], tb=True, bidx=(l,), name="d_merged")
        gl["w_out"] = _mm(s["merged"], dmix, ta=True, out_dtype=BF16, name="dw_out")
        mb = _rw(_merge_bwd, merge_ins(p_all, s["ys"]) + [R(dmerged, GW, 0, 1)], [b16(D_MODEL, GW, 1)] * 6,
                 ncol=D_MODEL // GW, name="merge_bwd")
        dgate, dys = mb[:3], mb[3:]
        dos = [_mm(dys[n], full["w_branch"], tb=True, bidx=(l, n), name="d_branch") for n in range(3)]
        gl["w_branch"] = lax.empty(full["w_branch"].shape[1:], BF16)
        for n in range(3):
            gl["w_branch"] = _mm(s["outs"][n], dys[n], ta=True, stack=(gl["w_branch"], (n,)), name="dw_branch")
        dfq, dfk, dfv, dcq3, dck3 = _fox_bwd(p_all, s["cq3"], s["ck3"], dos[2])
        dc = jnp.concatenate([dcq3[:, :, 0].T + dck3[:, 0, :].T, jnp.zeros((LP, LANE - HEADS), F32)], axis=1)
        dfl, db = _decay_bwd(p_all, w["b_pad"], dc)
        grads["b_forget"][l] = db[0, :HEADS]
        dconv, dcw = _conv_bwd(p_all, full["conv_w"], l, dos[1])
        grads["conv_w"][l] = dcw
        dq, dkv, dkper = _mla_bwd(s["q"], s["kv"], s["kper"], cosp, sinp, dos[0])
        dcqn = _mm(dq, full["w_uq"], tb=True, bidx=(l,), name="d_cqn")
        gl["w_uq"] = _mm(s["cqn"], dq, ta=True, out_dtype=BF16, name="dw_uq")
        dckvn = _mm(dkv, full["w_ukv"], tb=True, bidx=(l,), name="d_ckvn")
        gl["w_ukv"] = _mm(s["ckvn"], dkv, ta=True, out_dtype=BF16, name="dw_ukv")
        dcq, dckv, dkpe, dkrot, dgq, dgkv = _rw(
            _mla_prep_bwd, prep_ins(p_all, w) + [R(dcqn, Q_RANK), R(dckvn, KV_RANK), R(dkper, LANE)],
            [b16(Q_RANK, Q_RANK), b16(KV_RANK, KV_RANK), b16(LANE, LANE), b16(LANE, LANE), OA((1, Q_RANK)),
             OA((1, KV_RANK))], name="mla_prep_bwd")
        grads["g_q_lat"][l], grads["g_kv_lat"][l] = dgq[0], dgkv[0]
        dp = jnp.concatenate([*dgate, dconv[0], dconv[1], dconv[2], dfq, dfk.astype(BF16), dfv.astype(BF16), dcq, dckv,
                              dkpe, dkrot, dfl, jnp.zeros((LP, LANE), BF16)], axis=1)
        dhn = _mm(dp, full["w_in"], tb=True, bidx=(l,), name="d_hn")
        gl["w_in"] = _mm(s["hn"], dp, ta=True, out_dtype=BF16, name="dw_in")
        if flying:
            sums, slots = _split_wait(chips_plan, flying["send"], flying["recv"], flying["srcs"], flying["dsts"], dhn,
                                      "scatter_chips_wait_%d" % (l + 1))
            landed.append((l + 1, sums, slots))
        parts_l = [gl[n] for n in mats]
        send, recv, srcs, dsts, token = _split_start(
            sib_plan, 4 * len(mats), parts_l,
            [lax.empty((4,) + _slice_shape(g.shape, kd), BF16) for g, kd in zip(parts_l, mat_kinds)],
            "scatter_sibling_start_%d" % l)
        flying = dict(send=send, recv=recv, srcs=srcs, dsts=dsts, token=token)
        if l > 0:
            sp, wp = saved[l - 1], wl[l - 1]
            dh2, df, dg4, dg1 = _rw(
                _resid_norm_bwd, [R(sp["h2"], D_MODEL), R(sp["f"], D_MODEL), Pm(wp["g4"]), Pm(w["g1"]),
                                  R(dh, D_MODEL), R(dhn, D_MODEL)],
                [OR(D_MODEL, D_MODEL), b16(D_MODEL, D_MODEL), OA((1, D_MODEL)), OA((1, D_MODEL))], name="resid_norm_bwd")
            grads["g_ffn_post"][l - 1], grads["g_mix_pre"][l] = dg4[0], dg1[0]
        else:
            dh0, dg1 = _rw(_rms_bwd, [R(s["h"], D_MODEL), Pm(w["g1"]), R(dhn, D_MODEL), R(dh, D_MODEL)],
                           [OR(D_MODEL, D_MODEL), OA((1, D_MODEL))], name="rms_in_bwd")
            grads["g_mix_pre"][0] = dg1[0]
    grad_x = dh0[N_META:L_TOK][None]
    gfull = {n: jnp.stack(grads[n]) for n in grads}
    gfull["meta"] = dh0[:N_META]

    srcs, from_sib = _split_wait(sib_plan, flying["send"], flying["recv"], flying["srcs"], flying["dsts"], dh0,
                                 "scatter_sibling_wait_0")
    send, recv, sums, slots, token = pair_sums(0, srcs, from_sib)
    for layer, own, theirs in landed:
        finish(layer, own, theirs, (token,))
    sums, slots = _split_wait(chips_plan, send, recv, sums, slots, updates[mats[-1]][0], "scatter_chips_wait_0")
    finish(0, sums, slots, ())

    few = ("meta", "conv_w")
    partial = [gfull[n] for n in few]
    from_sib = _swap_sibling(partial, ("col", "col"), "scatter_sibling_few")
    chip_sums = [_pair_sum(_as3d(g), _as3d(r, 1), core, "col", "pair_sum_" + n)
                 for n, g, r in zip(few, partial, from_sib)]
    parts = dict(zip(few, _swap_chips(chip_sums, "scatter_chips_few")))

    def by_dest(a):
        return jnp.moveaxis(a.reshape(a.shape[:-1] + (N_DEV, a.shape[-1] // N_DEV)), -2, 0).astype(BF16)

    cols_in, cols_uq = _all_to_all([by_dest(_unext_w_in(row_sums["w_in"])), by_dest(_unext_w_uq(row_sums["w_uq"]))],
                                   ("raw", "raw"), ("row", "row"), "columns_of_row_sums")
    parts["w_in"], parts["w_uq"] = _as3d(cols_in)[None], _as3d(cols_uq)[None]
    out = {n: [r.reshape(given[n].shape) for r in updates[n]] for n in updates}
    for n in parts:
        res = _sum_adamw(parts[n], _as3d(given[n]), _as3d(given["m_" + n]), _as3d(given["v_" + n]), "adamw_" + n)
        out[n] = [r.reshape(given[n].shape) for r in res]

    def pack(d):
        flat = jnp.concatenate([d[n].reshape(-1) for n in _SMALL])
        return jnp.concatenate([flat, jnp.zeros((-flat.shape[0]) % (8 * LANE), F32)]).reshape(-1, LANE)

    (small_parts,) = _all_gather([pack(gfull)], ("raw",), "gather_small_grads")
    res = _sum_adamw(small_parts[:, None], pack(given)[None], pack({n: given["m_" + n] for n in _SMALL})[None],
                     pack({n: given["v_" + n] for n in _SMALL})[None], "adamw_small")
    off = 0
    for n in _SMALL:
        size = int(np.prod(given[n].shape))
        out[n] = [r.reshape(-1)[off:off + size].reshape(given[n].shape) for r in res]
        off += size

    return (loss, grad_x, *[out[n][0] for n in _ORDER], *[out[n][1] for n in _ORDER], *[out[n][2] for n in _ORDER],
            *[out[n][3] for n in _ORDER])
```

```python
import functools

import numpy as np
import jax
import jax.numpy as jnp
from jax import lax
from jax.experimental import pallas as pl
from jax.experimental.pallas import tpu as pltpu

D_MODEL = 2048
SEQ = 2048
DEPTH = 4
Q_RANK = 512
KV_RANK = 512
D_FF = 5632
N_META = 16
HEADS = 8
HD = 128
ROPE = 64
BW = HEADS * HD
EPS = 1e-6
NEG_INF = -1e30
ROPE_THETA = 10000.0
N_DEV = 8
LANE = 128
L_TOK = N_META + SEQ
LP = -(-L_TOK // LANE) * LANE
D_IN = Q_RANK + KV_RANK + ROPE + 6 * BW + HEADS + 3 * D_MODEL
MLA_SCALE = (HD + ROPE) ** -0.5
FOX_SCALE = HD ** -0.5
ADAM_LR, ADAM_B1, ADAM_B2, ADAM_EPS, ADAM_WD, ADAM_STEP = 0.001, 0.9, 0.999, 1e-08, 0.01, 10
VMEM_LIMIT = 48 * 1024 * 1024

F32 = jnp.float32
BF16 = jnp.bfloat16
MESH = pl.DeviceIdType.MESH

OFF_GATE = 0
OFF_CONV = 3 * D_MODEL
OFF_FOX = OFF_CONV + 3 * BW
OFF_CQ = OFF_FOX + 3 * BW
OFF_CKV = OFF_CQ + Q_RANK
OFF_KPE = OFF_CKV + KV_RANK
OFF_KROT = OFF_KPE + LANE
OFF_FL = OFF_KROT + LANE
W_ALL = OFF_FL + 2 * LANE
GW = 512 if D_MODEL % 512 == 0 else 256
TM_FF = 64
CW = 128
QG = 3 * LANE


def _pick(n, prefs):
    for p in prefs:
        if n % p == 0:
            return p
    return n


TM_ROW = 128
TQ = _pick(LP, (272, 128))


def _cb(off, w):
    assert off % w == 0, (off, w)
    return off // w


def _pcall(body, **kw):
    return pl.pallas_call(body, **kw)


def _params(sem):
    return pltpu.CompilerParams(dimension_semantics=sem, vmem_limit_bytes=VMEM_LIMIT)


def _bf(x):
    return x.astype(BF16)


def _dot(a, b, dims, **kw):
    return lax.dot_general(_bf(a), _bf(b), (dims, ((), ())), preferred_element_type=F32, **kw)


_NN = ((1,), (0,))
_NT = ((1,), (1,))
_TN = ((0,), (0,))
_ANY = pl.BlockSpec(memory_space=pl.ANY)


MM_VMEM_BUDGET = 38 * 1024 * 1024
HBM_BYTES_PER_STEP = 1 << 20


def _divisors(n, prefs):
    return [p for p in prefs if n % p == 0] or [n]


def _mm_tiles(m, n, kd, a_bytes, b_bytes, o_bytes):
    best = None
    for tm in _divisors(m, (2176, 2048, 1088, 1024, 544, 512, 272, 256, 128)):
        for tn in _divisors(n, (2048, 1536, 1024, 768, 512, 384, 256, 128)):
            for tk in _divisors(kd, (2816, 2304, 2176, 2048, 1536, 1408, 1024, 768, 544, 512, 384, 272, 256, 128)):
                nk = kd // tk
                vmem = 2 * (tm * tk * a_bytes + tk * tn * b_bytes + tm * tn * o_bytes) + 2 * tm * tn * 4
                vmem += (tm * tk * 2 if a_bytes == 4 else 0) + (tk * tn * 2 if b_bytes == 4 else 0)
                if vmem > MM_VMEM_BUDGET:
                    continue
                steps = (m // tm) * (n // tn) * nk
                cost = (m * kd * a_bytes * (1 if nk == 1 else n // tn) + kd * n * b_bytes * (m // tm)
                        + steps * HBM_BYTES_PER_STEP)
                if best is None or cost < best[0]:
                    best = (cost, tm, tn, tk)
    assert best is not None, (m, n, kd)
    return best[1:]


def _mm(a, b, *, ta=False, tb=False, bidx=(), stack=None, out_dtype=F32, deps=(), name):
    if ta:
        kd, m = a.shape
    else:
        m, kd = a.shape
    nlead = len(bidx)
    if tb:
        n, kd2 = b.shape[nlead:]
    else:
        kd2, n = b.shape[nlead:]
    assert kd == kd2, (a.shape, b.shape, ta, tb)
    if stack is not None:
        out_dtype = stack[0].dtype
    tm, tn, tk = _mm_tiles(m, n, kd, a.dtype.itemsize, b.dtype.itemsize, jnp.dtype(out_dtype).itemsize)
    nk = kd // tk
    dims = _TN if ta else (_NT if tb else _NN)

    def body(a_ref, b_ref, *rest):
        o_ref, acc_ref = rest[-2:]
        if nk == 1:
            o_ref[...] = _dot(a_ref[...], b_ref[...], dims).astype(o_ref.dtype)
            return
        k = pl.program_id(2)

        @pl.when(k == 0)
        def _():
            acc_ref[...] = jnp.zeros_like(acc_ref)

        acc_ref[...] += _dot(a_ref[...], b_ref[...], dims)

        @pl.when(k == nk - 1)
        def _():
            o_ref[...] = acc_ref[...].astype(o_ref.dtype)

    lead = (None,) * nlead
    a_spec = pl.BlockSpec((tk, tm), lambda i, j, k: (k, i)) if ta else pl.BlockSpec((tm, tk), lambda i, j, k: (i, k))
    if tb:
        b_spec = pl.BlockSpec(lead + (tn, tk), lambda i, j, k: bidx + (j, k))
    else:
        b_spec = pl.BlockSpec(lead + (tk, tn), lambda i, j, k: bidx + (k, j))
    in_specs, args, extra = [a_spec, b_spec], [a, b], {}
    if stack is None:
        out_spec = pl.BlockSpec((tm, tn), lambda i, j, k: (i, j))
        out_shape = jax.ShapeDtypeStruct((m, n), out_dtype)
    else:
        buf, sidx = stack
        assert buf.shape[len(sidx):] == (m, n), (buf.shape, sidx, m, n)
        in_specs.append(_ANY)
        args.append(buf)
        extra = dict(input_output_aliases={2: 0})
        out_spec = pl.BlockSpec((None,) * len(sidx) + (tm, tn), lambda i, j, k: sidx + (i, j))
        out_shape = jax.ShapeDtypeStruct(buf.shape, buf.dtype)
    in_specs += [_ANY] * len(deps)
    args += list(deps)
    return _pcall(
        body, name=name, grid=(m // tm, n // tn, nk), in_specs=in_specs, out_specs=out_spec, out_shape=out_shape,
        scratch_shapes=[pltpu.VMEM((tm, tn) if nk > 1 else (8, LANE), F32)],
        compiler_params=_params(("parallel", "parallel", "arbitrary")), **extra)(*args)


class R:
    def __init__(self, arr, w, cb0=0, cstep=0):
        self.arr, self.w, self.cb0, self.cstep = arr, w, cb0, cstep


class Pm:
    def __init__(self, arr):
        self.arr = arr


class OR:
    def __init__(self, total, w, cstep=0, dtype=F32):
        self.total, self.w, self.cstep, self.dtype = total, w, cstep, dtype


class OA:
    def __init__(self, shape):
        self.shape = shape


def _rw(fn, ins, outs, *, name, ncol=1, tm=None):
    tm = TM_ROW if tm is None else tm
    nrow = LP // tm
    n_in = len(ins)

    def body(*refs):
        j, i = pl.program_id(0), pl.program_id(1)
        res = fn(*[r[...] for r in refs[:n_in]])
        for o, ref, val in zip(outs, refs[n_in:], res):
            if isinstance(o, OR):
                ref[...] = val.astype(ref.dtype)
            else:
                @pl.when((i == 0) & (j == 0))
                def _(ref=ref):
                    ref[...] = jnp.zeros_like(ref)

                ref[...] += val

    in_specs = []
    for s in ins:
        if isinstance(s, R):
            in_specs.append(pl.BlockSpec((tm, s.w), lambda j, i, s=s: (i, s.cb0 + j * s.cstep)))
        else:
            in_specs.append(pl.BlockSpec(s.arr.shape, lambda j, i, nd=s.arr.ndim: (0,) * nd))
    out_specs, out_shape = [], []
    for o in outs:
        if isinstance(o, OR):
            out_specs.append(pl.BlockSpec((tm, o.w), lambda j, i, o=o: (i, j * o.cstep)))
            out_shape.append(jax.ShapeDtypeStruct((LP, o.total), o.dtype))
        else:
            out_specs.append(pl.BlockSpec(o.shape, lambda j, i: (0, 0)))
            out_shape.append(jax.ShapeDtypeStruct(o.shape, F32))
    return _pcall(body, name=name, grid=(ncol, nrow), in_specs=in_specs, out_specs=out_specs, out_shape=out_shape,
                  compiler_params=_params(("arbitrary", "arbitrary")))(*[s.arr for s in ins])


def _rms(x, g):
    return x * lax.rsqrt(jnp.mean(x * x, axis=-1, keepdims=True) + EPS) * g


def _resid_norm(h, z, ga, gb):
    h2 = h + _rms(z, ga)
    return h2, _rms(h2, gb)


def _resid_norm_bwd(h, z, ga, gb, dh2, dhn2):
    _, vjp = jax.vjp(_resid_norm, h, z, ga, gb)
    return vjp((dh2, dhn2))


def _rms_bwd(h, g, dhn, dh_in):
    _, vjp = jax.vjp(_rms, h, g)
    dh, dg = vjp(dhn)
    return dh + dh_in, dg


def _loss_fn(h2, f, g4, lt, rmask):
    h3 = h2 + _rms(f, g4)
    err = jnp.square(h3 - lt)
    return 0.5 * jnp.sum(jnp.mean(err, axis=-1, keepdims=True) * rmask)


def _loss_bwd(h2, f, g4, lt, rmask):
    val, (dh2, df, dg4) = jax.value_and_grad(_loss_fn, argnums=(0, 1, 2))(h2, f, g4, lt, rmask)
    return dh2, df, dg4, jnp.broadcast_to(val, (1, LANE))


def _mla_prep(cq, ckv, kpe, krot, gq, gkv, cosp, sinp):
    return _rms(cq, gq), _rms(ckv, gkv), kpe * cosp + krot * sinp


def _mla_prep_bwd(cq, ckv, kpe, krot, gq, gkv, cosp, sinp, dcqn, dckvn, dkper):
    _, vjp = jax.vjp(lambda a, b, c, d, e, f: _mla_prep(a, b, c, d, e, f, cosp, sinp), cq, ckv, kpe, krot, gq, gkv)
    return vjp((dcqn, dckvn, dkper))


def _merge(g0, g1, g2, y0, y1, y2):
    return (jax.nn.sigmoid(g0) * y0 + jax.nn.sigmoid(g1) * y1 + jax.nn.sigmoid(g2) * y2,)


def _merge_bwd(g0, g1, g2, y0, y1, y2, dm):
    _, vjp = jax.vjp(_merge, g0, g1, g2, y0, y1, y2)
    return vjp((dm,))


def _swiglu(gu):
    g, u = gu[:, :D_FF], gu[:, D_FF:]
    return (g * jax.nn.sigmoid(g) * u,)


def _swiglu_bwd(gu, dact):
    _, vjp = jax.vjp(_swiglu, gu)
    return vjp((dact,))


def _causal_probs(s, iq):
    qpos = iq * TQ + lax.broadcasted_iota(jnp.int32, s.shape, 0)
    kpos = lax.broadcasted_iota(jnp.int32, s.shape, 1)
    s = jnp.where(kpos <= qpos, s, NEG_INF)
    e = jnp.exp(s - jnp.max(s, axis=-1, keepdims=True))
    return e / jnp.sum(e, axis=-1, keepdims=True)


def _softmax_vjp(p, dp):
    return p * (dp - jnp.sum(p * dp, axis=-1, keepdims=True))


def _mla_specs():
    return [pl.BlockSpec((TQ, QG), lambda h, i: (i, h)),
            pl.BlockSpec((LP, 2 * HD), lambda h, i: (0, h)),
            pl.BlockSpec((LP, LANE), lambda h, i: (0, 0)),
            pl.BlockSpec((TQ, LANE), lambda h, i: (i, 0)),
            pl.BlockSpec((TQ, LANE), lambda h, i: (i, 0))]


def _mla_parts(q_ref, kv_ref, cos_ref, sin_ref):
    q, kv = q_ref[...], kv_ref[...]
    qn = q[:, :HD]
    qp = q[:, HD:2 * HD] * cos_ref[...] + q[:, 2 * HD:] * sin_ref[...]
    return qn, qp, kv[:, :HD], kv[:, HD:]


def _mla_fwd(q, kv, kper, cosp, sinp):
    def body(q_ref, kv_ref, kp_ref, cos_ref, sin_ref, o_ref):
        qn, qp, kn, v = _mla_parts(q_ref, kv_ref, cos_ref, sin_ref)
        s = (_dot(qn, kn, _NT) + _dot(qp, kp_ref[...], _NT)) * MLA_SCALE
        o_ref[...] = _dot(_causal_probs(s, pl.program_id(1)), v, _NN).astype(o_ref.dtype)

    return _pcall(body, name="mla_fwd", grid=(HEADS, LP // TQ), in_specs=_mla_specs(),
                  out_specs=pl.BlockSpec((TQ, HD), lambda h, i: (i, h)),
                  out_shape=jax.ShapeDtypeStruct((LP, BW), BF16),
                  compiler_params=_params(("parallel", "parallel")))(q, kv, kper, cosp, sinp)


def _mla_bwd(q, kv, kper, cosp, sinp, do):
    def body(q_ref, kv_ref, kp_ref, cos_ref, sin_ref, do_ref, dq_ref, dkv_ref, dkp_ref):
        h, iq = pl.program_id(0), pl.program_id(1)
        qn, qp, kn, v = _mla_parts(q_ref, kv_ref, cos_ref, sin_ref)
        kp, dout = kp_ref[...], do_ref[...]
        p = _causal_probs((_dot(qn, kn, _NT) + _dot(qp, kp, _NT)) * MLA_SCALE, iq)
        ds = _softmax_vjp(p, _dot(dout, v, _NT)) * MLA_SCALE
        dqp = _dot(ds, kp, _NN)
        dq = jnp.concatenate([_dot(ds, kn, _NN), dqp * cos_ref[...], dqp * sin_ref[...]], axis=1)
        dq_ref[...] = dq.astype(dq_ref.dtype)

        @pl.when(iq == 0)
        def _():
            dkv_ref[...] = jnp.zeros_like(dkv_ref)

        dkv_ref[...] += jnp.concatenate([_dot(ds, qn, _TN), _dot(p, dout, _TN)], axis=1)

        @pl.when((iq == 0) & (h == 0))
        def _():
            dkp_ref[...] = jnp.zeros_like(dkp_ref)

        dkp_ref[...] += _dot(ds, qp, _TN)

    return _pcall(
        body, name="mla_bwd", grid=(HEADS, LP // TQ),
        in_specs=_mla_specs() + [pl.BlockSpec((TQ, HD), lambda h, i: (i, h))],
        out_specs=[pl.BlockSpec((TQ, QG), lambda h, i: (i, h)), pl.BlockSpec((LP, 2 * HD), lambda h, i: (0, h)),
                   pl.BlockSpec((LP, LANE), lambda h, i: (0, 0))],
        out_shape=[jax.ShapeDtypeStruct((LP, HEADS * QG), BF16), jax.ShapeDtypeStruct((LP, 2 * BW), F32),
                   jax.ShapeDtypeStruct((LP, LANE), F32)],
        compiler_params=_params(("arbitrary", "arbitrary")))(q, kv, kper, cosp, sinp, do)


def _fox_specs():
    cq, ck, cv = _cb(OFF_FOX, HD), _cb(OFF_FOX + BW, HD), _cb(OFF_FOX + 2 * BW, HD)
    return [pl.BlockSpec((TQ, HD), lambda h, i: (i, cq + h)),
            pl.BlockSpec((LP, HD), lambda h, i: (0, ck + h)),
            pl.BlockSpec((LP, HD), lambda h, i: (0, cv + h)),
            pl.BlockSpec((1, TQ, 1), lambda h, i: (h, i, 0)),
            pl.BlockSpec((1, 1, LP), lambda h, i: (h, 0, 0))]


def _fox_probs(q_ref, k_ref, cq_ref, ck_ref, iq):
    s = _dot(q_ref[...], k_ref[...], _NT) * FOX_SCALE + (cq_ref[0] - ck_ref[0])
    return _causal_probs(s, iq)


def _fox_fwd(p_all, cq3, ck3):
    def body(q_ref, k_ref, v_ref, cq_ref, ck_ref, o_ref):
        p = _fox_probs(q_ref, k_ref, cq_ref, ck_ref, pl.program_id(1))
        o_ref[...] = _dot(p, v_ref[...], _NN).astype(o_ref.dtype)

    return _pcall(body, name="fox_fwd", grid=(HEADS, LP // TQ), in_specs=_fox_specs(),
                  out_specs=pl.BlockSpec((TQ, HD), lambda h, i: (i, h)),
                  out_shape=jax.ShapeDtypeStruct((LP, BW), BF16),
                  compiler_params=_params(("parallel", "parallel")))(p_all, p_all, p_all, cq3, ck3)


def _fox_bwd(p_all, cq3, ck3, do):
    def body(q_ref, k_ref, v_ref, cq_ref, ck_ref, do_ref, dq_ref, dk_ref, dv_ref, dcq_ref, dck_ref):
        iq = pl.program_id(1)
        p = _fox_probs(q_ref, k_ref, cq_ref, ck_ref, iq)
        dout = do_ref[...]
        ds = _softmax_vjp(p, _dot(dout, v_ref[...], _NT))
        dss = ds * FOX_SCALE
        dq_ref[...] = _dot(dss, k_ref[...], _NN).astype(dq_ref.dtype)
        dcq_ref[0] = jnp.sum(ds, axis=1, keepdims=True)

        @pl.when(iq == 0)
        def _():
            dk_ref[...] = jnp.zeros_like(dk_ref)
            dv_ref[...] = jnp.zeros_like(dv_ref)
            dck_ref[...] = jnp.zeros_like(dck_ref)

        dk_ref[...] += _dot(dss, q_ref[...], _TN)
        dv_ref[...] += _dot(p, dout, _TN)
        dck_ref[0] -= jnp.sum(ds, axis=0, keepdims=True)

    head_rows = pl.BlockSpec((TQ, HD), lambda h, i: (i, h))
    head_all = pl.BlockSpec((LP, HD), lambda h, i: (0, h))
    return _pcall(
        body, name="fox_bwd", grid=(HEADS, LP // TQ), in_specs=_fox_specs() + [head_rows],
        out_specs=[head_rows, head_all, head_all, pl.BlockSpec((1, TQ, 1), lambda h, i: (h, i, 0)),
                   pl.BlockSpec((1, 1, LP), lambda h, i: (h, 0, 0))],
        out_shape=[jax.ShapeDtypeStruct((LP, BW), BF16)] + [jax.ShapeDtypeStruct((LP, BW), F32)] * 2
        + [jax.ShapeDtypeStruct((HEADS, LP, 1), F32), jax.ShapeDtypeStruct((HEADS, 1, LP), F32)],
        compiler_params=_params(("arbitrary", "arbitrary")))(p_all, p_all, p_all, cq3, ck3, do)


def _log_sigmoid(x):
    return jnp.minimum(x, 0.0) - jnp.log(1.0 + jnp.exp(-jnp.abs(x)))


def _decay_fwd(p_all, b_pad):
    tc = TM_ROW

    def body(fl_ref, b_ref, c_ref):
        lf = _log_sigmoid(fl_ref[...] + b_ref[...])
        r = pl.program_id(0) * tc + lax.broadcasted_iota(jnp.int32, (tc, LP), 0)
        s = lax.broadcasted_iota(jnp.int32, (tc, LP), 1)
        c_ref[...] = jnp.dot((s <= r).astype(F32), lf, precision=lax.Precision.HIGHEST, preferred_element_type=F32)

    return _pcall(body, name="decay_fwd", grid=(LP // tc,),
                  in_specs=[pl.BlockSpec((LP, LANE), lambda i: (0, _cb(OFF_FL, LANE))),
                            pl.BlockSpec((1, LANE), lambda i: (0, 0))],
                  out_specs=pl.BlockSpec((tc, LANE), lambda i: (i, 0)),
                  out_shape=jax.ShapeDtypeStruct((LP, LANE), F32), compiler_params=_params(("parallel",)))(p_all, b_pad)


def _decay_bwd(p_all, b_pad, dc):
    tc = TM_ROW

    def body(fl_ref, b_ref, dc_ref, dfl_ref, db_ref):
        i = pl.program_id(0)
        r = i * tc + lax.broadcasted_iota(jnp.int32, (tc, LP), 0)
        t = lax.broadcasted_iota(jnp.int32, (tc, LP), 1)
        dlf = jnp.dot((t >= r).astype(F32), dc_ref[...], precision=lax.Precision.HIGHEST, preferred_element_type=F32)
        dfl = dlf * jax.nn.sigmoid(-(fl_ref[...] + b_ref[...]))
        dfl_ref[...] = dfl.astype(dfl_ref.dtype)

        @pl.when(i == 0)
        def _():
            db_ref[...] = jnp.zeros_like(db_ref)

        db_ref[...] += jnp.sum(dfl, axis=0, keepdims=True)

    return _pcall(body, name="decay_bwd", grid=(LP // tc,),
                  in_specs=[pl.BlockSpec((tc, LANE), lambda i: (i, _cb(OFF_FL, LANE))),
                            pl.BlockSpec((1, LANE), lambda i: (0, 0)), pl.BlockSpec((LP, LANE), lambda i: (0, 0))],
                  out_specs=[pl.BlockSpec((tc, LANE), lambda i: (i, 0)), pl.BlockSpec((1, LANE), lambda i: (0, 0))],
                  out_shape=[jax.ShapeDtypeStruct((LP, LANE), BF16), jax.ShapeDtypeStruct((1, LANE), F32)],
                  compiler_params=_params(("arbitrary",)))(p_all, b_pad, dc)


def _conv_specs():
    c0 = _cb(OFF_CONV, CW)
    step = BW // CW
    return [pl.BlockSpec((LP, CW), lambda j: (0, c0 + j)), pl.BlockSpec((LP, CW), lambda j: (0, c0 + step + j)),
            pl.BlockSpec((LP, CW), lambda j: (0, c0 + 2 * step + j)),
            pl.BlockSpec((3, CW), lambda j: (0, j))]


def _shift_down(x, k, t):
    return jnp.where(t >= k, pltpu.roll(x, k, 0), 0.0)


def _shift_up(x, k, t):
    return jnp.where(t < LP - k, pltpu.roll(x, LP - k, 0), 0.0)


def _conv_fwd(p_all, cw):
    def body(b_ref, c_ref, x_ref, w_ref, o_ref):
        t = lax.broadcasted_iota(jnp.int32, (LP, CW), 0)
        uu = c_ref[...] * x_ref[...]
        w = w_ref[...]
        u = w[2:3] * uu + w[1:2] * _shift_down(uu, 1, t) + w[0:1] * _shift_down(uu, 2, t)
        o_ref[...] = (b_ref[...] * u).astype(o_ref.dtype)

    return _pcall(body, name="conv_fwd", grid=(BW // CW,), in_specs=_conv_specs(),
                  out_specs=pl.BlockSpec((LP, CW), lambda j: (0, j)), out_shape=jax.ShapeDtypeStruct((LP, BW), BF16),
                  compiler_params=_params(("parallel",)))(p_all, p_all, p_all, cw)


def _conv_bwd(p_all, cw, do):
    def body(b_ref, c_ref, x_ref, w_ref, do_ref, d_ref, dw_ref):
        t = lax.broadcasted_iota(jnp.int32, (LP, CW), 0)
        cc, xx, w, dout = c_ref[...], x_ref[...], w_ref[...], do_ref[...]
        uu = cc * xx
        s1, s2 = _shift_down(uu, 1, t), _shift_down(uu, 2, t)
        u = w[2:3] * uu + w[1:2] * s1 + w[0:1] * s2
        du = dout * b_ref[...]
        duu = w[2:3] * du + w[1:2] * _shift_up(du, 1, t) + w[0:1] * _shift_up(du, 2, t)
        d_ref[0] = (dout * u).astype(d_ref.dtype)
        d_ref[1] = (duu * xx).astype(d_ref.dtype)
        d_ref[2] = (duu * cc).astype(d_ref.dtype)
        dw_ref[0:1, :] = jnp.sum(du * s2, axis=0, keepdims=True)
        dw_ref[1:2, :] = jnp.sum(du * s1, axis=0, keepdims=True)
        dw_ref[2:3, :] = jnp.sum(du * uu, axis=0, keepdims=True)

    return _pcall(body, name="conv_bwd", grid=(BW // CW,),
                  in_specs=_conv_specs() + [pl.BlockSpec((LP, CW), lambda j: (0, j))],
                  out_specs=[pl.BlockSpec((3, LP, CW), lambda j: (0, 0, j)), pl.BlockSpec((3, CW), lambda j: (0, j))],
                  out_shape=[jax.ShapeDtypeStruct((3, LP, BW), BF16), jax.ShapeDtypeStruct((3, BW), F32)],
                  compiler_params=_params(("parallel",)))(p_all, p_all, p_all, cw, do)


def _place():
    return lax.axis_index("x"), lax.axis_index("y"), lax.axis_index("c")


def _dest_slice(ref, kind, d):
    if kind == "raw":
        return ref.at[d]
    nd = len(ref.shape)
    if kind == "col":
        w = ref.shape[-1] // N_DEV
        return ref.at[(slice(None),) * (nd - 1) + (pl.ds(d * w, w),)]
    r = ref.shape[-2] // N_DEV
    return ref.at[(slice(None),) * (nd - 2) + (pl.ds(d * r, r), slice(None))]


def _unsharded_shape(shape, kind, lead=N_DEV):
    if kind == "raw":
        return (lead,) + shape
    if kind == "col":
        return shape[:-1] + (N_DEV * shape[-1],)
    return shape[:-2] + (N_DEV * shape[-2], shape[-1])


def _all_gather(arrs, kinds, name):
    n = len(arrs)

    def body(*refs):
        ins, outs = refs[:n], refs[n:2 * n]
        send, recv, loc = refs[2 * n:]
        x, y, c = _place()
        sib = (x, y, 1 - c)
        chips = [(1 - x, y), (x, 1 - y), (1 - x, 1 - y)]

        def idx(px, py, pc):
            return 4 * px + 2 * py + pc

        def copy(j, k, block, to, src=None):
            dst = _dest_slice(outs[k], kinds[k], idx(*block))
            return pltpu.make_async_remote_copy(src_ref=dst if src is None else src, dst_ref=dst, send_sem=send.at[j, k],
                                                recv_sem=recv.at[j, k], device_id=to, device_id_type=MESH)

        me = (x, y, c)
        mine = [pltpu.make_async_copy(ins[k], _dest_slice(outs[k], kinds[k], idx(*me)), loc.at[k]) for k in range(n)]
        for cp in mine:
            cp.start()
        first = [copy(0, k, me, sib, src=ins[k]) for k in range(n)]
        first += [copy(1 + j, k, me, (*chip, c), src=ins[k]) for j, chip in enumerate(chips) for k in range(n)]
        for cp in first:
            cp.start()
        passed = []
        for j, chip in enumerate(chips):
            for k in range(n):
                copy(1 + j, k, (*chip, c), me, src=ins[k]).wait_recv()
                cp = copy(4 + j, k, (*chip, c), sib)
                cp.start()
                passed.append(cp)
        for k in range(n):
            copy(0, k, sib, me, src=ins[k]).wait_recv()
        for j, chip in enumerate(chips):
            for k in range(n):
                copy(4 + j, k, (*chip, 1 - c), me).wait_recv()
        for cp in first + passed:
            cp.wait_send()
        for cp in mine:
            cp.wait()

    return _pcall(body, name=name, in_specs=[_ANY] * n, out_specs=[_ANY] * n,
                  out_shape=[jax.ShapeDtypeStruct(_unsharded_shape(a.shape, kd), a.dtype) for a, kd in zip(arrs, kinds)],
                  scratch_shapes=[pltpu.SemaphoreType.DMA((7, n)), pltpu.SemaphoreType.DMA((7, n)),
                                  pltpu.SemaphoreType.DMA((n,))])(*arrs)


def _shard_shape(a, kind):
    if kind == "raw":
        return a.shape[2:]
    if kind == "col":
        return a.shape[:-1] + (a.shape[-1] // N_DEV,)
    return a.shape[:-2] + (a.shape[-2] // N_DEV, a.shape[-1])


def _swap_sibling(arrs, kinds, name):
    n = len(arrs)
    npieces = sum(1 if kd == "raw" else 4 for kd in kinds)

    def body(*refs):
        ins, outs = refs[:n], refs[n:2 * n]
        send, recv = refs[2 * n:]
        x, y, c = _place()
        pieces = []
        for k in range(n):
            if kinds[k] == "raw":
                pieces.append((ins[k].at[1 - c], outs[k]))
            else:
                pieces += [(_dest_slice(ins[k], kinds[k], 2 * p + 1 - c), outs[k].at[p]) for p in range(4)]
        cps = [pltpu.make_async_remote_copy(src_ref=src, dst_ref=dst, send_sem=send.at[i], recv_sem=recv.at[i],
                                            device_id=(x, y, 1 - c), device_id_type=MESH)
               for i, (src, dst) in enumerate(pieces)]
        for cp in cps:
            cp.start()
        for cp in cps:
            cp.wait()

    return _pcall(body, name=name, in_specs=[_ANY] * n, out_specs=[_ANY] * n,
                  out_shape=[jax.ShapeDtypeStruct((4,) + _shard_shape(a, kd), a.dtype) for a, kd in zip(arrs, kinds)],
                  scratch_shapes=[pltpu.SemaphoreType.DMA((npieces,)), pltpu.SemaphoreType.DMA((npieces,))])(*arrs)


def _swap_chips(arrs, name):
    n = len(arrs)

    def body(*refs):
        ins, outs = refs[:n], refs[n:2 * n]
        send, recv, loc = refs[2 * n:]
        x, y, c = _place()
        mychip = 2 * x + y
        mine = [pltpu.make_async_copy(ins[k].at[mychip], outs[k].at[mychip], loc.at[k]) for k in range(n)]
        for cp in mine:
            cp.start()
        cps = []
        for j, (cx, cy) in enumerate([(1 - x, y), (x, 1 - y), (1 - x, 1 - y)]):
            for k in range(n):
                cps.append(pltpu.make_async_remote_copy(
                    src_ref=ins[k].at[2 * cx + cy], dst_ref=outs[k].at[mychip], send_sem=send.at[j, k],
                    recv_sem=recv.at[j, k], device_id=(cx, cy, c), device_id_type=MESH))
        for cp in cps:
            cp.start()
        for cp in cps:
            cp.wait()
        for cp in mine:
            cp.wait()

    return _pcall(body, name=name, in_specs=[_ANY] * n, out_specs=[_ANY] * n,
                  out_shape=[jax.ShapeDtypeStruct(a.shape, a.dtype) for a in arrs],
                  scratch_shapes=[pltpu.SemaphoreType.DMA((3, n)), pltpu.SemaphoreType.DMA((3, n)),
                                  pltpu.SemaphoreType.DMA((n,))])(*arrs)


_HBM = pl.BlockSpec(memory_space=pltpu.HBM)
_SEM = pl.BlockSpec(memory_space=pltpu.SEMAPHORE)
_SIDE_EFFECT = pltpu.SideEffectType.DATAFLOW_SIDE_EFFECTING


def _split_copies(plan, refs, ns, nd):
    send, recv = refs[ns + nd], refs[ns + nd + 1]
    return [pltpu.make_async_remote_copy(src_ref=src, dst_ref=dst, send_sem=send.at[i], recv_sem=recv.at[i],
                                         device_id=dev, device_id_type=MESH)
            for i, (src, dst, dev) in enumerate(plan(refs[:ns], refs[ns:ns + nd]))]


def _split_start(plan, ncopies, srcs, dsts, name):
    ns, nd = len(srcs), len(dsts)

    def body(*refs):
        copies = _split_copies(plan, refs, ns, nd)
        assert len(copies) == ncopies
        for cp in copies:
            cp.start()
        refs[-1][...] = jnp.zeros_like(refs[-1])

    bufs = [pltpu.with_memory_space_constraint(a, pltpu.HBM) for a in list(srcs) + list(dsts)]
    res = _pcall(
        body, name=name, in_specs=[_HBM] * (ns + nd),
        out_specs=[_SEM, _SEM] + [_HBM] * (ns + nd) + [pl.BlockSpec(memory_space=pltpu.VMEM)],
        out_shape=[pltpu.SemaphoreType.DMA((ncopies,)), pltpu.SemaphoreType.DMA((ncopies,))]
        + [pltpu.HBM(a.shape, a.dtype) for a in bufs] + [jax.ShapeDtypeStruct((8, LANE), F32)],
        input_output_aliases={i: 2 + i for i in range(ns + nd)},
        compiler_params=pltpu.CompilerParams(has_side_effects=_SIDE_EFFECT))(*bufs)
    return res[0], res[1], list(res[2:2 + ns]), list(res[2 + ns:2 + ns + nd]), res[-1]


def _split_wait(plan, send, recv, srcs, dsts, after, name):
    ns, nd = len(srcs), len(dsts)

    def body(*refs):
        for cp in _split_copies(plan, refs, ns, nd):
            cp.wait_send()
            cp.wait_recv()

    res = _pcall(
        body, name=name, in_specs=[_HBM] * (ns + nd) + [_SEM, _SEM, _ANY], out_specs=[_HBM] * (ns + nd),
        out_shape=[pltpu.HBM(a.shape, a.dtype) for a in list(srcs) + list(dsts)],
        input_output_aliases={i: i for i in range(ns + nd)},
        compiler_params=pltpu.CompilerParams(has_side_effects=_SIDE_EFFECT))(*srcs, *dsts, send, recv, after)
    return list(res[:ns]), list(res[ns:])


def _sibling_plan(kinds):
    def plan(srcs, dsts):
        x, y, c = _place()
        return [(_dest_slice(srcs[k], kinds[k], 2 * p + 1 - c), dsts[k].at[p], (x, y, 1 - c))
                for k in range(len(kinds)) for p in range(4)]
    return plan


def _chips_plan(n):
    def plan(srcs, dsts):
        x, y, c = _place()
        return [(srcs[k].at[2 * cx + cy], dsts[k].at[j], (cx, cy, c))
                for j, (cx, cy) in enumerate([(1 - x, y), (x, 1 - y), (1 - x, 1 - y)]) for k in range(n)]
    return plan


def _gather_plan(kinds, layer):
    def plan(srcs, dsts):
        x, y, c = _place()
        me = 4 * x + 2 * y + c
        return [(srcs[k].at[layer], _dest_slice(dsts[k], kinds[k], me), dev)
                for k in range(len(kinds)) for dev in [(x, y, 1 - c), (1 - x, y, c), (x, 1 - y, c), (1 - x, 1 - y, c)]]
    return plan


def _pass_on_plan(kinds):
    def plan(srcs, dsts):
        x, y, c = _place()
        parts = [_dest_slice(dsts[k], kinds[k], 4 * cx + 2 * cy + c)
                 for k in range(len(kinds)) for cx, cy in [(1 - x, y), (x, 1 - y), (1 - x, 1 - y)]]
        return [(part, part, (x, y, 1 - c)) for part in parts]
    return plan


def _place_own(shards, bufs, kinds, layer, name):
    n = len(shards)

    def body(*refs):
        x, y, c = _place()
        sem = refs[-1]
        cps = [pltpu.make_async_copy(refs[k].at[layer], _dest_slice(refs[n + k], kinds[k], 4 * x + 2 * y + c), sem.at[k])
               for k in range(n)]
        for cp in cps:
            cp.start()
        for cp in cps:
            cp.wait()

    return _pcall(body, name=name, in_specs=[_ANY] * (2 * n), out_specs=[_ANY] * n,
                  out_shape=[jax.ShapeDtypeStruct(b.shape, b.dtype) for b in bufs],
                  input_output_aliases={n + k: k for k in range(n)},
                  scratch_shapes=[pltpu.SemaphoreType.DMA((n,))])(*shards, *bufs)


def _slice_shape(shape, kind):
    if kind == "raw":
        return shape[1:]
    if kind == "col":
        return shape[:-1] + (shape[-1] // N_DEV,)
    return shape[:-2] + (shape[-2] // N_DEV, shape[-1])


def _all_to_all(arrs, src_kinds, dst_kinds, name):
    n = len(arrs)

    def body(*refs):
        ins, outs = refs[:n], refs[n:2 * n]
        send, recv, loc = refs[2 * n:]
        x, y, c = _place()
        me = 4 * x + 2 * y + c
        mine = [pltpu.make_async_copy(_dest_slice(ins[k], src_kinds[k], me), _dest_slice(outs[k], dst_kinds[k], me),
                                      loc.at[k]) for k in range(n)]
        for cp in mine:
            cp.start()
        cps = []
        for r in range(1, N_DEV):
            px, py, pc = (1 - x if r & 4 else x), (1 - y if r & 2 else y), (1 - c if r & 1 else c)
            for k in range(n):
                cps.append(pltpu.make_async_remote_copy(
                    src_ref=_dest_slice(ins[k], src_kinds[k], 4 * px + 2 * py + pc),
                    dst_ref=_dest_slice(outs[k], dst_kinds[k], me), send_sem=send.at[r - 1, k],
                    recv_sem=recv.at[r - 1, k], device_id=(px, py, pc), device_id_type=MESH))
        for cp in cps:
            cp.start()
        for cp in cps:
            cp.wait()
        for cp in mine:
            cp.wait()

    out_shape = [jax.ShapeDtypeStruct(_unsharded_shape(_slice_shape(a.shape, sk), dk), a.dtype)
                 for a, sk, dk in zip(arrs, src_kinds, dst_kinds)]
    return _pcall(body, name=name, in_specs=[_ANY] * n, out_specs=[_ANY] * n, out_shape=out_shape,
                  scratch_shapes=[pltpu.SemaphoreType.DMA((N_DEV - 1, n)), pltpu.SemaphoreType.DMA((N_DEV - 1, n)),
                                  pltpu.SemaphoreType.DMA((n,))])(*arrs)


def _rows_tile(rows, cols=0):
    cap = 512 * 1024
    return _pick(rows, [t for t in (128, 64, 32, 16) if t * cols <= cap])


def _pair_sum(g, r1, core, kind, name):
    _, groups, rows, cols = r1.shape
    tr = _rows_tile(rows, cols)
    nr = rows // tr
    if kind == "raw":
        g_spec = pl.BlockSpec((None, None, None, tr, cols), lambda p, q, i, c: (c[0], p, q, i, 0))
    elif kind == "col":
        g_spec = pl.BlockSpec((None, tr, cols), lambda p, q, i, c: (q, i, 2 * p + c[0]))
    else:
        g_spec = pl.BlockSpec((None, tr, cols), lambda p, q, i, c: (q, (2 * p + c[0]) * nr + i, 0))
    r_spec = pl.BlockSpec((None, None, tr, cols), lambda p, q, i, c: (p, q, i, 0))

    def body(c_ref, g_ref, r_ref, o_ref):
        o_ref[...] = (g_ref[...].astype(F32) + r_ref[...].astype(F32)).astype(o_ref.dtype)

    return _pcall(
        body, name=name,
        grid_spec=pltpu.PrefetchScalarGridSpec(num_scalar_prefetch=1, grid=(4, groups, nr), in_specs=[g_spec, r_spec],
                                               out_specs=r_spec),
        out_shape=jax.ShapeDtypeStruct(r1.shape, r1.dtype),
        compiler_params=_params(("parallel", "parallel", "parallel")))(core, g, r1)


def _adamw(g, w, m, v):
    m = ADAM_B1 * m + (1.0 - ADAM_B1) * g
    v = ADAM_B2 * v + (1.0 - ADAM_B2) * jnp.square(g)
    m_hat = m / (1.0 - ADAM_B1 ** ADAM_STEP)
    v_hat = v / (1.0 - ADAM_B2 ** ADAM_STEP)
    return -ADAM_LR * (m_hat / (jnp.sqrt(v_hat) + ADAM_EPS) + ADAM_WD * w), m, v


def _sum_adamw(parts, w, m, v, name):
    npart, groups, rows, cols = parts.shape
    tr = _rows_tile(rows, cols)

    def body(p_ref, w_ref, m_ref, v_ref, g_ref, d_ref, nm_ref, nv_ref):
        g = p_ref[0].astype(F32)
        for k in range(1, npart):
            g = g + p_ref[k].astype(F32)
        g_ref[...] = g
        d_ref[...], nm_ref[...], nv_ref[...] = _adamw(g, w_ref[...], m_ref[...], v_ref[...])

    blk = pl.BlockSpec((None, tr, cols), lambda q, i: (q, i, 0))
    return _pcall(body, name=name, grid=(groups, rows // tr),
                  in_specs=[pl.BlockSpec((npart, None, tr, cols), lambda q, i: (0, q, i, 0)), blk, blk, blk],
                  out_specs=[blk] * 4, out_shape=[jax.ShapeDtypeStruct((groups, rows, cols), F32)] * 4,
                  compiler_params=_params(("parallel", "parallel")))(parts, w, m, v)


def _layer_sum_specs(own, layer):
    _, groups, rows, cols = own.shape
    tr = _rows_tile(rows, cols)
    own_spec = pl.BlockSpec((None, None, tr, cols), lambda q, i, ch: (ch[0], q, i, 0))
    theirs_spec = pl.BlockSpec((3, None, tr, cols), lambda q, i, ch: (0, q, i, 0))
    stacked = pl.BlockSpec((None, tr, cols), lambda q, i, ch: (layer * groups + q, i, 0))
    return (groups, rows // tr), own_spec, theirs_spec, stacked


def _layer_sum(own_ref, theirs_ref):
    g = own_ref[...].astype(F32)
    for k in range(3):
        g = g + theirs_ref[k].astype(F32)
    return g


def _sum_parts_layer(own, theirs, chip, buf, layer, name):
    grid, own_spec, theirs_spec, stacked = _layer_sum_specs(own, layer)

    def body(ch_ref, own_ref, theirs_ref, buf_ref, o_ref):
        o_ref[...] = _layer_sum(own_ref, theirs_ref)

    return _pcall(
        body, name=name,
        grid_spec=pltpu.PrefetchScalarGridSpec(num_scalar_prefetch=1, grid=grid, in_specs=[own_spec, theirs_spec, _ANY],
                                               out_specs=stacked),
        out_shape=jax.ShapeDtypeStruct(buf.shape, buf.dtype), input_output_aliases={3: 0},
        compiler_params=_params(("parallel", "parallel")))(chip, own, theirs, buf)


def _sum_adamw_layer(own, theirs, chip, w, m, v, outs, layer, deps, name):
    grid, own_spec, theirs_spec, stacked = _layer_sum_specs(own, layer)

    def body(ch_ref, own_ref, theirs_ref, w_ref, m_ref, v_ref, *rest):
        g_ref, d_ref, nm_ref, nv_ref = rest[-4:]
        g = _layer_sum(own_ref, theirs_ref)
        g_ref[...] = g
        d_ref[...], nm_ref[...], nv_ref[...] = _adamw(g, w_ref[...], m_ref[...], v_ref[...])

    return _pcall(
        body, name=name,
        grid_spec=pltpu.PrefetchScalarGridSpec(
            num_scalar_prefetch=1, grid=grid,
            in_specs=[own_spec, theirs_spec, stacked, stacked, stacked] + [_ANY] * (4 + len(deps)), out_specs=[stacked] * 4),
        out_shape=[jax.ShapeDtypeStruct(o.shape, o.dtype) for o in outs],
        input_output_aliases={6 + i: i for i in range(4)},
        compiler_params=_params(("parallel", "parallel")))(chip, own, theirs, w, m, v, *outs, *deps)


def _rot_cols(w):
    return jnp.concatenate([-w[..., ROPE // 2:], w[..., :ROPE // 2]], axis=-1)


def _rot_cols_t(dw):
    return jnp.concatenate([dw[..., ROPE // 2:], -dw[..., :ROPE // 2]], axis=-1)


_IN_SPLITS = np.cumsum([0, Q_RANK, KV_RANK, ROPE, BW, BW, BW, BW, BW, BW, HEADS, 3 * D_MODEL])


def _ext_w_in(w):
    o = _IN_SPLITS
    kpe = w[..., o[2]:o[3]]
    z = lambda n: jnp.zeros(w.shape[:-1] + (n,), w.dtype)
    return jnp.concatenate([w[..., o[10]:o[11]], w[..., o[3]:o[9]], w[..., o[0]:o[2]], kpe, z(LANE - ROPE),
                            _rot_cols(kpe), z(LANE - ROPE), w[..., o[9]:o[10]], z(2 * LANE - HEADS)], axis=-1)


def _unext_w_in(dw):
    kpe = dw[..., OFF_KPE:OFF_KPE + ROPE] + _rot_cols_t(dw[..., OFF_KROT:OFF_KROT + ROPE])
    return jnp.concatenate([dw[..., OFF_CQ:OFF_KPE], kpe, dw[..., OFF_CONV:OFF_CQ], dw[..., OFF_FL:OFF_FL + HEADS],
                            dw[..., OFF_GATE:OFF_CONV]], axis=-1)


def _ext_w_uq(w):
    w3 = w.reshape(w.shape[:-1] + (HEADS, HD + ROPE))
    pe = w3[..., HD:]
    z = jnp.zeros(w3.shape[:-1] + (LANE - ROPE,), w.dtype)
    return jnp.concatenate([w3[..., :HD], pe, z, _rot_cols(pe), z], axis=-1).reshape(w.shape[:-1] + (HEADS * QG,))


def _unext_w_uq(dw):
    d3 = dw.reshape(dw.shape[:-1] + (HEADS, QG))
    pe = d3[..., HD:HD + ROPE] + _rot_cols_t(d3[..., 2 * HD:2 * HD + ROPE])
    return jnp.concatenate([d3[..., :HD], pe], axis=-1).reshape(dw.shape[:-1] + (HEADS * (HD + ROPE),))


_BIG = (("meta", "col", F32), ("w_in", "row", BF16), ("w_uq", "row", BF16), ("w_ukv", "col", BF16),
        ("conv_w", "col", F32), ("w_branch", "col", BF16), ("w_out", "row", BF16), ("w_ffn_in", "col", BF16),
        ("w_ffn_out", "row", BF16))
_SMALL = ("b_forget", "g_q_lat", "g_kv_lat", "g_mix_pre", "g_mix_post", "g_ffn_pre", "g_ffn_post")
_ORDER = ("meta", "w_in", "b_forget", "g_q_lat", "g_kv_lat", "w_uq", "w_ukv", "conv_w", "w_branch", "w_out",
          "w_ffn_in", "w_ffn_out", "g_mix_pre", "g_mix_post", "g_ffn_pre", "g_ffn_post")


def _unshard_cols(g):
    g = jnp.moveaxis(g, 0, -2)
    return g.reshape(g.shape[:-2] + (g.shape[-2] * g.shape[-1],))


def _as3d(a, lead=0):
    return a.reshape(a.shape[:lead] + (-1,) + a.shape[-2:])


def kernel(x, meta, w_in, b_forget, g_q_lat, g_kv_lat, w_uq, w_ukv, conv_w, w_branch, w_out, w_ffn_in, w_ffn_out, g_mix_pre, g_mix_post, g_ffn_pre, g_ffn_post, loss_target, m_meta, m_w_in, m_b_forget, m_g_q_lat, m_g_kv_lat, m_w_uq, m_w_ukv, m_conv_w, m_w_branch, m_w_out, m_w_ffn_in, m_w_ffn_out, m_g_mix_pre, m_g_mix_post, m_g_ffn_pre, m_g_ffn_post, v_meta, v_w_in, v_b_forget, v_g_q_lat, v_g_kv_lat, v_w_uq, v_w_ukv, v_conv_w, v_w_branch, v_w_out, v_w_ffn_in, v_w_ffn_out, v_g_mix_pre, v_g_mix_post, v_g_ffn_pre, v_g_ffn_post):
    given = dict(locals())
    core = lax.axis_index("c").astype(jnp.int32).reshape(1)

    rows_in, rows_uq = _all_to_all([w_in.astype(BF16), w_uq.astype(BF16)], ("row", "row"), ("raw", "raw"),
                                   "rows_of_column_shards")
    shards = {n: given[n].astype(dt) for n, _, dt in _BIG}
    shards["w_in"] = _ext_w_in(_unshard_cols(rows_in))
    shards["w_uq"] = _ext_w_uq(_unshard_cols(rows_uq))
    (meta_full,) = _all_gather([shards["meta"]], ("col",), "gather_meta")
    per_layer = tuple((n, kd) for n, kd, _ in _BIG if n != "meta")
    layer_kinds = tuple(kd for _, kd in per_layer)
    pass_plan = _pass_on_plan(layer_kinds)

    def start_gather(layer, stacked):
        bufs = [lax.empty(_unsharded_shape(a.shape[1:], kd), a.dtype) for a, kd in zip(stacked, layer_kinds)]
        bufs = _place_own(stacked, bufs, layer_kinds, layer, "place_own_shards")
        send, recv, stacked, bufs, token = _split_start(_gather_plan(layer_kinds, layer), 4 * len(per_layer), stacked, bufs,
                                                        "gather_start_%d" % layer)
        return dict(send=send, recv=recv, srcs=stacked, dsts=bufs, token=token)

    def finish_gather(layer, fly, after):
        stacked, bufs = _split_wait(_gather_plan(layer_kinds, layer), fly["send"], fly["recv"], fly["srcs"], fly["dsts"],
                                    after, "gather_wait_%d" % layer)
        send, recv, _, bufs, token = _split_start(pass_plan, 3 * len(per_layer), [], bufs, "pass_on_start_%d" % layer)
        _, bufs = _split_wait(pass_plan, send, recv, [], bufs, token, "pass_on_wait_%d" % layer)
        return stacked, {n: b for (n, _), b in zip(per_layer, bufs)}

    gathering = start_gather(0, [shards[n] for n, _ in per_layer])

    pos = jnp.arange(LP, dtype=F32)[:, None]
    inv_freq = 1.0 / (ROPE_THETA ** (jnp.arange(0, ROPE, 2, dtype=F32) / ROPE))
    ang = pos * inv_freq[None, :]
    zpad = jnp.zeros((LP, LANE - ROPE), F32)
    cosp = jnp.concatenate([jnp.cos(ang), jnp.cos(ang), zpad], axis=1)
    sinp = jnp.concatenate([jnp.sin(ang), jnp.sin(ang), zpad], axis=1)

    tail = jnp.zeros((LP - L_TOK, D_MODEL), F32)
    h = jnp.concatenate([meta_full, x[0], tail], axis=0)
    ltp = jnp.concatenate([jnp.zeros((N_META, D_MODEL), F32), loss_target[0], tail], axis=0)
    row = jnp.arange(LP)[:, None]
    rmask = ((row >= N_META) & (row < L_TOK)).astype(F32)

    def vec(a, l):
        return a[l][None, :]

    wl = []
    for l in range(DEPTH):
        wl.append(dict(
            b_pad=jnp.concatenate([b_forget[l], jnp.zeros((LANE - HEADS,), F32)])[None, :],
            gq=vec(g_q_lat, l), gkv=vec(g_kv_lat, l), g1=vec(g_mix_pre, l), g2=vec(g_mix_post, l),
            g3=vec(g_ffn_pre, l), g4=vec(g_ffn_post, l)))

    def prep_ins(p_all, w):
        return [R(p_all, Q_RANK, _cb(OFF_CQ, Q_RANK)), R(p_all, KV_RANK, _cb(OFF_CKV, KV_RANK)),
                R(p_all, LANE, _cb(OFF_KPE, LANE)), R(p_all, LANE, _cb(OFF_KROT, LANE)), Pm(w["gq"]), Pm(w["gkv"]),
                R(cosp, LANE), R(sinp, LANE)]

    def merge_ins(p_all, ys):
        return [R(p_all, GW, _cb(OFF_GATE + n * D_MODEL, GW), 1) for n in range(3)] + [R(yv, GW, 0, 1) for yv in ys]

    def b16(total, w, cstep=0):
        return OR(total, w, cstep, BF16)

    (hn,) = _rw(lambda a, g: (_rms(a, g),), [R(h, D_MODEL), Pm(wl[0]["g1"])], [b16(D_MODEL, D_MODEL)], name="rms_in")
    saved, full = [], []
    for l in range(DEPTH):
        w = wl[l]
        s = dict(h=h, hn=hn)
        stacked, wts = finish_gather(l, gathering, h if l == 0 else saved[-1]["f"])
        full.append(wts)
        if l + 1 < DEPTH:
            gathering = start_gather(l + 1, stacked)
        p_all = _mm(hn, wts["w_in"], deps=(gathering["token"],) if l + 1 < DEPTH else (), name="proj_in")
        cqn, ckvn, kper = _rw(_mla_prep, prep_ins(p_all, w),
                              [b16(Q_RANK, Q_RANK), b16(KV_RANK, KV_RANK), OR(LANE, LANE)], name="mla_prep")
        q = _mm(cqn, wts["w_uq"], name="proj_q")
        kv = _mm(ckvn, wts["w_ukv"], name="proj_kv")
        o_a = _mla_fwd(q, kv, kper, cosp, sinp)
        o_b = _conv_fwd(p_all, wts["conv_w"])
        cdec = _decay_fwd(p_all, w["b_pad"])
        cq3 = cdec[:, :HEADS].T[:, :, None]
        ck3 = cdec[:, :HEADS].T[:, None, :]
        o_c = _fox_fwd(p_all, cq3, ck3)
        outs = (o_a, o_b, o_c)
        ys = [_mm(outs[n], wts["w_branch"], bidx=(n,), name="proj_branch") for n in range(3)]
        (merged,) = _rw(_merge, merge_ins(p_all, ys), [b16(D_MODEL, GW, 1)], ncol=D_MODEL // GW, name="merge")
        mix = _mm(merged, wts["w_out"], name="proj_out")
        h2, hn2 = _rw(_resid_norm, [R(h, D_MODEL), R(mix, D_MODEL), Pm(w["g2"]), Pm(w["g3"])],
                      [OR(D_MODEL, D_MODEL), b16(D_MODEL, D_MODEL)], name="resid_norm")
        gu = _mm(hn2, wts["w_ffn_in"], name="ffn_in")
        (act,) = _rw(_swiglu, [R(gu, 2 * D_FF)], [b16(D_FF, D_FF)], tm=TM_FF, name="swiglu")
        f = _mm(act, wts["w_ffn_out"], name="ffn_out")
        s.update(p_all=p_all, cqn=cqn, ckvn=ckvn, kper=kper, q=q, kv=kv, outs=outs, cq3=cq3, ck3=ck3, ys=ys,
                 merged=merged, mix=mix, h2=h2, hn2=hn2, gu=gu, act=act, f=f)
        saved.append(s)
        if l + 1 < DEPTH:
            h, hn = _rw(_resid_norm, [R(h2, D_MODEL), R(f, D_MODEL), Pm(w["g4"]), Pm(wl[l + 1]["g1"])],
                        [OR(D_MODEL, D_MODEL), b16(D_MODEL, D_MODEL)], name="resid_norm")

    grads = {n: [None] * DEPTH for n in _SMALL + ("conv_w",)}
    mats = tuple(n for n, _, _ in _BIG if n not in ("meta", "conv_w"))
    mat_kinds = tuple(kd for n, kd, _ in _BIG if n in mats)
    via_rows = ("w_in", "w_uq")
    chip = (2 * lax.axis_index("x") + lax.axis_index("y")).astype(jnp.int32).reshape(1)
    sib_plan, chips_plan = _sibling_plan(mat_kinds), _chips_plan(len(mats))
    updates = {n: [lax.empty(_as3d(given[n]).shape, F32) for _ in range(4)] for n in mats if n not in via_rows}
    row_sums = {n: lax.empty((DEPTH, full[0][n].shape[0] // N_DEV, full[0][n].shape[1]), F32) for n in via_rows}
    landed = []
    flying = None

    def finish(layer, own, theirs, deps):
        for n, o, t in zip(mats, own, theirs):
            if n in via_rows:
                row_sums[n] = _sum_parts_layer(o, t, chip, row_sums[n], layer, "sum_" + n)
            else:
                updates[n] = _sum_adamw_layer(o, t, chip, _as3d(given[n]), _as3d(given["m_" + n]),
                                              _as3d(given["v_" + n]), updates[n], layer, deps, "adamw_" + n)

    def pair_sums(layer, srcs, from_sib):
        sums = [_pair_sum(_as3d(g), _as3d(r, 1), core, kd, "pair_sum_" + n)
                for n, kd, g, r in zip(mats, mat_kinds, srcs, from_sib)]
        return _split_start(chips_plan, 3 * len(mats), sums, [lax.empty((3,) + p.shape[1:], BF16) for p in sums],
                            "scatter_chips_start_%d" % layer)

    s, w = saved[-1], wl[-1]
    dh2, df, dg4, loss_acc = _rw(
        _loss_bwd, [R(s["h2"], D_MODEL), R(s["f"], D_MODEL), Pm(w["g4"]), R(ltp, D_MODEL), R(rmask, 1)],
        [OR(D_MODEL, D_MODEL), b16(D_MODEL, D_MODEL), OA((1, D_MODEL)), OA((1, LANE))], name="loss_bwd")
    loss = lax.psum(loss_acc[0, 0], ("x", "y", "c"))
    grads["g_ffn_post"][DEPTH - 1] = dg4[0]
    for l in reversed(range(DEPTH)):
        s, w = saved[l], wl[l]
        p_all = s["p_all"]
        gl = {}
        wts = full[l]
        dact = _mm(df, wts["w_ffn_out"], tb=True, deps=(flying["token"],) if flying else (), name="d_act")
        gl["w_ffn_out"] = _mm(s["act"], df, ta=True, out_dtype=BF16, name="dw_ffn_out")
        (dgu,) = _rw(_swiglu_bwd, [R(s["gu"], 2 * D_FF), R(dact, D_FF)], [b16(2 * D_FF, 2 * D_FF)], tm=TM_FF,
                     name="swiglu_bwd")
        if flying:
            srcs, from_sib = _split_wait(sib_plan, flying["send"], flying["recv"], flying["srcs"], flying["dsts"], dgu,
                                         "scatter_sibling_wait_%d" % (l + 1))
            send, recv, sums, slots, token = pair_sums(l + 1, srcs, from_sib)
            flying = dict(send=send, recv=recv, srcs=sums, dsts=slots, token=token)
        dhn2 = _mm(dgu, wts["w_ffn_in"], tb=True, deps=(flying["token"],) if flying else (), name="d_hn2")
        gl["w_ffn_in"] = _mm(s["hn2"], dgu, ta=True, out_dtype=BF16, name="dw_ffn_in")
        dh, dmix, dg2, dg3 = _rw(
            _resid_norm_bwd, [R(s["h"], D_MODEL), R(s["mix"], D_MODEL), Pm(w["g2"]), Pm(w["g3"]), R(dh2, D_MODEL),
                              R(dhn2, D_MODEL)],
            [OR(D_MODEL, D_MODEL), b16(D_MODEL, D_MODEL), OA((1, D_MODEL)), OA((1, D_MODEL))], name="resid_norm_bwd")
        grads["g_mix_post"][l], grads["g_ffn_pre"][l] = dg2[0], dg3[0]
        dmerged = _mm(dmix, wts["w_out"], tb=True, name="d_merged")
        gl["w_out"] = _mm(s["merged"], dmix, ta=True, out_dtype=BF16, name="dw_out")
        mb = _rw(_merge_bwd, merge_ins(p_all, s["ys"]) + [R(dmerged, GW, 0, 1)], [b16(D_MODEL, GW, 1)] * 6,
                 ncol=D_MODEL // GW, name="merge_bwd")
        dgate, dys = mb[:3], mb[3:]
        dos = [_mm(dys[n], wts["w_branch"], tb=True, bidx=(n,), name="d_branch") for n in range(3)]
        gl["w_branch"] = lax.empty(wts["w_branch"].shape, BF16)
        for n in range(3):
            gl["w_branch"] = _mm(s["outs"][n], dys[n], ta=True, stack=(gl["w_branch"], (n,)), name="dw_branch")
        dfq, dfk, dfv, dcq3, dck3 = _fox_bwd(p_all, s["cq3"], s["ck3"], dos[2])
        dc = jnp.concatenate([dcq3[:, :, 0].T + dck3[:, 0, :].T, jnp.zeros((LP, LANE - HEADS), F32)], axis=1)
        dfl, db = _decay_bwd(p_all, w["b_pad"], dc)
        grads["b_forget"][l] = db[0, :HEADS]
        dconv, dcw = _conv_bwd(p_all, wts["conv_w"], dos[1])
        grads["conv_w"][l] = dcw
        dq, dkv, dkper = _mla_bwd(s["q"], s["kv"], s["kper"], cosp, sinp, dos[0])
        dcqn = _mm(dq, wts["w_uq"], tb=True, name="d_cqn")
        gl["w_uq"] = _mm(s["cqn"], dq, ta=True, out_dtype=BF16, name="dw_uq")
        dckvn = _mm(dkv, wts["w_ukv"], tb=True, name="d_ckvn")
        gl["w_ukv"] = _mm(s["ckvn"], dkv, ta=True, out_dtype=BF16, name="dw_ukv")
        dcq, dckv, dkpe, dkrot, dgq, dgkv = _rw(
            _mla_prep_bwd, prep_ins(p_all, w) + [R(dcqn, Q_RANK), R(dckvn, KV_RANK), R(dkper, LANE)],
            [b16(Q_RANK, Q_RANK), b16(KV_RANK, KV_RANK), b16(LANE, LANE), b16(LANE, LANE), OA((1, Q_RANK)),
             OA((1, KV_RANK))], name="mla_prep_bwd")
        grads["g_q_lat"][l], grads["g_kv_lat"][l] = dgq[0], dgkv[0]
        dp = jnp.concatenate([*dgate, dconv[0], dconv[1], dconv[2], dfq, dfk.astype(BF16), dfv.astype(BF16), dcq, dckv,
                              dkpe, dkrot, dfl, jnp.zeros((LP, LANE), BF16)], axis=1)
        dhn = _mm(dp, wts["w_in"], tb=True, name="d_hn")
        gl["w_in"] = _mm(s["hn"], dp, ta=True, out_dtype=BF16, name="dw_in")
        if flying:
            sums, slots = _split_wait(chips_plan, flying["send"], flying["recv"], flying["srcs"], flying["dsts"], dhn,
                                      "scatter_chips_wait_%d" % (l + 1))
            landed.append((l + 1, sums, slots))
        parts_l = [gl[n] for n in mats]
        send, recv, srcs, dsts, token = _split_start(
            sib_plan, 4 * len(mats), parts_l,
            [lax.empty((4,) + _slice_shape(g.shape, kd), BF16) for g, kd in zip(parts_l, mat_kinds)],
            "scatter_sibling_start_%d" % l)
        flying = dict(send=send, recv=recv, srcs=srcs, dsts=dsts, token=token)
        if l > 0:
            sp, wp = saved[l - 1], wl[l - 1]
            dh2, df, dg4, dg1 = _rw(
                _resid_norm_bwd, [R(sp["h2"], D_MODEL), R(sp["f"], D_MODEL), Pm(wp["g4"]), Pm(w["g1"]),
                                  R(dh, D_MODEL), R(dhn, D_MODEL)],
                [OR(D_MODEL, D_MODEL), b16(D_MODEL, D_MODEL), OA((1, D_MODEL)), OA((1, D_MODEL))], name="resid_norm_bwd")
            grads["g_ffn_post"][l - 1], grads["g_mix_pre"][l] = dg4[0], dg1[0]
        else:
            dh0, dg1 = _rw(_rms_bwd, [R(s["h"], D_MODEL), Pm(w["g1"]), R(dhn, D_MODEL), R(dh, D_MODEL)],
                           [OR(D_MODEL, D_MODEL), OA((1, D_MODEL))], name="rms_in_bwd")
            grads["g_mix_pre"][0] = dg1[0]
    grad_x = dh0[N_META:L_TOK][None]
    gfull = {n: jnp.stack(grads[n]) for n in grads}
    gfull["meta"] = dh0[:N_META]

    srcs, from_sib = _split_wait(sib_plan, flying["send"], flying["recv"], flying["srcs"], flying["dsts"], dh0,
                                 "scatter_sibling_wait_0")
    send, recv, sums, slots, token = pair_sums(0, srcs, from_sib)
    for layer, own, theirs in landed:
        finish(layer, own, theirs, (token,))
    sums, slots = _split_wait(chips_plan, send, recv, sums, slots, updates[mats[-1]][0], "scatter_chips_wait_0")
    finish(0, sums, slots, ())

    few = ("meta", "conv_w")
    partial = [gfull[n] for n in few]
    from_sib = _swap_sibling(partial, ("col", "col"), "scatter_sibling_few")
    chip_sums = [_pair_sum(_as3d(g), _as3d(r, 1), core, "col", "pair_sum_" + n)
                 for n, g, r in zip(few, partial, from_sib)]
    parts = dict(zip(few, _swap_chips(chip_sums, "scatter_chips_few")))

    def by_dest(a):
        return jnp.moveaxis(a.reshape(a.shape[:-1] + (N_DEV, a.shape[-1] // N_DEV)), -2, 0).astype(BF16)

    cols_in, cols_uq = _all_to_all([by_dest(_unext_w_in(row_sums["w_in"])), by_dest(_unext_w_uq(row_sums["w_uq"]))],
                                   ("raw", "raw"), ("row", "row"), "columns_of_row_sums")
    parts["w_in"], parts["w_uq"] = _as3d(cols_in)[None], _as3d(cols_uq)[None]
    out = {n: [r.reshape(given[n].shape) for r in updates[n]] for n in updates}
    for n in parts:
        res = _sum_adamw(parts[n], _as3d(given[n]), _as3d(given["m_" + n]), _as3d(given["v_" + n]), "adamw_" + n)
        out[n] = [r.reshape(given[n].shape) for r in res]

    def pack(d):
        flat = jnp.concatenate([d[n].reshape(-1) for n in _SMALL])
        return jnp.concatenate([flat, jnp.zeros((-flat.shape[0]) % (8 * LANE), F32)]).reshape(-1, LANE)

    (small_parts,) = _all_gather([pack(gfull)], ("raw",), "gather_small_grads")
    res = _sum_adamw(small_parts[:, None], pack(given)[None], pack({n: given["m_" + n] for n in _SMALL})[None],
                     pack({n: given["v_" + n] for n in _SMALL})[None], "adamw_small")
    off = 0
    for n in _SMALL:
        size = int(np.prod(given[n].shape))
        out[n] = [r.reshape(-1)[off:off + size].reshape(given[n].shape) for r in res]
        off += size

    return (loss, grad_x, *[out[n][0] for n in _ORDER], *[out[n][1] for n in _ORDER], *[out[n][2] for n in _ORDER],
            *[out[n][3] for n in _ORDER])
```

```python
import functools

import numpy as np
import jax
import jax.numpy as jnp
from jax import lax
from jax.experimental import pallas as pl
from jax.experimental.pallas import tpu as pltpu

D_MODEL = 2048
SEQ = 2048
DEPTH = 4
Q_RANK = 512
KV_RANK = 512
D_FF = 5632
N_META = 16
HEADS = 8
HD = 128
ROPE = 64
BW = HEADS * HD
EPS = 1e-6
NEG_INF = -1e30
ROPE_THETA = 10000.0
N_DEV = 8
LANE = 128
L_TOK = N_META + SEQ
LP = -(-L_TOK // LANE) * LANE
D_IN = Q_RANK + KV_RANK + ROPE + 6 * BW + HEADS + 3 * D_MODEL
MLA_SCALE = (HD + ROPE) ** -0.5
FOX_SCALE = HD ** -0.5
ADAM_LR, ADAM_B1, ADAM_B2, ADAM_EPS, ADAM_WD, ADAM_STEP = 0.001, 0.9, 0.999, 1e-08, 0.01, 10
VMEM_LIMIT = 48 * 1024 * 1024

F32 = jnp.float32
BF16 = jnp.bfloat16
MESH = pl.DeviceIdType.MESH

OFF_GATE = 0
OFF_CONV = 3 * D_MODEL
OFF_FOX = OFF_CONV + 3 * BW
OFF_CQ = OFF_FOX + 3 * BW
OFF_CKV = OFF_CQ + Q_RANK
OFF_KPE = OFF_CKV + KV_RANK
OFF_KROT = OFF_KPE + LANE
OFF_FL = OFF_KROT + LANE
W_ALL = OFF_FL + 2 * LANE
GW = 512 if D_MODEL % 512 == 0 else 256
TM_FF = 64
CW = 128
QG = 3 * LANE


def _pick(n, prefs):
    for p in prefs:
        if n % p == 0:
            return p
    return n


TM_ROW = 128
TQ = _pick(LP, (272, 128))


def _cb(off, w):
    assert off % w == 0, (off, w)
    return off // w


def _pcall(body, **kw):
    return pl.pallas_call(body, **kw)


def _params(sem):
    return pltpu.CompilerParams(dimension_semantics=sem, vmem_limit_bytes=VMEM_LIMIT)


def _bf(x):
    return x.astype(BF16)


def _dot(a, b, dims, **kw):
    return lax.dot_general(_bf(a), _bf(b), (dims, ((), ())), preferred_element_type=F32, **kw)


_NN = ((1,), (0,))
_NT = ((1,), (1,))
_TN = ((0,), (0,))
_ANY = pl.BlockSpec(memory_space=pl.ANY)


MM_VMEM_BUDGET = 38 * 1024 * 1024
HBM_BYTES_PER_STEP = 1 << 20


def _divisors(n, prefs):
    return [p for p in prefs if n % p == 0] or [n]


def _mm_tiles(m, n, kd, a_bytes, b_bytes, o_bytes):
    best = None
    for tm in _divisors(m, (2176, 2048, 1088, 1024, 544, 512, 272, 256, 128)):
        for tn in _divisors(n, (2048, 1536, 1024, 768, 512, 384, 256, 128)):
            for tk in _divisors(kd, (2816, 2304, 2176, 2048, 1536, 1408, 1024, 768, 544, 512, 384, 272, 256, 128)):
                nk = kd // tk
                vmem = 2 * (tm * tk * a_bytes + tk * tn * b_bytes + tm * tn * o_bytes) + 2 * tm * tn * 4
                vmem += (tm * tk * 2 if a_bytes == 4 else 0) + (tk * tn * 2 if b_bytes == 4 else 0)
                if vmem > MM_VMEM_BUDGET:
                    continue
                steps = (m // tm) * (n // tn) * nk
                cost = (m * kd * a_bytes * (1 if nk == 1 else n // tn) + kd * n * b_bytes * (m // tm)
                        + steps * HBM_BYTES_PER_STEP)
                if best is None or cost < best[0]:
                    best = (cost, tm, tn, tk)
    assert best is not None, (m, n, kd)
    return best[1:]


def _mm(a, b, *, ta=False, tb=False, bidx=(), stack=None, out_dtype=F32, deps=(), name):
    if ta:
        kd, m = a.shape
    else:
        m, kd = a.shape
    nlead = len(bidx)
    if tb:
        n, kd2 = b.shape[nlead:]
    else:
        kd2, n = b.shape[nlead:]
    assert kd == kd2, (a.shape, b.shape, ta, tb)
    if stack is not None:
        out_dtype = stack[0].dtype
    tm, tn, tk = _mm_tiles(m, n, kd, a.dtype.itemsize, b.dtype.itemsize, jnp.dtype(out_dtype).itemsize)
    nk = kd // tk
    dims = _TN if ta else (_NT if tb else _NN)

    def body(a_ref, b_ref, *rest):
        o_ref, acc_ref = rest[-2:]
        if nk == 1:
            o_ref[...] = _dot(a_ref[...], b_ref[...], dims).astype(o_ref.dtype)
            return
        k = pl.program_id(2)

        @pl.when(k == 0)
        def _():
            acc_ref[...] = jnp.zeros_like(acc_ref)

        acc_ref[...] += _dot(a_ref[...], b_ref[...], dims)

        @pl.when(k == nk - 1)
        def _():
            o_ref[...] = acc_ref[...].astype(o_ref.dtype)

    lead = (None,) * nlead
    a_spec = pl.BlockSpec((tk, tm), lambda i, j, k: (k, i)) if ta else pl.BlockSpec((tm, tk), lambda i, j, k: (i, k))
    if tb:
        b_spec = pl.BlockSpec(lead + (tn, tk), lambda i, j, k: bidx + (j, k))
    else:
        b_spec = pl.BlockSpec(lead + (tk, tn), lambda i, j, k: bidx + (k, j))
    in_specs, args, extra = [a_spec, b_spec], [a, b], {}
    if stack is None:
        out_spec = pl.BlockSpec((tm, tn), lambda i, j, k: (i, j))
        out_shape = jax.ShapeDtypeStruct((m, n), out_dtype)
    else:
        buf, sidx = stack
        assert buf.shape[len(sidx):] == (m, n), (buf.shape, sidx, m, n)
        in_specs.append(_ANY)
        args.append(buf)
        extra = dict(input_output_aliases={2: 0})
        out_spec = pl.BlockSpec((None,) * len(sidx) + (tm, tn), lambda i, j, k: sidx + (i, j))
        out_shape = jax.ShapeDtypeStruct(buf.shape, buf.dtype)
    in_specs += [_ANY] * len(deps)
    args += list(deps)
    return _pcall(
        body, name=name, grid=(m // tm, n // tn, nk), in_specs=in_specs, out_specs=out_spec, out_shape=out_shape,
        scratch_shapes=[pltpu.VMEM((tm, tn) if nk > 1 else (8, LANE), F32)],
        compiler_params=_params(("parallel", "parallel", "arbitrary")), **extra)(*args)


class R:
    def __init__(self, arr, w, cb0=0, cstep=0):
        self.arr, self.w, self.cb0, self.cstep = arr, w, cb0, cstep


class Pm:
    def __init__(self, arr):
        self.arr = arr


class OR:
    def __init__(self, total, w, cstep=0, dtype=F32):
        self.total, self.w, self.cstep, self.dtype = total, w, cstep, dtype


class OA:
    def __init__(self, shape):
        self.shape = shape


def _rw(fn, ins, outs, *, name, ncol=1, tm=None):
    tm = TM_ROW if tm is None else tm
    nrow = LP // tm
    n_in = len(ins)

    def body(*refs):
        j, i = pl.program_id(0), pl.program_id(1)
        res = fn(*[r[...] for r in refs[:n_in]])
        for o, ref, val in zip(outs, refs[n_in:], res):
            if isinstance(o, OR):
                ref[...] = val.astype(ref.dtype)
            else:
                @pl.when((i == 0) & (j == 0))
                def _(ref=ref):
                    ref[...] = jnp.zeros_like(ref)

                ref[...] += val

    in_specs = []
    for s in ins:
        if isinstance(s, R):
            in_specs.append(pl.BlockSpec((tm, s.w), lambda j, i, s=s: (i, s.cb0 + j * s.cstep)))
        else:
            in_specs.append(pl.BlockSpec(s.arr.shape, lambda j, i, nd=s.arr.ndim: (0,) * nd))
    out_specs, out_shape = [], []
    for o in outs:
        if isinstance(o, OR):
            out_specs.append(pl.BlockSpec((tm, o.w), lambda j, i, o=o: (i, j * o.cstep)))
            out_shape.append(jax.ShapeDtypeStruct((LP, o.total), o.dtype))
        else:
            out_specs.append(pl.BlockSpec(o.shape, lambda j, i: (0, 0)))
            out_shape.append(jax.ShapeDtypeStruct(o.shape, F32))
    return _pcall(body, name=name, grid=(ncol, nrow), in_specs=in_specs, out_specs=out_specs, out_shape=out_shape,
                  compiler_params=_params(("arbitrary", "arbitrary")))(*[s.arr for s in ins])


def _rms(x, g):
    return x * lax.rsqrt(jnp.mean(x * x, axis=-1, keepdims=True) + EPS) * g


def _resid_norm(h, z, ga, gb):
    h2 = h + _rms(z, ga)
    return h2, _rms(h2, gb)


def _resid_norm_bwd(h, z, ga, gb, dh2, dhn2):
    _, vjp = jax.vjp(_resid_norm, h, z, ga, gb)
    return vjp((dh2, dhn2))


def _rms_bwd(h, g, dhn, dh_in):
    _, vjp = jax.vjp(_rms, h, g)
    dh, dg = vjp(dhn)
    return dh + dh_in, dg


def _loss_fn(h2, f, g4, lt, rmask):
    h3 = h2 + _rms(f, g4)
    err = jnp.square(h3 - lt)
    return 0.5 * jnp.sum(jnp.mean(err, axis=-1, keepdims=True) * rmask)


def _loss_bwd(h2, f, g4, lt, rmask):
    val, (dh2, df, dg4) = jax.value_and_grad(_loss_fn, argnums=(0, 1, 2))(h2, f, g4, lt, rmask)
    return dh2, df, dg4, jnp.broadcast_to(val, (1, LANE))


def _mla_prep(cq, ckv, kpe, krot, gq, gkv, cosp, sinp):
    return _rms(cq, gq), _rms(ckv, gkv), kpe * cosp + krot * sinp


def _mla_prep_bwd(cq, ckv, kpe, krot, gq, gkv, cosp, sinp, dcqn, dckvn, dkper):
    _, vjp = jax.vjp(lambda a, b, c, d, e, f: _mla_prep(a, b, c, d, e, f, cosp, sinp), cq, ckv, kpe, krot, gq, gkv)
    return vjp((dcqn, dckvn, dkper))


def _merge(g0, g1, g2, y0, y1, y2):
    return (jax.nn.sigmoid(g0) * y0 + jax.nn.sigmoid(g1) * y1 + jax.nn.sigmoid(g2) * y2,)


def _merge_bwd(g0, g1, g2, y0, y1, y2, dm):
    _, vjp = jax.vjp(_merge, g0, g1, g2, y0, y1, y2)
    return vjp((dm,))


def _swiglu(gu):
    g, u = gu[:, :D_FF], gu[:, D_FF:]
    return (g * jax.nn.sigmoid(g) * u,)


def _swiglu_bwd(gu, dact):
    _, vjp = jax.vjp(_swiglu, gu)
    return vjp((dact,))


def _causal_probs(s, iq):
    qpos = iq * TQ + lax.broadcasted_iota(jnp.int32, s.shape, 0)
    kpos = lax.broadcasted_iota(jnp.int32, s.shape, 1)
    s = jnp.where(kpos <= qpos, s, NEG_INF)
    e = jnp.exp(s - jnp.max(s, axis=-1, keepdims=True))
    return e / jnp.sum(e, axis=-1, keepdims=True)


def _softmax_vjp(p, dp):
    return p * (dp - jnp.sum(p * dp, axis=-1, keepdims=True))


def _mla_specs():
    return [pl.BlockSpec((TQ, QG), lambda h, i: (i, h)),
            pl.BlockSpec((LP, 2 * HD), lambda h, i: (0, h)),
            pl.BlockSpec((LP, LANE), lambda h, i: (0, 0)),
            pl.BlockSpec((TQ, LANE), lambda h, i: (i, 0)),
            pl.BlockSpec((TQ, LANE), lambda h, i: (i, 0))]


def _mla_parts(q_ref, kv_ref, cos_ref, sin_ref):
    q, kv = q_ref[...], kv_ref[...]
    qn = q[:, :HD]
    qp = q[:, HD:2 * HD] * cos_ref[...] + q[:, 2 * HD:] * sin_ref[...]
    return qn, qp, kv[:, :HD], kv[:, HD:]


def _mla_fwd(q, kv, kper, cosp, sinp):
    def body(q_ref, kv_ref, kp_ref, cos_ref, sin_ref, o_ref):
        qn, qp, kn, v = _mla_parts(q_ref, kv_ref, cos_ref, sin_ref)
        s = (_dot(qn, kn, _NT) + _dot(qp, kp_ref[...], _NT)) * MLA_SCALE
        o_ref[...] = _dot(_causal_probs(s, pl.program_id(1)), v, _NN).astype(o_ref.dtype)

    return _pcall(body, name="mla_fwd", grid=(HEADS, LP // TQ), in_specs=_mla_specs(),
                  out_specs=pl.BlockSpec((TQ, HD), lambda h, i: (i, h)),
                  out_shape=jax.ShapeDtypeStruct((LP, BW), BF16),
                  compiler_params=_params(("parallel", "parallel")))(q, kv, kper, cosp, sinp)


def _mla_bwd(q, kv, kper, cosp, sinp, do):
    def body(q_ref, kv_ref, kp_ref, cos_ref, sin_ref, do_ref, dq_ref, dkv_ref, dkp_ref):
        h, iq = pl.program_id(0), pl.program_id(1)
        qn, qp, kn, v = _mla_parts(q_ref, kv_ref, cos_ref, sin_ref)
        kp, dout = kp_ref[...], do_ref[...]
        p = _causal_probs((_dot(qn, kn, _NT) + _dot(qp, kp, _NT)) * MLA_SCALE, iq)
        ds = _softmax_vjp(p, _dot(dout, v, _NT)) * MLA_SCALE
        dqp = _dot(ds, kp, _NN)
        dq = jnp.concatenate([_dot(ds, kn, _NN), dqp * cos_ref[...], dqp * sin_ref[...]], axis=1)
        dq_ref[...] = dq.astype(dq_ref.dtype)

        @pl.when(iq == 0)
        def _():
            dkv_ref[...] = jnp.zeros_like(dkv_ref)

        dkv_ref[...] += jnp.concatenate([_dot(ds, qn, _TN), _dot(p, dout, _TN)], axis=1)

        @pl.when((iq == 0) & (h == 0))
        def _():
            dkp_ref[...] = jnp.zeros_like(dkp_ref)

        dkp_ref[...] += _dot(ds, qp, _TN)

    return _pcall(
        body, name="mla_bwd", grid=(HEADS, LP // TQ),
        in_specs=_mla_specs() + [pl.BlockSpec((TQ, HD), lambda h, i: (i, h))],
        out_specs=[pl.BlockSpec((TQ, QG), lambda h, i: (i, h)), pl.BlockSpec((LP, 2 * HD), lambda h, i: (0, h)),
                   pl.BlockSpec((LP, LANE), lambda h, i: (0, 0))],
        out_shape=[jax.ShapeDtypeStruct((LP, HEADS * QG), BF16), jax.ShapeDtypeStruct((LP, 2 * BW), F32),
                   jax.ShapeDtypeStruct((LP, LANE), F32)],
        compiler_params=_params(("arbitrary", "arbitrary")))(q, kv, kper, cosp, sinp, do)


def _fox_specs():
    cq, ck, cv = _cb(OFF_FOX, HD), _cb(OFF_FOX + BW, HD), _cb(OFF_FOX + 2 * BW, HD)
    return [pl.BlockSpec((TQ, HD), lambda h, i: (i, cq + h)),
            pl.BlockSpec((LP, HD), lambda h, i: (0, ck + h)),
            pl.BlockSpec((LP, HD), lambda h, i: (0, cv + h)),
            pl.BlockSpec((1, TQ, 1), lambda h, i: (h, i, 0)),
            pl.BlockSpec((1, 1, LP), lambda h, i: (h, 0, 0))]


def _fox_probs(q_ref, k_ref, cq_ref, ck_ref, iq):
    s = _dot(q_ref[...], k_ref[...], _NT) * FOX_SCALE + (cq_ref[0] - ck_ref[0])
    return _causal_probs(s, iq)


def _fox_fwd(p_all, cq3, ck3):
    def body(q_ref, k_ref, v_ref, cq_ref, ck_ref, o_ref):
        p = _fox_probs(q_ref, k_ref, cq_ref, ck_ref, pl.program_id(1))
        o_ref[...] = _dot(p, v_ref[...], _NN).astype(o_ref.dtype)

    return _pcall(body, name="fox_fwd", grid=(HEADS, LP // TQ), in_specs=_fox_specs(),
                  out_specs=pl.BlockSpec((TQ, HD), lambda h, i: (i, h)),
                  out_shape=jax.ShapeDtypeStruct((LP, BW), BF16),
                  compiler_params=_params(("parallel", "parallel")))(p_all, p_all, p_all, cq3, ck3)


def _fox_bwd(p_all, cq3, ck3, do):
    def body(q_ref, k_ref, v_ref, cq_ref, ck_ref, do_ref, dq_ref, dk_ref, dv_ref, dcq_ref, dck_ref):
        iq = pl.program_id(1)
        p = _fox_probs(q_ref, k_ref, cq_ref, ck_ref, iq)
        dout = do_ref[...]
        ds = _softmax_vjp(p, _dot(dout, v_ref[...], _NT))
        dss = ds * FOX_SCALE
        dq_ref[...] = _dot(dss, k_ref[...], _NN).astype(dq_ref.dtype)
        dcq_ref[0] = jnp.sum(ds, axis=1, keepdims=True)

        @pl.when(iq == 0)
        def _():
            dk_ref[...] = jnp.zeros_like(dk_ref)
            dv_ref[...] = jnp.zeros_like(dv_ref)
            dck_ref[...] = jnp.zeros_like(dck_ref)

        dk_ref[...] += _dot(dss, q_ref[...], _TN)
        dv_ref[...] += _dot(p, dout, _TN)
        dck_ref[0] -= jnp.sum(ds, axis=0, keepdims=True)

    head_rows = pl.BlockSpec((TQ, HD), lambda h, i: (i, h))
    head_all = pl.BlockSpec((LP, HD), lambda h, i: (0, h))
    return _pcall(
        body, name="fox_bwd", grid=(HEADS, LP // TQ), in_specs=_fox_specs() + [head_rows],
        out_specs=[head_rows, head_all, head_all, pl.BlockSpec((1, TQ, 1), lambda h, i: (h, i, 0)),
                   pl.BlockSpec((1, 1, LP), lambda h, i: (h, 0, 0))],
        out_shape=[jax.ShapeDtypeStruct((LP, BW), BF16)] + [jax.ShapeDtypeStruct((LP, BW), F32)] * 2
        + [jax.ShapeDtypeStruct((HEADS, LP, 1), F32), jax.ShapeDtypeStruct((HEADS, 1, LP), F32)],
        compiler_params=_params(("arbitrary", "arbitrary")))(p_all, p_all, p_all, cq3, ck3, do)


def _log_sigmoid(x):
    return jnp.minimum(x, 0.0) - jnp.log(1.0 + jnp.exp(-jnp.abs(x)))


def _decay_fwd(p_all, b_pad):
    tc = TM_ROW

    def body(fl_ref, b_ref, c_ref):
        lf = _log_sigmoid(fl_ref[...] + b_ref[...])
        r = pl.program_id(0) * tc + lax.broadcasted_iota(jnp.int32, (tc, LP), 0)
        s = lax.broadcasted_iota(jnp.int32, (tc, LP), 1)
        c_ref[...] = jnp.dot((s <= r).astype(F32), lf, precision=lax.Precision.HIGHEST, preferred_element_type=F32)

    return _pcall(body, name="decay_fwd", grid=(LP // tc,),
                  in_specs=[pl.BlockSpec((LP, LANE), lambda i: (0, _cb(OFF_FL, LANE))),
                            pl.BlockSpec((1, LANE), lambda i: (0, 0))],
                  out_specs=pl.BlockSpec((tc, LANE), lambda i: (i, 0)),
                  out_shape=jax.ShapeDtypeStruct((LP, LANE), F32), compiler_params=_params(("parallel",)))(p_all, b_pad)


def _decay_bwd(p_all, b_pad, dc):
    tc = TM_ROW

    def body(fl_ref, b_ref, dc_ref, dfl_ref, db_ref):
        i = pl.program_id(0)
        r = i * tc + lax.broadcasted_iota(jnp.int32, (tc, LP), 0)
        t = lax.broadcasted_iota(jnp.int32, (tc, LP), 1)
        dlf = jnp.dot((t >= r).astype(F32), dc_ref[...], precision=lax.Precision.HIGHEST, preferred_element_type=F32)
        dfl = dlf * jax.nn.sigmoid(-(fl_ref[...] + b_ref[...]))
        dfl_ref[...] = dfl.astype(dfl_ref.dtype)

        @pl.when(i == 0)
        def _():
            db_ref[...] = jnp.zeros_like(db_ref)

        db_ref[...] += jnp.sum(dfl, axis=0, keepdims=True)

    return _pcall(body, name="decay_bwd", grid=(LP // tc,),
                  in_specs=[pl.BlockSpec((tc, LANE), lambda i: (i, _cb(OFF_FL, LANE))),
                            pl.BlockSpec((1, LANE), lambda i: (0, 0)), pl.BlockSpec((LP, LANE), lambda i: (0, 0))],
                  out_specs=[pl.BlockSpec((tc, LANE), lambda i: (i, 0)), pl.BlockSpec((1, LANE), lambda i: (0, 0))],
                  out_shape=[jax.ShapeDtypeStruct((LP, LANE), BF16), jax.ShapeDtypeStruct((1, LANE), F32)],
                  compiler_params=_params(("arbitrary",)))(p_all, b_pad, dc)


def _conv_specs():
    c0 = _cb(OFF_CONV, CW)
    step = BW // CW
    return [pl.BlockSpec((LP, CW), lambda j: (0, c0 + j)), pl.BlockSpec((LP, CW), lambda j: (0, c0 + step + j)),
            pl.BlockSpec((LP, CW), lambda j: (0, c0 + 2 * step + j)),
            pl.BlockSpec((3, CW), lambda j: (0, j))]


def _shift_down(x, k, t):
    return jnp.where(t >= k, pltpu.roll(x, k, 0), 0.0)


def _shift_up(x, k, t):
    return jnp.where(t < LP - k, pltpu.roll(x, LP - k, 0), 0.0)


def _conv_fwd(p_all, cw):
    def body(b_ref, c_ref, x_ref, w_ref, o_ref):
        t = lax.broadcasted_iota(jnp.int32, (LP, CW), 0)
        uu = c_ref[...] * x_ref[...]
        w = w_ref[...]
        u = w[2:3] * uu + w[1:2] * _shift_down(uu, 1, t) + w[0:1] * _shift_down(uu, 2, t)
        o_ref[...] = (b_ref[...] * u).astype(o_ref.dtype)

    return _pcall(body, name="conv_fwd", grid=(BW // CW,), in_specs=_conv_specs(),
                  out_specs=pl.BlockSpec((LP, CW), lambda j: (0, j)), out_shape=jax.ShapeDtypeStruct((LP, BW), BF16),
                  compiler_params=_params(("parallel",)))(p_all, p_all, p_all, cw)


def _conv_bwd(p_all, cw, do):
    def body(b_ref, c_ref, x_ref, w_ref, do_ref, d_ref, dw_ref):
        t = lax.broadcasted_iota(jnp.int32, (LP, CW), 0)
        cc, xx, w, dout = c_ref[...], x_ref[...], w_ref[...], do_ref[...]
        uu = cc * xx
        s1, s2 = _shift_down(uu, 1, t), _shift_down(uu, 2, t)
        u = w[2:3] * uu + w[1:2] * s1 + w[0:1] * s2
        du = dout * b_ref[...]
        duu = w[2:3] * du + w[1:2] * _shift_up(du, 1, t) + w[0:1] * _shift_up(du, 2, t)
        d_ref[0] = (dout * u).astype(d_ref.dtype)
        d_ref[1] = (duu * xx).astype(d_ref.dtype)
        d_ref[2] = (duu * cc).astype(d_ref.dtype)
        dw_ref[0:1, :] = jnp.sum(du * s2, axis=0, keepdims=True)
        dw_ref[1:2, :] = jnp.sum(du * s1, axis=0, keepdims=True)
        dw_ref[2:3, :] = jnp.sum(du * uu, axis=0, keepdims=True)

    return _pcall(body, name="conv_bwd", grid=(BW // CW,),
                  in_specs=_conv_specs() + [pl.BlockSpec((LP, CW), lambda j: (0, j))],
                  out_specs=[pl.BlockSpec((3, LP, CW), lambda j: (0, 0, j)), pl.BlockSpec((3, CW), lambda j: (0, j))],
                  out_shape=[jax.ShapeDtypeStruct((3, LP, BW), BF16), jax.ShapeDtypeStruct((3, BW), F32)],
                  compiler_params=_params(("parallel",)))(p_all, p_all, p_all, cw, do)


def _place():
    return lax.axis_index("x"), lax.axis_index("y"), lax.axis_index("c")


def _dest_slice(ref, kind, d):
    if kind == "raw":
        return ref.at[d]
    nd = len(ref.shape)
    if kind == "col":
        w = ref.shape[-1] // N_DEV
        return ref.at[(slice(None),) * (nd - 1) + (pl.ds(d * w, w),)]
    r = ref.shape[-2] // N_DEV
    return ref.at[(slice(None),) * (nd - 2) + (pl.ds(d * r, r), slice(None))]


def _unsharded_shape(shape, kind, lead=N_DEV):
    if kind == "raw":
        return (lead,) + shape
    if kind == "col":
        return shape[:-1] + (N_DEV * shape[-1],)
    return shape[:-2] + (N_DEV * shape[-2], shape[-1])


def _all_gather(arrs, kinds, name):
    n = len(arrs)

    def body(*refs):
        ins, outs = refs[:n], refs[n:2 * n]
        send, recv, loc = refs[2 * n:]
        x, y, c = _place()
        sib = (x, y, 1 - c)
        chips = [(1 - x, y), (x, 1 - y), (1 - x, 1 - y)]

        def idx(px, py, pc):
            return 4 * px + 2 * py + pc

        def copy(j, k, block, to, src=None):
            dst = _dest_slice(outs[k], kinds[k], idx(*block))
            return pltpu.make_async_remote_copy(src_ref=dst if src is None else src, dst_ref=dst, send_sem=send.at[j, k],
                                                recv_sem=recv.at[j, k], device_id=to, device_id_type=MESH)

        me = (x, y, c)
        mine = [pltpu.make_async_copy(ins[k], _dest_slice(outs[k], kinds[k], idx(*me)), loc.at[k]) for k in range(n)]
        for cp in mine:
            cp.start()
        first = [copy(0, k, me, sib, src=ins[k]) for k in range(n)]
        first += [copy(1 + j, k, me, (*chip, c), src=ins[k]) for j, chip in enumerate(chips) for k in range(n)]
        for cp in first:
            cp.start()
        passed = []
        for j, chip in enumerate(chips):
            for k in range(n):
                copy(1 + j, k, (*chip, c), me, src=ins[k]).wait_recv()
                cp = copy(4 + j, k, (*chip, c), sib)
                cp.start()
                passed.append(cp)
        for k in range(n):
            copy(0, k, sib, me, src=ins[k]).wait_recv()
        for j, chip in enumerate(chips):
            for k in range(n):
                copy(4 + j, k, (*chip, 1 - c), me).wait_recv()
        for cp in first + passed:
            cp.wait_send()
        for cp in mine:
            cp.wait()

    return _pcall(body, name=name, in_specs=[_ANY] * n, out_specs=[_ANY] * n,
                  out_shape=[jax.ShapeDtypeStruct(_unsharded_shape(a.shape, kd), a.dtype) for a, kd in zip(arrs, kinds)],
                  scratch_shapes=[pltpu.SemaphoreType.DMA((7, n)), pltpu.SemaphoreType.DMA((7, n)),
                                  pltpu.SemaphoreType.DMA((n,))])(*arrs)


def _shard_shape(a, kind):
    if kind == "raw":
        return a.shape[2:]
    if kind == "col":
        return a.shape[:-1] + (a.shape[-1] // N_DEV,)
    return a.shape[:-2] + (a.shape[-2] // N_DEV, a.shape[-1])


def _swap_sibling(arrs, kinds, name):
    n = len(arrs)
    npieces = sum(1 if kd == "raw" else 4 for kd in kinds)

    def body(*refs):
        ins, outs = refs[:n], refs[n:2 * n]
        send, recv = refs[2 * n:]
        x, y, c = _place()
        pieces = []
        for k in range(n):
            if kinds[k] == "raw":
                pieces.append((ins[k].at[1 - c], outs[k]))
            else:
                pieces += [(_dest_slice(ins[k], kinds[k], 2 * p + 1 - c), outs[k].at[p]) for p in range(4)]
        cps = [pltpu.make_async_remote_copy(src_ref=src, dst_ref=dst, send_sem=send.at[i], recv_sem=recv.at[i],
                                            device_id=(x, y, 1 - c), device_id_type=MESH)
               for i, (src, dst) in enumerate(pieces)]
        for cp in cps:
            cp.start()
        for cp in cps:
            cp.wait()

    return _pcall(body, name=name, in_specs=[_ANY] * n, out_specs=[_ANY] * n,
                  out_shape=[jax.ShapeDtypeStruct((4,) + _shard_shape(a, kd), a.dtype) for a, kd in zip(arrs, kinds)],
                  scratch_shapes=[pltpu.SemaphoreType.DMA((npieces,)), pltpu.SemaphoreType.DMA((npieces,))])(*arrs)


def _swap_chips(arrs, name):
    n = len(arrs)

    def body(*refs):
        ins, outs = refs[:n], refs[n:2 * n]
        send, recv, loc = refs[2 * n:]
        x, y, c = _place()
        mychip = 2 * x + y
        mine = [pltpu.make_async_copy(ins[k].at[mychip], outs[k].at[mychip], loc.at[k]) for k in range(n)]
        for cp in mine:
            cp.start()
        cps = []
        for j, (cx, cy) in enumerate([(1 - x, y), (x, 1 - y), (1 - x, 1 - y)]):
            for k in range(n):
                cps.append(pltpu.make_async_remote_copy(
                    src_ref=ins[k].at[2 * cx + cy], dst_ref=outs[k].at[mychip], send_sem=send.at[j, k],
                    recv_sem=recv.at[j, k], device_id=(cx, cy, c), device_id_type=MESH))
        for cp in cps:
            cp.start()
        for cp in cps:
            cp.wait()
        for cp in mine:
            cp.wait()

    return _pcall(body, name=name, in_specs=[_ANY] * n, out_specs=[_ANY] * n,
                  out_shape=[jax.ShapeDtypeStruct(a.shape, a.dtype) for a in arrs],
                  scratch_shapes=[pltpu.SemaphoreType.DMA((3, n)), pltpu.SemaphoreType.DMA((3, n)),
                                  pltpu.SemaphoreType.DMA((n,))])(*arrs)


_HBM = pl.BlockSpec(memory_space=pltpu.HBM)
_SEM = pl.BlockSpec(memory_space=pltpu.SEMAPHORE)
_SIDE_EFFECT = pltpu.SideEffectType.DATAFLOW_SIDE_EFFECTING


def _split_copies(plan, refs, ns, nd):
    send, recv = refs[ns + nd], refs[ns + nd + 1]
    return [pltpu.make_async_remote_copy(src_ref=src, dst_ref=dst, send_sem=send.at[i], recv_sem=recv.at[i],
                                         device_id=dev, device_id_type=MESH)
            for i, (src, dst, dev) in enumerate(plan(refs[:ns], refs[ns:ns + nd]))]


def _split_start(plan, ncopies, srcs, dsts, name):
    ns, nd = len(srcs), len(dsts)

    def body(*refs):
        copies = _split_copies(plan, refs, ns, nd)
        assert len(copies) == ncopies
        for cp in copies:
            cp.start()
        refs[-1][...] = jnp.zeros_like(refs[-1])

    bufs = [pltpu.with_memory_space_constraint(a, pltpu.HBM) for a in list(srcs) + list(dsts)]
    res = _pcall(
        body, name=name, in_specs=[_HBM] * (ns + nd),
        out_specs=[_SEM, _SEM] + [_HBM] * (ns + nd) + [pl.BlockSpec(memory_space=pltpu.VMEM)],
        out_shape=[pltpu.SemaphoreType.DMA((ncopies,)), pltpu.SemaphoreType.DMA((ncopies,))]
        + [pltpu.HBM(a.shape, a.dtype) for a in bufs] + [jax.ShapeDtypeStruct((8, LANE), F32)],
        input_output_aliases={i: 2 + i for i in range(ns + nd)},
        compiler_params=pltpu.CompilerParams(has_side_effects=_SIDE_EFFECT))(*bufs)
    return res[0], res[1], list(res[2:2 + ns]), list(res[2 + ns:2 + ns + nd]), res[-1]


def _split_wait(plan, send, recv, srcs, dsts, after, name):
    ns, nd = len(srcs), len(dsts)

    def body(*refs):
        for cp in _split_copies(plan, refs, ns, nd):
            cp.wait_send()
            cp.wait_recv()

    res = _pcall(
        body, name=name, in_specs=[_HBM] * (ns + nd) + [_SEM, _SEM, _ANY], out_specs=[_HBM] * (ns + nd),
        out_shape=[pltpu.HBM(a.shape, a.dtype) for a in list(srcs) + list(dsts)],
        input_output_aliases={i: i for i in range(ns + nd)},
        compiler_params=pltpu.CompilerParams(has_side_effects=_SIDE_EFFECT))(*srcs, *dsts, send, recv, after)
    return list(res[:ns]), list(res[ns:])


def _sibling_plan(kinds):
    def plan(srcs, dsts):
        x, y, c = _place()
        return [(_dest_slice(srcs[k], kinds[k], 2 * p + 1 - c), dsts[k].at[p], (x, y, 1 - c))
                for k in range(len(kinds)) for p in range(4)]
    return plan


def _chips_plan(n):
    def plan(srcs, dsts):
        x, y, c = _place()
        return [(srcs[k].at[2 * cx + cy], dsts[k].at[j], (cx, cy, c))
                for j, (cx, cy) in enumerate([(1 - x, y), (x, 1 - y), (1 - x, 1 - y)]) for k in range(n)]
    return plan


def _gather_plan(kinds, layer):
    def plan(srcs, dsts):
        x, y, c = _place()
        me = 4 * x + 2 * y + c
        return [(srcs[k].at[layer], _dest_slice(dsts[k], kinds[k], me), dev)
                for k in range(len(kinds)) for dev in [(x, y, 1 - c), (1 - x, y, c), (x, 1 - y, c), (1 - x, 1 - y, c)]]
    return plan


def _pass_on_plan(kinds):
    def plan(srcs, dsts):
        x, y, c = _place()
        parts = [_dest_slice(dsts[k], kinds[k], 4 * px + 2 * py + pc) for k in range(len(kinds))
                 for px, py, pc in [(1 - x, y, c), (x, 1 - y, c), (1 - x, 1 - y, c), (x, y, 1 - c)]]
        return [(part, part, (x, y, 1 - c)) for part in parts]
    return plan


def _slice_shape(shape, kind):
    if kind == "raw":
        return shape[1:]
    if kind == "col":
        return shape[:-1] + (shape[-1] // N_DEV,)
    return shape[:-2] + (shape[-2] // N_DEV, shape[-1])


def _all_to_all(arrs, src_kinds, dst_kinds, name):
    n = len(arrs)

    def body(*refs):
        ins, outs = refs[:n], refs[n:2 * n]
        send, recv, loc = refs[2 * n:]
        x, y, c = _place()
        me = 4 * x + 2 * y + c
        mine = [pltpu.make_async_copy(_dest_slice(ins[k], src_kinds[k], me), _dest_slice(outs[k], dst_kinds[k], me),
                                      loc.at[k]) for k in range(n)]
        for cp in mine:
            cp.start()
        cps = []
        for r in range(1, N_DEV):
            px, py, pc = (1 - x if r & 4 else x), (1 - y if r & 2 else y), (1 - c if r & 1 else c)
            for k in range(n):
                cps.append(pltpu.make_async_remote_copy(
                    src_ref=_dest_slice(ins[k], src_kinds[k], 4 * px + 2 * py + pc),
                    dst_ref=_dest_slice(outs[k], dst_kinds[k], me), send_sem=send.at[r - 1, k],
                    recv_sem=recv.at[r - 1, k], device_id=(px, py, pc), device_id_type=MESH))
        for cp in cps:
            cp.start()
        for cp in cps:
            cp.wait()
        for cp in mine:
            cp.wait()

    out_shape = [jax.ShapeDtypeStruct(_unsharded_shape(_slice_shape(a.shape, sk), dk), a.dtype)
                 for a, sk, dk in zip(arrs, src_kinds, dst_kinds)]
    return _pcall(body, name=name, in_specs=[_ANY] * n, out_specs=[_ANY] * n, out_shape=out_shape,
                  scratch_shapes=[pltpu.SemaphoreType.DMA((N_DEV - 1, n)), pltpu.SemaphoreType.DMA((N_DEV - 1, n)),
                                  pltpu.SemaphoreType.DMA((n,))])(*arrs)


def _rows_tile(rows, cols=0):
    cap = 512 * 1024
    return _pick(rows, [t for t in (128, 64, 32, 16) if t * cols <= cap])


def _pair_sum(g, r1, core, kind, name):
    _, groups, rows, cols = r1.shape
    tr = _rows_tile(rows, cols)
    nr = rows // tr
    if kind == "raw":
        g_spec = pl.BlockSpec((None, None, None, tr, cols), lambda p, q, i, c: (c[0], p, q, i, 0))
    elif kind == "col":
        g_spec = pl.BlockSpec((None, tr, cols), lambda p, q, i, c: (q, i, 2 * p + c[0]))
    else:
        g_spec = pl.BlockSpec((None, tr, cols), lambda p, q, i, c: (q, (2 * p + c[0]) * nr + i, 0))
    r_spec = pl.BlockSpec((None, None, tr, cols), lambda p, q, i, c: (p, q, i, 0))

    def body(c_ref, g_ref, r_ref, o_ref):
        o_ref[...] = (g_ref[...].astype(F32) + r_ref[...].astype(F32)).astype(o_ref.dtype)

    return _pcall(
        body, name=name,
        grid_spec=pltpu.PrefetchScalarGridSpec(num_scalar_prefetch=1, grid=(4, groups, nr), in_specs=[g_spec, r_spec],
                                               out_specs=r_spec),
        out_shape=jax.ShapeDtypeStruct(r1.shape, r1.dtype),
        compiler_params=_params(("parallel", "parallel", "parallel")))(core, g, r1)


def _adamw(g, w, m, v):
    m = ADAM_B1 * m + (1.0 - ADAM_B1) * g
    v = ADAM_B2 * v + (1.0 - ADAM_B2) * jnp.square(g)
    m_hat = m / (1.0 - ADAM_B1 ** ADAM_STEP)
    v_hat = v / (1.0 - ADAM_B2 ** ADAM_STEP)
    return -ADAM_LR * (m_hat / (jnp.sqrt(v_hat) + ADAM_EPS) + ADAM_WD * w), m, v


def _sum_adamw(parts, w, m, v, name):
    npart, groups, rows, cols = parts.shape
    tr = _rows_tile(rows, cols)

    def body(p_ref, w_ref, m_ref, v_ref, g_ref, d_ref, nm_ref, nv_ref):
        g = p_ref[0].astype(F32)
        for k in range(1, npart):
            g = g + p_ref[k].astype(F32)
        g_ref[...] = g
        d_ref[...], nm_ref[...], nv_ref[...] = _adamw(g, w_ref[...], m_ref[...], v_ref[...])

    blk = pl.BlockSpec((None, tr, cols), lambda q, i: (q, i, 0))
    return _pcall(body, name=name, grid=(groups, rows // tr),
                  in_specs=[pl.BlockSpec((npart, None, tr, cols), lambda q, i: (0, q, i, 0)), blk, blk, blk],
                  out_specs=[blk] * 4, out_shape=[jax.ShapeDtypeStruct((groups, rows, cols), F32)] * 4,
                  compiler_params=_params(("parallel", "parallel")))(parts, w, m, v)


def _layer_sum_specs(own, layer):
    _, groups, rows, cols = own.shape
    tr = _rows_tile(rows, cols)
    own_spec = pl.BlockSpec((None, None, tr, cols), lambda q, i, ch: (ch[0], q, i, 0))
    theirs_spec = pl.BlockSpec((3, None, tr, cols), lambda q, i, ch: (0, q, i, 0))
    stacked = pl.BlockSpec((None, tr, cols), lambda q, i, ch: (layer * groups + q, i, 0))
    return (groups, rows // tr), own_spec, theirs_spec, stacked


def _layer_sum(own_ref, theirs_ref):
    g = own_ref[...].astype(F32)
    for k in range(3):
        g = g + theirs_ref[k].astype(F32)
    return g


def _sum_parts_layer(own, theirs, chip, buf, layer, name):
    grid, own_spec, theirs_spec, stacked = _layer_sum_specs(own, layer)

    def body(ch_ref, own_ref, theirs_ref, buf_ref, o_ref):
        o_ref[...] = _layer_sum(own_ref, theirs_ref)

    return _pcall(
        body, name=name,
        grid_spec=pltpu.PrefetchScalarGridSpec(num_scalar_prefetch=1, grid=grid, in_specs=[own_spec, theirs_spec, _ANY],
                                               out_specs=stacked),
        out_shape=jax.ShapeDtypeStruct(buf.shape, buf.dtype), input_output_aliases={3: 0},
        compiler_params=_params(("parallel", "parallel")))(chip, own, theirs, buf)


def _sum_adamw_layer(own, theirs, chip, w, m, v, outs, layer, deps, name):
    grid, own_spec, theirs_spec, stacked = _layer_sum_specs(own, layer)

    def body(ch_ref, own_ref, theirs_ref, w_ref, m_ref, v_ref, *rest):
        g_ref, d_ref, nm_ref, nv_ref = rest[-4:]
        g = _layer_sum(own_ref, theirs_ref)
        g_ref[...] = g
        d_ref[...], nm_ref[...], nv_ref[...] = _adamw(g, w_ref[...], m_ref[...], v_ref[...])

    return _pcall(
        body, name=name,
        grid_spec=pltpu.PrefetchScalarGridSpec(
            num_scalar_prefetch=1, grid=grid,
            in_specs=[own_spec, theirs_spec, stacked, stacked, stacked] + [_ANY] * (4 + len(deps)), out_specs=[stacked] * 4),
        out_shape=[jax.ShapeDtypeStruct(o.shape, o.dtype) for o in outs],
        input_output_aliases={6 + i: i for i in range(4)},
        compiler_params=_params(("parallel", "parallel")))(chip, own, theirs, w, m, v, *outs, *deps)


def _rot_cols(w):
    return jnp.concatenate([-w[..., ROPE // 2:], w[..., :ROPE // 2]], axis=-1)


def _rot_cols_t(dw):
    return jnp.concatenate([dw[..., ROPE // 2:], -dw[..., :ROPE // 2]], axis=-1)


_IN_SPLITS = np.cumsum([0, Q_RANK, KV_RANK, ROPE, BW, BW, BW, BW, BW, BW, HEADS, 3 * D_MODEL])


def _ext_w_in(w):
    o = _IN_SPLITS
    kpe = w[..., o[2]:o[3]]
    z = lambda n: jnp.zeros(w.shape[:-1] + (n,), w.dtype)
    return jnp.concatenate([w[..., o[10]:o[11]], w[..., o[3]:o[9]], w[..., o[0]:o[2]], kpe, z(LANE - ROPE),
                            _rot_cols(kpe), z(LANE - ROPE), w[..., o[9]:o[10]], z(2 * LANE - HEADS)], axis=-1)


def _unext_w_in(dw):
    kpe = dw[..., OFF_KPE:OFF_KPE + ROPE] + _rot_cols_t(dw[..., OFF_KROT:OFF_KROT + ROPE])
    return jnp.concatenate([dw[..., OFF_CQ:OFF_KPE], kpe, dw[..., OFF_CONV:OFF_CQ], dw[..., OFF_FL:OFF_FL + HEADS],
                            dw[..., OFF_GATE:OFF_CONV]], axis=-1)


def _ext_w_uq(w):
    w3 = w.reshape(w.shape[:-1] + (HEADS, HD + ROPE))
    pe = w3[..., HD:]
    z = jnp.zeros(w3.shape[:-1] + (LANE - ROPE,), w.dtype)
    return jnp.concatenate([w3[..., :HD], pe, z, _rot_cols(pe), z], axis=-1).reshape(w.shape[:-1] + (HEADS * QG,))


def _unext_w_uq(dw):
    d3 = dw.reshape(dw.shape[:-1] + (HEADS, QG))
    pe = d3[..., HD:HD + ROPE] + _rot_cols_t(d3[..., 2 * HD:2 * HD + ROPE])
    return jnp.concatenate([d3[..., :HD], pe], axis=-1).reshape(dw.shape[:-1] + (HEADS * (HD + ROPE),))


_BIG = (("meta", "col", F32), ("w_in", "row", BF16), ("w_uq", "row", BF16), ("w_ukv", "col", BF16),
        ("conv_w", "col", F32), ("w_branch", "col", BF16), ("w_out", "row", BF16), ("w_ffn_in", "col", BF16),
        ("w_ffn_out", "row", BF16))
_SMALL = ("b_forget", "g_q_lat", "g_kv_lat", "g_mix_pre", "g_mix_post", "g_ffn_pre", "g_ffn_post")
_ORDER = ("meta", "w_in", "b_forget", "g_q_lat", "g_kv_lat", "w_uq", "w_ukv", "conv_w", "w_branch", "w_out",
          "w_ffn_in", "w_ffn_out", "g_mix_pre", "g_mix_post", "g_ffn_pre", "g_ffn_post")


def _unshard_cols(g):
    g = jnp.moveaxis(g, 0, -2)
    return g.reshape(g.shape[:-2] + (g.shape[-2] * g.shape[-1],))


def _as3d(a, lead=0):
    return a.reshape(a.shape[:lead] + (-1,) + a.shape[-2:])


def kernel(x, meta, w_in, b_forget, g_q_lat, g_kv_lat, w_uq, w_ukv, conv_w, w_branch, w_out, w_ffn_in, w_ffn_out, g_mix_pre, g_mix_post, g_ffn_pre, g_ffn_post, loss_target, m_meta, m_w_in, m_b_forget, m_g_q_lat, m_g_kv_lat, m_w_uq, m_w_ukv, m_conv_w, m_w_branch, m_w_out, m_w_ffn_in, m_w_ffn_out, m_g_mix_pre, m_g_mix_post, m_g_ffn_pre, m_g_ffn_post, v_meta, v_w_in, v_b_forget, v_g_q_lat, v_g_kv_lat, v_w_uq, v_w_ukv, v_conv_w, v_w_branch, v_w_out, v_w_ffn_in, v_w_ffn_out, v_g_mix_pre, v_g_mix_post, v_g_ffn_pre, v_g_ffn_post):
    given = dict(locals())
    core = lax.axis_index("c").astype(jnp.int32).reshape(1)

    rows_in, rows_uq = _all_to_all([w_in.astype(BF16), w_uq.astype(BF16)], ("row", "row"), ("raw", "raw"),
                                   "rows_of_column_shards")
    shards = {n: given[n].astype(dt) for n, _, dt in _BIG}
    shards["w_in"] = _ext_w_in(_unshard_cols(rows_in))
    shards["w_uq"] = _ext_w_uq(_unshard_cols(rows_uq))
    (meta_full,) = _all_gather([shards["meta"]], ("col",), "gather_meta")
    per_layer = tuple((n, kd) for n, kd, _ in _BIG if n != "meta")
    layer_kinds = tuple(kd for _, kd in per_layer)
    pass_plan = _pass_on_plan(layer_kinds)

    def start_gather(layer, stacked):
        bufs = [lax.empty(_unsharded_shape(a.shape[1:], kd), a.dtype) for a, kd in zip(stacked, layer_kinds)]
        send, recv, stacked, bufs, token = _split_start(_gather_plan(layer_kinds, layer), 4 * len(per_layer), stacked, bufs,
                                                        "gather_start_%d" % layer)
        return dict(send=send, recv=recv, srcs=stacked, dsts=bufs, token=token)

    def finish_gather(layer, fly, after):
        stacked, bufs = _split_wait(_gather_plan(layer_kinds, layer), fly["send"], fly["recv"], fly["srcs"], fly["dsts"],
                                    after, "gather_wait_%d" % layer)
        send, recv, _, bufs, token = _split_start(pass_plan, 4 * len(per_layer), [], bufs, "pass_on_start_%d" % layer)
        _, bufs = _split_wait(pass_plan, send, recv, [], bufs, token, "pass_on_wait_%d" % layer)
        return stacked, {n: b for (n, _), b in zip(per_layer, bufs)}

    gathering = start_gather(0, [shards[n] for n, _ in per_layer])

    pos = jnp.arange(LP, dtype=F32)[:, None]
    inv_freq = 1.0 / (ROPE_THETA ** (jnp.arange(0, ROPE, 2, dtype=F32) / ROPE))
    ang = pos * inv_freq[None, :]
    zpad = jnp.zeros((LP, LANE - ROPE), F32)
    cosp = jnp.concatenate([jnp.cos(ang), jnp.cos(ang), zpad], axis=1)
    sinp = jnp.concatenate([jnp.sin(ang), jnp.sin(ang), zpad], axis=1)

    tail = jnp.zeros((LP - L_TOK, D_MODEL), F32)
    h = jnp.concatenate([meta_full, x[0], tail], axis=0)
    ltp = jnp.concatenate([jnp.zeros((N_META, D_MODEL), F32), loss_target[0], tail], axis=0)
    row = jnp.arange(LP)[:, None]
    rmask = ((row >= N_META) & (row < L_TOK)).astype(F32)

    def vec(a, l):
        return a[l][None, :]

    wl = []
    for l in range(DEPTH):
        wl.append(dict(
            b_pad=jnp.concatenate([b_forget[l], jnp.zeros((LANE - HEADS,), F32)])[None, :],
            gq=vec(g_q_lat, l), gkv=vec(g_kv_lat, l), g1=vec(g_mix_pre, l), g2=vec(g_mix_post, l),
            g3=vec(g_ffn_pre, l), g4=vec(g_ffn_post, l)))

    def prep_ins(p_all, w):
        return [R(p_all, Q_RANK, _cb(OFF_CQ, Q_RANK)), R(p_all, KV_RANK, _cb(OFF_CKV, KV_RANK)),
                R(p_all, LANE, _cb(OFF_KPE, LANE)), R(p_all, LANE, _cb(OFF_KROT, LANE)), Pm(w["gq"]), Pm(w["gkv"]),
                R(cosp, LANE), R(sinp, LANE)]

    def merge_ins(p_all, ys):
        return [R(p_all, GW, _cb(OFF_GATE + n * D_MODEL, GW), 1) for n in range(3)] + [R(yv, GW, 0, 1) for yv in ys]

    def b16(total, w, cstep=0):
        return OR(total, w, cstep, BF16)

    (hn,) = _rw(lambda a, g: (_rms(a, g),), [R(h, D_MODEL), Pm(wl[0]["g1"])], [b16(D_MODEL, D_MODEL)], name="rms_in")
    saved, full = [], []
    for l in range(DEPTH):
        w = wl[l]
        s = dict(h=h, hn=hn)
        stacked, wts = finish_gather(l, gathering, h if l == 0 else saved[-1]["f"])
        full.append(wts)
        if l + 1 < DEPTH:
            gathering = start_gather(l + 1, stacked)
        p_all = _mm(hn, wts["w_in"], deps=(gathering["token"],) if l + 1 < DEPTH else (), name="proj_in")
        cqn, ckvn, kper = _rw(_mla_prep, prep_ins(p_all, w),
                              [b16(Q_RANK, Q_RANK), b16(KV_RANK, KV_RANK), OR(LANE, LANE)], name="mla_prep")
        q = _mm(cqn, wts["w_uq"], name="proj_q")
        kv = _mm(ckvn, wts["w_ukv"], name="proj_kv")
        o_a = _mla_fwd(q, kv, kper, cosp, sinp)
        o_b = _conv_fwd(p_all, wts["conv_w"])
        cdec = _decay_fwd(p_all, w["b_pad"])
        cq3 = cdec[:, :HEADS].T[:, :, None]
        ck3 = cdec[:, :HEADS].T[:, None, :]
        o_c = _fox_fwd(p_all, cq3, ck3)
        outs = (o_a, o_b, o_c)
        ys = [_mm(outs[n], wts["w_branch"], bidx=(n,), name="proj_branch") for n in range(3)]
        (merged,) = _rw(_merge, merge_ins(p_all, ys), [b16(D_MODEL, GW, 1)], ncol=D_MODEL // GW, name="merge")
        mix = _mm(merged, wts["w_out"], name="proj_out")
        h2, hn2 = _rw(_resid_norm, [R(h, D_MODEL), R(mix, D_MODEL), Pm(w["g2"]), Pm(w["g3"])],
                      [OR(D_MODEL, D_MODEL), b16(D_MODEL, D_MODEL)], name="resid_norm")
        gu = _mm(hn2, wts["w_ffn_in"], name="ffn_in")
        (act,) = _rw(_swiglu, [R(gu, 2 * D_FF)], [b16(D_FF, D_FF)], tm=TM_FF, name="swiglu")
        f = _mm(act, wts["w_ffn_out"], name="ffn_out")
        s.update(p_all=p_all, cqn=cqn, ckvn=ckvn, kper=kper, q=q, kv=kv, outs=outs, cq3=cq3, ck3=ck3, ys=ys,
                 merged=merged, mix=mix, h2=h2, hn2=hn2, gu=gu, act=act, f=f)
        saved.append(s)
        if l + 1 < DEPTH:
            h, hn = _rw(_resid_norm, [R(h2, D_MODEL), R(f, D_MODEL), Pm(w["g4"]), Pm(wl[l + 1]["g1"])],
                        [OR(D_MODEL, D_MODEL), b16(D_MODEL, D_MODEL)], name="resid_norm")

    grads = {n: [None] * DEPTH for n in _SMALL + ("conv_w",)}
    mats = tuple(n for n, _, _ in _BIG if n not in ("meta", "conv_w"))
    mat_kinds = tuple(kd for n, kd, _ in _BIG if n in mats)
    via_rows = ("w_in", "w_uq")
    chip = (2 * lax.axis_index("x") + lax.axis_index("y")).astype(jnp.int32).reshape(1)
    sib_plan, chips_plan = _sibling_plan(mat_kinds), _chips_plan(len(mats))
    updates = {n: [lax.empty(_as3d(given[n]).shape, F32) for _ in range(4)] for n in mats if n not in via_rows}
    row_sums = {n: lax.empty((DEPTH, full[0][n].shape[0] // N_DEV, full[0][n].shape[1]), F32) for n in via_rows}
    landed = []
    flying = None

    def finish(layer, own, theirs, deps):
        for n, o, t in zip(mats, own, theirs):
            if n in via_rows:
                row_sums[n] = _sum_parts_layer(o, t, chip, row_sums[n], layer, "sum_" + n)
            else:
                updates[n] = _sum_adamw_layer(o, t, chip, _as3d(given[n]), _as3d(given["m_" + n]),
                                              _as3d(given["v_" + n]), updates[n], layer, deps, "adamw_" + n)

    def pair_sums(layer, srcs, from_sib):
        sums = [_pair_sum(_as3d(g), _as3d(r, 1), core, kd, "pair_sum_" + n)
                for n, kd, g, r in zip(mats, mat_kinds, srcs, from_sib)]
        return _split_start(chips_plan, 3 * len(mats), sums, [lax.empty((3,) + p.shape[1:], BF16) for p in sums],
                            "scatter_chips_start_%d" % layer)

    s, w = saved[-1], wl[-1]
    dh2, df, dg4, loss_acc = _rw(
        _loss_bwd, [R(s["h2"], D_MODEL), R(s["f"], D_MODEL), Pm(w["g4"]), R(ltp, D_MODEL), R(rmask, 1)],
        [OR(D_MODEL, D_MODEL), b16(D_MODEL, D_MODEL), OA((1, D_MODEL)), OA((1, LANE))], name="loss_bwd")
    loss = lax.psum(loss_acc[0, 0], ("x", "y", "c"))
    grads["g_ffn_post"][DEPTH - 1] = dg4[0]
    for l in reversed(range(DEPTH)):
        s, w = saved[l], wl[l]
        p_all = s["p_all"]
        gl = {}
        wts = full[l]
        dact = _mm(df, wts["w_ffn_out"], tb=True, deps=(flying["token"],) if flying else (), name="d_act")
        gl["w_ffn_out"] = _mm(s["act"], df, ta=True, out_dtype=BF16, name="dw_ffn_out")
        (dgu,) = _rw(_swiglu_bwd, [R(s["gu"], 2 * D_FF), R(dact, D_FF)], [b16(2 * D_FF, 2 * D_FF)], tm=TM_FF,
                     name="swiglu_bwd")
        if flying:
            srcs, from_sib = _split_wait(sib_plan, flying["send"], flying["recv"], flying["srcs"], flying["dsts"], dgu,
                                         "scatter_sibling_wait_%d" % (l + 1))
            send, recv, sums, slots, token = pair_sums(l + 1, srcs, from_sib)
            flying = dict(send=send, recv=recv, srcs=sums, dsts=slots, token=token)
        dhn2 = _mm(dgu, wts["w_ffn_in"], tb=True, deps=(flying["token"],) if flying else (), name="d_hn2")
        gl["w_ffn_in"] = _mm(s["hn2"], dgu, ta=True, out_dtype=BF16, name="dw_ffn_in")
        dh, dmix, dg2, dg3 = _rw(
            _resid_norm_bwd, [R(s["h"], D_MODEL), R(s["mix"], D_MODEL), Pm(w["g2"]), Pm(w["g3"]), R(dh2, D_MODEL),
                              R(dhn2, D_MODEL)],
            [OR(D_MODEL, D_MODEL), b16(D_MODEL, D_MODEL), OA((1, D_MODEL)), OA((1, D_MODEL))], name="resid_norm_bwd")
        grads["g_mix_post"][l], grads["g_ffn_pre"][l] = dg2[0], dg3[0]
        dmerged = _mm(dmix, wts["w_out"], tb=True, name="d_merged")
        gl["w_out"] = _mm(s["merged"], dmix, ta=True, out_dtype=BF16, name="dw_out")
        mb = _rw(_merge_bwd, merge_ins(p_all, s["ys"]) + [R(dmerged, GW, 0, 1)], [b16(D_MODEL, GW, 1)] * 6,
                 ncol=D_MODEL // GW, name="merge_bwd")
        dgate, dys = mb[:3], mb[3:]
        dos = [_mm(dys[n], wts["w_branch"], tb=True, bidx=(n,), name="d_branch") for n in range(3)]
        gl["w_branch"] = lax.empty(wts["w_branch"].shape, BF16)
        for n in range(3):
            gl["w_branch"] = _mm(s["outs"][n], dys[n], ta=True, stack=(gl["w_branch"], (n,)), name="dw_branch")
        dfq, dfk, dfv, dcq3, dck3 = _fox_bwd(p_all, s["cq3"], s["ck3"], dos[2])
        dc = jnp.concatenate([dcq3[:, :, 0].T + dck3[:, 0, :].T, jnp.zeros((LP, LANE - HEADS), F32)], axis=1)
        dfl, db = _decay_bwd(p_all, w["b_pad"], dc)
        grads["b_forget"][l] = db[0, :HEADS]
        dconv, dcw = _conv_bwd(p_all, wts["conv_w"], dos[1])
        grads["conv_w"][l] = dcw
        dq, dkv, dkper = _mla_bwd(s["q"], s["kv"], s["kper"], cosp, sinp, dos[0])
        dcqn = _mm(dq, wts["w_uq"], tb=True, name="d_cqn")
        gl["w_uq"] = _mm(s["cqn"], dq, ta=True, out_dtype=BF16, name="dw_uq")
        dckvn = _mm(dkv, wts["w_ukv"], tb=True, name="d_ckvn")
        gl["w_ukv"] = _mm(s["ckvn"], dkv, ta=True, out_dtype=BF16, name="dw_ukv")
        dcq, dckv, dkpe, dkrot, dgq, dgkv = _rw(
            _mla_prep_bwd, prep_ins(p_all, w) + [R(dcqn, Q_RANK), R(dckvn, KV_RANK), R(dkper, LANE)],
            [b16(Q_RANK, Q_RANK), b16(KV_RANK, KV_RANK), b16(LANE, LANE), b16(LANE, LANE), OA((1, Q_RANK)),
             OA((1, KV_RANK))], name="mla_prep_bwd")
        grads["g_q_lat"][l], grads["g_kv_lat"][l] = dgq[0], dgkv[0]
        dp = jnp.concatenate([*dgate, dconv[0], dconv[1], dconv[2], dfq, dfk.astype(BF16), dfv.astype(BF16), dcq, dckv,
                              dkpe, dkrot, dfl, jnp.zeros((LP, LANE), BF16)], axis=1)
        dhn = _mm(dp, wts["w_in"], tb=True, name="d_hn")
        gl["w_in"] = _mm(s["hn"], dp, ta=True, out_dtype=BF16, name="dw_in")
        if flying:
            sums, slots = _split_wait(chips_plan, flying["send"], flying["recv"], flying["srcs"], flying["dsts"], dhn,
                                      "scatter_chips_wait_%d" % (l + 1))
            landed.append((l + 1, sums, slots))
        parts_l = [gl[n] for n in mats]
        send, recv, srcs, dsts, token = _split_start(
            sib_plan, 4 * len(mats), parts_l,
            [lax.empty((4,) + _slice_shape(g.shape, kd), BF16) for g, kd in zip(parts_l, mat_kinds)],
            "scatter_sibling_start_%d" % l)
        flying = dict(send=send, recv=recv, srcs=srcs, dsts=dsts, token=token)
        if l > 0:
            sp, wp = saved[l - 1], wl[l - 1]
            dh2, df, dg4, dg1 = _rw(
                _resid_norm_bwd, [R(sp["h2"], D_MODEL), R(sp["f"], D_MODEL), Pm(wp["g4"]), Pm(w["g1"]),
                                  R(dh, D_MODEL), R(dhn, D_MODEL)],
                [OR(D_MODEL, D_MODEL), b16(D_MODEL, D_MODEL), OA((1, D_MODEL)), OA((1, D_MODEL))], name="resid_norm_bwd")
            grads["g_ffn_post"][l - 1], grads["g_mix_pre"][l] = dg4[0], dg1[0]
        else:
            dh0, dg1 = _rw(_rms_bwd, [R(s["h"], D_MODEL), Pm(w["g1"]), R(dhn, D_MODEL), R(dh, D_MODEL)],
                           [OR(D_MODEL, D_MODEL), OA((1, D_MODEL))], name="rms_in_bwd")
            grads["g_mix_pre"][0] = dg1[0]
    grad_x = dh0[N_META:L_TOK][None]
    gfull = {n: jnp.stack(grads[n]) for n in grads}
    gfull["meta"] = dh0[:N_META]

    srcs, from_sib = _split_wait(sib_plan, flying["send"], flying["recv"], flying["srcs"], flying["dsts"], dh0,
                                 "scatter_sibling_wait_0")
    send, recv, sums, slots, token = pair_sums(0, srcs, from_sib)
    for layer, own, theirs in landed:
        finish(layer, own, theirs, (token,))
    sums, slots = _split_wait(chips_plan, send, recv, sums, slots, updates[mats[-1]][0], "scatter_chips_wait_0")
    finish(0, sums, slots, ())

    few = ("meta", "conv_w")
    partial = [gfull[n] for n in few]
    from_sib = _swap_sibling(partial, ("col", "col"), "scatter_sibling_few")
    chip_sums = [_pair_sum(_as3d(g), _as3d(r, 1), core, "col", "pair_sum_" + n)
                 for n, g, r in zip(few, partial, from_sib)]
    parts = dict(zip(few, _swap_chips(chip_sums, "scatter_chips_few")))

    def by_dest(a):
        return jnp.moveaxis(a.reshape(a.shape[:-1] + (N_DEV, a.shape[-1] // N_DEV)), -2, 0).astype(BF16)

    cols_in, cols_uq = _all_to_all([by_dest(_unext_w_in(row_sums["w_in"])), by_dest(_unext_w_uq(row_sums["w_uq"]))],
                                   ("raw", "raw"), ("row", "row"), "columns_of_row_sums")
    parts["w_in"], parts["w_uq"] = _as3d(cols_in)[None], _as3d(cols_uq)[None]
    out = {n: [r.reshape(given[n].shape) for r in updates[n]] for n in updates}
    for n in parts:
        res = _sum_adamw(parts[n], _as3d(given[n]), _as3d(given["m_" + n]), _as3d(given["v_" + n]), "adamw_" + n)
        out[n] = [r.reshape(given[n].shape) for r in res]

    def pack(d):
        flat = jnp.concatenate([d[n].reshape(-1) for n in _SMALL])
        return jnp.concatenate([flat, jnp.zeros((-flat.shape[0]) % (8 * LANE), F32)]).reshape(-1, LANE)

    (small_parts,) = _all_gather([pack(gfull)], ("raw",), "gather_small_grads")
    res = _sum_adamw(small_parts[:, None], pack(given)[None], pack({n: given["m_" + n] for n in _SMALL})[None],
                     pack({n: given["v_" + n] for n in _SMALL})[None], "adamw_small")
    off = 0
    for n in _SMALL:
        size = int(np.prod(given[n].shape))
        out[n] = [r.reshape(-1)[off:off + size].reshape(given[n].shape) for r in res]
        off += size

    return (loss, grad_x, *[out[n][0] for n in _ORDER], *[out[n][1] for n in _ORDER], *[out[n][2] for n in _ORDER],
            *[out[n][3] for n in _ORDER])
```

```python
import functools

import numpy as np
import jax
import jax.numpy as jnp
from jax import lax
from jax.experimental import pallas as pl
from jax.experimental.pallas import tpu as pltpu

D_MODEL = 2048
SEQ = 2048
DEPTH = 4
Q_RANK = 512
KV_RANK = 512
D_FF = 5632
N_META = 16
HEADS = 8
HD = 128
ROPE = 64
BW = HEADS * HD
EPS = 1e-6
NEG_INF = -1e30
ROPE_THETA = 10000.0
N_DEV = 8
LANE = 128
L_TOK = N_META + SEQ
LP = -(-L_TOK // LANE) * LANE
D_IN = Q_RANK + KV_RANK + ROPE + 6 * BW + HEADS + 3 * D_MODEL
MLA_SCALE = (HD + ROPE) ** -0.5
FOX_SCALE = HD ** -0.5
ADAM_LR, ADAM_B1, ADAM_B2, ADAM_EPS, ADAM_WD, ADAM_STEP = 0.001, 0.9, 0.999, 1e-08, 0.01, 10
VMEM_LIMIT = 48 * 1024 * 1024

F32 = jnp.float32
BF16 = jnp.bfloat16
MESH = pl.DeviceIdType.MESH

OFF_GATE = 0
OFF_CONV = 3 * D_MODEL
OFF_FOX = OFF_CONV + 3 * BW
OFF_CQ = OFF_FOX + 3 * BW
OFF_CKV = OFF_CQ + Q_RANK
OFF_KPE = OFF_CKV + KV_RANK
OFF_KROT = OFF_KPE + LANE
OFF_FL = OFF_KROT + LANE
W_ALL = OFF_FL + 2 * LANE
GW = 512 if D_MODEL % 512 == 0 else 256
TM_FF = 64
CW = 128
QG = 3 * LANE


def _pick(n, prefs):
    for p in prefs:
        if n % p == 0:
            return p
    return n


TM_ROW = 128
TQ = _pick(LP, (272, 128))


def _cb(off, w):
    assert off % w == 0, (off, w)
    return off // w


def _pcall(body, **kw):
    return pl.pallas_call(body, **kw)


def _params(sem):
    return pltpu.CompilerParams(dimension_semantics=sem, vmem_limit_bytes=VMEM_LIMIT)


def _bf(x):
    return x.astype(BF16)


def _dot(a, b, dims, **kw):
    return lax.dot_general(_bf(a), _bf(b), (dims, ((), ())), preferred_element_type=F32, **kw)


_NN = ((1,), (0,))
_NT = ((1,), (1,))
_TN = ((0,), (0,))
_ANY = pl.BlockSpec(memory_space=pl.ANY)


MM_VMEM_BUDGET = 38 * 1024 * 1024
HBM_BYTES_PER_STEP = 1 << 20


def _divisors(n, prefs):
    return [p for p in prefs if n % p == 0] or [n]


def _mm_tiles(m, n, kd, a_bytes, b_bytes, o_bytes):
    best = None
    for tm in _divisors(m, (2176, 2048, 1088, 1024, 544, 512, 272, 256, 128)):
        for tn in _divisors(n, (2048, 1536, 1024, 768, 512, 384, 256, 128)):
            for tk in _divisors(kd, (2816, 2304, 2176, 2048, 1536, 1408, 1024, 768, 544, 512, 384, 272, 256, 128)):
                nk = kd // tk
                vmem = 2 * (tm * tk * a_bytes + tk * tn * b_bytes + tm * tn * o_bytes) + 2 * tm * tn * 4
                vmem += (tm * tk * 2 if a_bytes == 4 else 0) + (tk * tn * 2 if b_bytes == 4 else 0)
                if vmem > MM_VMEM_BUDGET:
                    continue
                steps = (m // tm) * (n // tn) * nk
                cost = (m * kd * a_bytes * (1 if nk == 1 else n // tn) + kd * n * b_bytes * (m // tm)
                        + steps * HBM_BYTES_PER_STEP)
                if best is None or cost < best[0]:
                    best = (cost, tm, tn, tk)
    assert best is not None, (m, n, kd)
    return best[1:]


def _mm(a, b, *, ta=False, tb=False, bidx=(), stack=None, out_dtype=F32, deps=(), name):
    if ta:
        kd, m = a.shape
    else:
        m, kd = a.shape
    nlead = len(bidx)
    if tb:
        n, kd2 = b.shape[nlead:]
    else:
        kd2, n = b.shape[nlead:]
    assert kd == kd2, (a.shape, b.shape, ta, tb)
    if stack is not None:
        out_dtype = stack[0].dtype
    tm, tn, tk = _mm_tiles(m, n, kd, a.dtype.itemsize, b.dtype.itemsize, jnp.dtype(out_dtype).itemsize)
    nk = kd // tk
    dims = _TN if ta else (_NT if tb else _NN)

    def body(a_ref, b_ref, *rest):
        o_ref, acc_ref = rest[-2:]
        if nk == 1:
            o_ref[...] = _dot(a_ref[...], b_ref[...], dims).astype(o_ref.dtype)
            return
        k = pl.program_id(2)

        @pl.when(k == 0)
        def _():
            acc_ref[...] = jnp.zeros_like(acc_ref)

        acc_ref[...] += _dot(a_ref[...], b_ref[...], dims)

        @pl.when(k == nk - 1)
        def _():
            o_ref[...] = acc_ref[...].astype(o_ref.dtype)

    lead = (None,) * nlead
    a_spec = pl.BlockSpec((tk, tm), lambda i, j, k: (k, i)) if ta else pl.BlockSpec((tm, tk), lambda i, j, k: (i, k))
    if tb:
        b_spec = pl.BlockSpec(lead + (tn, tk), lambda i, j, k: bidx + (j, k))
    else:
        b_spec = pl.BlockSpec(lead + (tk, tn), lambda i, j, k: bidx + (k, j))
    in_specs, args, extra = [a_spec, b_spec], [a, b], {}
    if stack is None:
        out_spec = pl.BlockSpec((tm, tn), lambda i, j, k: (i, j))
        out_shape = jax.ShapeDtypeStruct((m, n), out_dtype)
    else:
        buf, sidx = stack
        assert buf.shape[len(sidx):] == (m, n), (buf.shape, sidx, m, n)
        in_specs.append(_ANY)
        args.append(buf)
        extra = dict(input_output_aliases={2: 0})
        out_spec = pl.BlockSpec((None,) * len(sidx) + (tm, tn), lambda i, j, k: sidx + (i, j))
        out_shape = jax.ShapeDtypeStruct(buf.shape, buf.dtype)
    in_specs += [_ANY] * len(deps)
    args += list(deps)
    return _pcall(
        body, name=name, grid=(m // tm, n // tn, nk), in_specs=in_specs, out_specs=out_spec, out_shape=out_shape,
        scratch_shapes=[pltpu.VMEM((tm, tn) if nk > 1 else (8, LANE), F32)],
        compiler_params=_params(("parallel", "parallel", "arbitrary")), **extra)(*args)


class R:
    def __init__(self, arr, w, cb0=0, cstep=0):
        self.arr, self.w, self.cb0, self.cstep = arr, w, cb0, cstep


class Pm:
    def __init__(self, arr):
        self.arr = arr


class OR:
    def __init__(self, total, w, cstep=0, dtype=F32):
        self.total, self.w, self.cstep, self.dtype = total, w, cstep, dtype


class OA:
    def __init__(self, shape):
        self.shape = shape


def _rw(fn, ins, outs, *, name, ncol=1, tm=None):
    tm = TM_ROW if tm is None else tm
    nrow = LP // tm
    n_in = len(ins)

    def body(*refs):
        j, i = pl.program_id(0), pl.program_id(1)
        res = fn(*[r[...] for r in refs[:n_in]])
        for o, ref, val in zip(outs, refs[n_in:], res):
            if isinstance(o, OR):
                ref[...] = val.astype(ref.dtype)
            else:
                @pl.when((i == 0) & (j == 0))
                def _(ref=ref):
                    ref[...] = jnp.zeros_like(ref)

                ref[...] += val

    in_specs = []
    for s in ins:
        if isinstance(s, R):
            in_specs.append(pl.BlockSpec((tm, s.w), lambda j, i, s=s: (i, s.cb0 + j * s.cstep)))
        else:
            in_specs.append(pl.BlockSpec(s.arr.shape, lambda j, i, nd=s.arr.ndim: (0,) * nd))
    out_specs, out_shape = [], []
    for o in outs:
        if isinstance(o, OR):
            out_specs.append(pl.BlockSpec((tm, o.w), lambda j, i, o=o: (i, j * o.cstep)))
            out_shape.append(jax.ShapeDtypeStruct((LP, o.total), o.dtype))
        else:
            out_specs.append(pl.BlockSpec(o.shape, lambda j, i: (0, 0)))
            out_shape.append(jax.ShapeDtypeStruct(o.shape, F32))
    return _pcall(body, name=name, grid=(ncol, nrow), in_specs=in_specs, out_specs=out_specs, out_shape=out_shape,
                  compiler_params=_params(("arbitrary", "arbitrary")))(*[s.arr for s in ins])


def _rms(x, g):
    return x * lax.rsqrt(jnp.mean(x * x, axis=-1, keepdims=True) + EPS) * g


def _resid_norm(h, z, ga, gb):
    h2 = h + _rms(z, ga)
    return h2, _rms(h2, gb)


def _resid_norm_bwd(h, z, ga, gb, dh2, dhn2):
    _, vjp = jax.vjp(_resid_norm, h, z, ga, gb)
    return vjp((dh2, dhn2))


def _rms_bwd(h, g, dhn, dh_in):
    _, vjp = jax.vjp(_rms, h, g)
    dh, dg = vjp(dhn)
    return dh + dh_in, dg


def _loss_fn(h2, f, g4, lt, rmask):
    h3 = h2 + _rms(f, g4)
    err = jnp.square(h3 - lt)
    return 0.5 * jnp.sum(jnp.mean(err, axis=-1, keepdims=True) * rmask)


def _loss_bwd(h2, f, g4, lt, rmask):
    val, (dh2, df, dg4) = jax.value_and_grad(_loss_fn, argnums=(0, 1, 2))(h2, f, g4, lt, rmask)
    return dh2, df, dg4, jnp.broadcast_to(val, (1, LANE))


def _mla_prep(cq, ckv, kpe, krot, gq, gkv, cosp, sinp):
    return _rms(cq, gq), _rms(ckv, gkv), kpe * cosp + krot * sinp


def _mla_prep_bwd(cq, ckv, kpe, krot, gq, gkv, cosp, sinp, dcqn, dckvn, dkper):
    _, vjp = jax.vjp(lambda a, b, c, d, e, f: _mla_prep(a, b, c, d, e, f, cosp, sinp), cq, ckv, kpe, krot, gq, gkv)
    return vjp((dcqn, dckvn, dkper))


def _merge(g0, g1, g2, y0, y1, y2):
    return (jax.nn.sigmoid(g0) * y0 + jax.nn.sigmoid(g1) * y1 + jax.nn.sigmoid(g2) * y2,)


def _merge_bwd(g0, g1, g2, y0, y1, y2, dm):
    _, vjp = jax.vjp(_merge, g0, g1, g2, y0, y1, y2)
    return vjp((dm,))


def _swiglu(gu):
    g, u = gu[:, :D_FF], gu[:, D_FF:]
    return (g * jax.nn.sigmoid(g) * u,)


def _swiglu_bwd(gu, dact):
    _, vjp = jax.vjp(_swiglu, gu)
    return vjp((dact,))


def _causal_probs(s, iq):
    qpos = iq * TQ + lax.broadcasted_iota(jnp.int32, s.shape, 0)
    kpos = lax.broadcasted_iota(jnp.int32, s.shape, 1)
    s = jnp.where(kpos <= qpos, s, NEG_INF)
    e = jnp.exp(s - jnp.max(s, axis=-1, keepdims=True))
    return e / jnp.sum(e, axis=-1, keepdims=True)


def _softmax_vjp(p, dp):
    return p * (dp - jnp.sum(p * dp, axis=-1, keepdims=True))


def _mla_specs():
    return [pl.BlockSpec((TQ, QG), lambda h, i: (i, h)),
            pl.BlockSpec((LP, 2 * HD), lambda h, i: (0, h)),
            pl.BlockSpec((LP, LANE), lambda h, i: (0, 0)),
            pl.BlockSpec((TQ, LANE), lambda h, i: (i, 0)),
            pl.BlockSpec((TQ, LANE), lambda h, i: (i, 0))]


def _mla_parts(q_ref, kv_ref, cos_ref, sin_ref):
    q, kv = q_ref[...], kv_ref[...]
    qn = q[:, :HD]
    qp = q[:, HD:2 * HD] * cos_ref[...] + q[:, 2 * HD:] * sin_ref[...]
    return qn, qp, kv[:, :HD], kv[:, HD:]


def _mla_fwd(q, kv, kper, cosp, sinp):
    def body(q_ref, kv_ref, kp_ref, cos_ref, sin_ref, o_ref):
        qn, qp, kn, v = _mla_parts(q_ref, kv_ref, cos_ref, sin_ref)
        s = (_dot(qn, kn, _NT) + _dot(qp, kp_ref[...], _NT)) * MLA_SCALE
        o_ref[...] = _dot(_causal_probs(s, pl.program_id(1)), v, _NN).astype(o_ref.dtype)

    return _pcall(body, name="mla_fwd", grid=(HEADS, LP // TQ), in_specs=_mla_specs(),
                  out_specs=pl.BlockSpec((TQ, HD), lambda h, i: (i, h)),
                  out_shape=jax.ShapeDtypeStruct((LP, BW), BF16),
                  compiler_params=_params(("parallel", "parallel")))(q, kv, kper, cosp, sinp)


def _mla_bwd(q, kv, kper, cosp, sinp, do):
    def body(q_ref, kv_ref, kp_ref, cos_ref, sin_ref, do_ref, dq_ref, dkv_ref, dkp_ref):
        h, iq = pl.program_id(0), pl.program_id(1)
        qn, qp, kn, v = _mla_parts(q_ref, kv_ref, cos_ref, sin_ref)
        kp, dout = kp_ref[...], do_ref[...]
        p = _causal_probs((_dot(qn, kn, _NT) + _dot(qp, kp, _NT)) * MLA_SCALE, iq)
        ds = _softmax_vjp(p, _dot(dout, v, _NT)) * MLA_SCALE
        dqp = _dot(ds, kp, _NN)
        dq = jnp.concatenate([_dot(ds, kn, _NN), dqp * cos_ref[...], dqp * sin_ref[...]], axis=1)
        dq_ref[...] = dq.astype(dq_ref.dtype)

        @pl.when(iq == 0)
        def _():
            dkv_ref[...] = jnp.zeros_like(dkv_ref)

        dkv_ref[...] += jnp.concatenate([_dot(ds, qn, _TN), _dot(p, dout, _TN)], axis=1)

        @pl.when((iq == 0) & (h == 0))
        def _():
            dkp_ref[...] = jnp.zeros_like(dkp_ref)

        dkp_ref[...] += _dot(ds, qp, _TN)

    return _pcall(
        body, name="mla_bwd", grid=(HEADS, LP // TQ),
        in_specs=_mla_specs() + [pl.BlockSpec((TQ, HD), lambda h, i: (i, h))],
        out_specs=[pl.BlockSpec((TQ, QG), lambda h, i: (i, h)), pl.BlockSpec((LP, 2 * HD), lambda h, i: (0, h)),
                   pl.BlockSpec((LP, LANE), lambda h, i: (0, 0))],
        out_shape=[jax.ShapeDtypeStruct((LP, HEADS * QG), BF16), jax.ShapeDtypeStruct((LP, 2 * BW), F32),
                   jax.ShapeDtypeStruct((LP, LANE), F32)],
        compiler_params=_params(("arbitrary", "arbitrary")))(q, kv, kper, cosp, sinp, do)


def _fox_specs():
    cq, ck, cv = _cb(OFF_FOX, HD), _cb(OFF_FOX + BW, HD), _cb(OFF_FOX + 2 * BW, HD)
    return [pl.BlockSpec((TQ, HD), lambda h, i: (i, cq + h)),
            pl.BlockSpec((LP, HD), lambda h, i: (0, ck + h)),
            pl.BlockSpec((LP, HD), lambda h, i: (0, cv + h)),
            pl.BlockSpec((1, TQ, 1), lambda h, i: (h, i, 0)),
            pl.BlockSpec((1, 1, LP), lambda h, i: (h, 0, 0))]


def _fox_probs(q_ref, k_ref, cq_ref, ck_ref, iq):
    s = _dot(q_ref[...], k_ref[...], _NT) * FOX_SCALE + (cq_ref[0] - ck_ref[0])
    return _causal_probs(s, iq)


def _fox_fwd(p_all, cq3, ck3):
    def body(q_ref, k_ref, v_ref, cq_ref, ck_ref, o_ref):
        p = _fox_probs(q_ref, k_ref, cq_ref, ck_ref, pl.program_id(1))
        o_ref[...] = _dot(p, v_ref[...], _NN).astype(o_ref.dtype)

    return _pcall(body, name="fox_fwd", grid=(HEADS, LP // TQ), in_specs=_fox_specs(),
                  out_specs=pl.BlockSpec((TQ, HD), lambda h, i: (i, h)),
                  out_shape=jax.ShapeDtypeStruct((LP, BW), BF16),
                  compiler_params=_params(("parallel", "parallel")))(p_all, p_all, p_all, cq3, ck3)


def _fox_bwd(p_all, cq3, ck3, do):
    def body(q_ref, k_ref, v_ref, cq_ref, ck_ref, do_ref, dq_ref, dk_ref, dv_ref, dcq_ref, dck_ref):
        iq = pl.program_id(1)
        p = _fox_probs(q_ref, k_ref, cq_ref, ck_ref, iq)
        dout = do_ref[...]
        ds = _softmax_vjp(p, _dot(dout, v_ref[...], _NT))
        dss = ds * FOX_SCALE
        dq_ref[...] = _dot(dss, k_ref[...], _NN).astype(dq_ref.dtype)
        dcq_ref[0] = jnp.sum(ds, axis=1, keepdims=True)

        @pl.when(iq == 0)
        def _():
            dk_ref[...] = jnp.zeros_like(dk_ref)
            dv_ref[...] = jnp.zeros_like(dv_ref)
            dck_ref[...] = jnp.zeros_like(dck_ref)

        dk_ref[...] += _dot(dss, q_ref[...], _TN)
        dv_ref[...] += _dot(p, dout, _TN)
        dck_ref[0] -= jnp.sum(ds, axis=0, keepdims=True)

    head_rows = pl.BlockSpec((TQ, HD), lambda h, i: (i, h))
    head_all = pl.BlockSpec((LP, HD), lambda h, i: (0, h))
    return _pcall(
        body, name="fox_bwd", grid=(HEADS, LP // TQ), in_specs=_fox_specs() + [head_rows],
        out_specs=[head_rows, head_all, head_all, pl.BlockSpec((1, TQ, 1), lambda h, i: (h, i, 0)),
                   pl.BlockSpec((1, 1, LP), lambda h, i: (h, 0, 0))],
        out_shape=[jax.ShapeDtypeStruct((LP, BW), BF16)] + [jax.ShapeDtypeStruct((LP, BW), F32)] * 2
        + [jax.ShapeDtypeStruct((HEADS, LP, 1), F32), jax.ShapeDtypeStruct((HEADS, 1, LP), F32)],
        compiler_params=_params(("arbitrary", "arbitrary")))(p_all, p_all, p_all, cq3, ck3, do)


def _log_sigmoid(x):
    return jnp.minimum(x, 0.0) - jnp.log(1.0 + jnp.exp(-jnp.abs(x)))


def _decay_fwd(p_all, b_pad):
    tc = TM_ROW

    def body(fl_ref, b_ref, c_ref):
        lf = _log_sigmoid(fl_ref[...] + b_ref[...])
        r = pl.program_id(0) * tc + lax.broadcasted_iota(jnp.int32, (tc, LP), 0)
        s = lax.broadcasted_iota(jnp.int32, (tc, LP), 1)
        c_ref[...] = jnp.dot((s <= r).astype(F32), lf, precision=lax.Precision.HIGHEST, preferred_element_type=F32)

    return _pcall(body, name="decay_fwd", grid=(LP // tc,),
                  in_specs=[pl.BlockSpec((LP, LANE), lambda i: (0, _cb(OFF_FL, LANE))),
                            pl.BlockSpec((1, LANE), lambda i: (0, 0))],
                  out_specs=pl.BlockSpec((tc, LANE), lambda i: (i, 0)),
                  out_shape=jax.ShapeDtypeStruct((LP, LANE), F32), compiler_params=_params(("parallel",)))(p_all, b_pad)


def _decay_bwd(p_all, b_pad, dc):
    tc = TM_ROW

    def body(fl_ref, b_ref, dc_ref, dfl_ref, db_ref):
        i = pl.program_id(0)
        r = i * tc + lax.broadcasted_iota(jnp.int32, (tc, LP), 0)
        t = lax.broadcasted_iota(jnp.int32, (tc, LP), 1)
        dlf = jnp.dot((t >= r).astype(F32), dc_ref[...], precision=lax.Precision.HIGHEST, preferred_element_type=F32)
        dfl = dlf * jax.nn.sigmoid(-(fl_ref[...] + b_ref[...]))
        dfl_ref[...] = dfl.astype(dfl_ref.dtype)

        @pl.when(i == 0)
        def _():
            db_ref[...] = jnp.zeros_like(db_ref)

        db_ref[...] += jnp.sum(dfl, axis=0, keepdims=True)

    return _pcall(body, name="decay_bwd", grid=(LP // tc,),
                  in_specs=[pl.BlockSpec((tc, LANE), lambda i: (i, _cb(OFF_FL, LANE))),
                            pl.BlockSpec((1, LANE), lambda i: (0, 0)), pl.BlockSpec((LP, LANE), lambda i: (0, 0))],
                  out_specs=[pl.BlockSpec((tc, LANE), lambda i: (i, 0)), pl.BlockSpec((1, LANE), lambda i: (0, 0))],
                  out_shape=[jax.ShapeDtypeStruct((LP, LANE), BF16), jax.ShapeDtypeStruct((1, LANE), F32)],
                  compiler_params=_params(("arbitrary",)))(p_all, b_pad, dc)


def _conv_specs():
    c0 = _cb(OFF_CONV, CW)
    step = BW // CW
    return [pl.BlockSpec((LP, CW), lambda j: (0, c0 + j)), pl.BlockSpec((LP, CW), lambda j: (0, c0 + step + j)),
            pl.BlockSpec((LP, CW), lambda j: (0, c0 + 2 * step + j)),
            pl.BlockSpec((3, CW), lambda j: (0, j))]


def _shift_down(x, k, t):
    return jnp.where(t >= k, pltpu.roll(x, k, 0), 0.0)


def _shift_up(x, k, t):
    return jnp.where(t < LP - k, pltpu.roll(x, LP - k, 0), 0.0)


def _conv_fwd(p_all, cw):
    def body(b_ref, c_ref, x_ref, w_ref, o_ref):
        t = lax.broadcasted_iota(jnp.int32, (LP, CW), 0)
        uu = c_ref[...] * x_ref[...]
        w = w_ref[...]
        u = w[2:3] * uu + w[1:2] * _shift_down(uu, 1, t) + w[0:1] * _shift_down(uu, 2, t)
        o_ref[...] = (b_ref[...] * u).astype(o_ref.dtype)

    return _pcall(body, name="conv_fwd", grid=(BW // CW,), in_specs=_conv_specs(),
                  out_specs=pl.BlockSpec((LP, CW), lambda j: (0, j)), out_shape=jax.ShapeDtypeStruct((LP, BW), BF16),
                  compiler_params=_params(("parallel",)))(p_all, p_all, p_all, cw)


def _conv_bwd(p_all, cw, do):
    def body(b_ref, c_ref, x_ref, w_ref, do_ref, d_ref, dw_ref):
        t = lax.broadcasted_iota(jnp.int32, (LP, CW), 0)
        cc, xx, w, dout = c_ref[...], x_ref[...], w_ref[...], do_ref[...]
        uu = cc * xx
        s1, s2 = _shift_down(uu, 1, t), _shift_down(uu, 2, t)
        u = w[2:3] * uu + w[1:2] * s1 + w[0:1] * s2
        du = dout * b_ref[...]
        duu = w[2:3] * du + w[1:2] * _shift_up(du, 1, t) + w[0:1] * _shift_up(du, 2, t)
        d_ref[0] = (dout * u).astype(d_ref.dtype)
        d_ref[1] = (duu * xx).astype(d_ref.dtype)
        d_ref[2] = (duu * cc).astype(d_ref.dtype)
        dw_ref[0:1, :] = jnp.sum(du * s2, axis=0, keepdims=True)
        dw_ref[1:2, :] = jnp.sum(du * s1, axis=0, keepdims=True)
        dw_ref[2:3, :] = jnp.sum(du * uu, axis=0, keepdims=True)

    return _pcall(body, name="conv_bwd", grid=(BW // CW,),
                  in_specs=_conv_specs() + [pl.BlockSpec((LP, CW), lambda j: (0, j))],
                  out_specs=[pl.BlockSpec((3, LP, CW), lambda j: (0, 0, j)), pl.BlockSpec((3, CW), lambda j: (0, j))],
                  out_shape=[jax.ShapeDtypeStruct((3, LP, BW), BF16), jax.ShapeDtypeStruct((3, BW), F32)],
                  compiler_params=_params(("parallel",)))(p_all, p_all, p_all, cw, do)


def _place():
    return lax.axis_index("x"), lax.axis_index("y"), lax.axis_index("c")


def _dest_slice(ref, kind, d):
    if kind == "raw":
        return ref.at[d]
    nd = len(ref.shape)
    if kind == "col":
        w = ref.shape[-1] // N_DEV
        return ref.at[(slice(None),) * (nd - 1) + (pl.ds(d * w, w),)]
    r = ref.shape[-2] // N_DEV
    return ref.at[(slice(None),) * (nd - 2) + (pl.ds(d * r, r), slice(None))]


def _unsharded_shape(shape, kind, lead=N_DEV):
    if kind == "raw":
        return (lead,) + shape
    if kind == "col":
        return shape[:-1] + (N_DEV * shape[-1],)
    return shape[:-2] + (N_DEV * shape[-2], shape[-1])


def _all_gather(arrs, kinds, name):
    n = len(arrs)

    def body(*refs):
        ins, outs = refs[:n], refs[n:2 * n]
        send, recv, loc = refs[2 * n:]
        x, y, c = _place()
        sib = (x, y, 1 - c)
        chips = [(1 - x, y), (x, 1 - y), (1 - x, 1 - y)]

        def idx(px, py, pc):
            return 4 * px + 2 * py + pc

        def copy(j, k, block, to, src=None):
            dst = _dest_slice(outs[k], kinds[k], idx(*block))
            return pltpu.make_async_remote_copy(src_ref=dst if src is None else src, dst_ref=dst, send_sem=send.at[j, k],
                                                recv_sem=recv.at[j, k], device_id=to, device_id_type=MESH)

        me = (x, y, c)
        mine = [pltpu.make_async_copy(ins[k], _dest_slice(outs[k], kinds[k], idx(*me)), loc.at[k]) for k in range(n)]
        for cp in mine:
            cp.start()
        first = [copy(0, k, me, sib, src=ins[k]) for k in range(n)]
        first += [copy(1 + j, k, me, (*chip, c), src=ins[k]) for j, chip in enumerate(chips) for k in range(n)]
        for cp in first:
            cp.start()
        passed = []
        for j, chip in enumerate(chips):
            for k in range(n):
                copy(1 + j, k, (*chip, c), me, src=ins[k]).wait_recv()
                cp = copy(4 + j, k, (*chip, c), sib)
                cp.start()
                passed.append(cp)
        for k in range(n):
            copy(0, k, sib, me, src=ins[k]).wait_recv()
        for j, chip in enumerate(chips):
            for k in range(n):
                copy(4 + j, k, (*chip, 1 - c), me).wait_recv()
        for cp in first + passed:
            cp.wait_send()
        for cp in mine:
            cp.wait()

    return _pcall(body, name=name, in_specs=[_ANY] * n, out_specs=[_ANY] * n,
                  out_shape=[jax.ShapeDtypeStruct(_unsharded_shape(a.shape, kd), a.dtype) for a, kd in zip(arrs, kinds)],
                  scratch_shapes=[pltpu.SemaphoreType.DMA((7, n)), pltpu.SemaphoreType.DMA((7, n)),
                                  pltpu.SemaphoreType.DMA((n,))])(*arrs)


def _shard_shape(a, kind):
    if kind == "raw":
        return a.shape[2:]
    if kind == "col":
        return a.shape[:-1] + (a.shape[-1] // N_DEV,)
    return a.shape[:-2] + (a.shape[-2] // N_DEV, a.shape[-1])


def _swap_sibling(arrs, kinds, name):
    n = len(arrs)
    npieces = sum(1 if kd == "raw" else 4 for kd in kinds)

    def body(*refs):
        ins, outs = refs[:n], refs[n:2 * n]
        send, recv = refs[2 * n:]
        x, y, c = _place()
        pieces = []
        for k in range(n):
            if kinds[k] == "raw":
                pieces.append((ins[k].at[1 - c], outs[k]))
            else:
                pieces += [(_dest_slice(ins[k], kinds[k], 2 * p + 1 - c), outs[k].at[p]) for p in range(4)]
        cps = [pltpu.make_async_remote_copy(src_ref=src, dst_ref=dst, send_sem=send.at[i], recv_sem=recv.at[i],
                                            device_id=(x, y, 1 - c), device_id_type=MESH)
               for i, (src, dst) in enumerate(pieces)]
        for cp in cps:
            cp.start()
        for cp in cps:
            cp.wait()

    return _pcall(body, name=name, in_specs=[_ANY] * n, out_specs=[_ANY] * n,
                  out_shape=[jax.ShapeDtypeStruct((4,) + _shard_shape(a, kd), a.dtype) for a, kd in zip(arrs, kinds)],
                  scratch_shapes=[pltpu.SemaphoreType.DMA((npieces,)), pltpu.SemaphoreType.DMA((npieces,))])(*arrs)


def _swap_chips(arrs, name):
    n = len(arrs)

    def body(*refs):
        ins, outs = refs[:n], refs[n:2 * n]
        send, recv, loc = refs[2 * n:]
        x, y, c = _place()
        mychip = 2 * x + y
        mine = [pltpu.make_async_copy(ins[k].at[mychip], outs[k].at[mychip], loc.at[k]) for k in range(n)]
        for cp in mine:
            cp.start()
        cps = []
        for j, (cx, cy) in enumerate([(1 - x, y), (x, 1 - y), (1 - x, 1 - y)]):
            for k in range(n):
                cps.append(pltpu.make_async_remote_copy(
                    src_ref=ins[k].at[2 * cx + cy], dst_ref=outs[k].at[mychip], send_sem=send.at[j, k],
                    recv_sem=recv.at[j, k], device_id=(cx, cy, c), device_id_type=MESH))
        for cp in cps:
            cp.start()
        for cp in cps:
            cp.wait()
        for cp in mine:
            cp.wait()

    return _pcall(body, name=name, in_specs=[_ANY] * n, out_specs=[_ANY] * n,
                  out_shape=[jax.ShapeDtypeStruct(a.shape, a.dtype) for a in arrs],
                  scratch_shapes=[pltpu.SemaphoreType.DMA((3, n)), pltpu.SemaphoreType.DMA((3, n)),
                                  pltpu.SemaphoreType.DMA((n,))])(*arrs)


_HBM = pl.BlockSpec(memory_space=pltpu.HBM)
_SEM = pl.BlockSpec(memory_space=pltpu.SEMAPHORE)
_SIDE_EFFECT = pltpu.SideEffectType.DATAFLOW_SIDE_EFFECTING


def _split_copies(plan, refs, ns, nd):
    send, recv = refs[ns + nd], refs[ns + nd + 1]
    return [pltpu.make_async_remote_copy(src_ref=src, dst_ref=dst, send_sem=send.at[i], recv_sem=recv.at[i],
                                         device_id=dev, device_id_type=MESH)
            for i, (src, dst, dev) in enumerate(plan(refs[:ns], refs[ns:ns + nd]))]


def _split_start(plan, ncopies, srcs, dsts, name):
    ns, nd = len(srcs), len(dsts)

    def body(*refs):
        copies = _split_copies(plan, refs, ns, nd)
        assert len(copies) == ncopies
        for cp in copies:
            cp.start()
        refs[-1][...] = jnp.zeros_like(refs[-1])

    bufs = [pltpu.with_memory_space_constraint(a, pltpu.HBM) for a in list(srcs) + list(dsts)]
    res = _pcall(
        body, name=name, in_specs=[_HBM] * (ns + nd),
        out_specs=[_SEM, _SEM] + [_HBM] * (ns + nd) + [pl.BlockSpec(memory_space=pltpu.VMEM)],
        out_shape=[pltpu.SemaphoreType.DMA((ncopies,)), pltpu.SemaphoreType.DMA((ncopies,))]
        + [pltpu.HBM(a.shape, a.dtype) for a in bufs] + [jax.ShapeDtypeStruct((8, LANE), F32)],
        input_output_aliases={i: 2 + i for i in range(ns + nd)},
        compiler_params=pltpu.CompilerParams(has_side_effects=_SIDE_EFFECT))(*bufs)
    return res[0], res[1], list(res[2:2 + ns]), list(res[2 + ns:2 + ns + nd]), res[-1]


def _split_wait(plan, send, recv, srcs, dsts, after, name):
    ns, nd = len(srcs), len(dsts)

    def body(*refs):
        for cp in _split_copies(plan, refs, ns, nd):
            cp.wait_send()
            cp.wait_recv()

    res = _pcall(
        body, name=name, in_specs=[_HBM] * (ns + nd) + [_SEM, _SEM, _ANY], out_specs=[_HBM] * (ns + nd),
        out_shape=[pltpu.HBM(a.shape, a.dtype) for a in list(srcs) + list(dsts)],
        input_output_aliases={i: i for i in range(ns + nd)},
        compiler_params=pltpu.CompilerParams(has_side_effects=_SIDE_EFFECT))(*srcs, *dsts, send, recv, after)
    return list(res[:ns]), list(res[ns:])


def _sibling_plan(kinds):
    def plan(srcs, dsts):
        x, y, c = _place()
        return [(_dest_slice(srcs[k], kinds[k], 2 * p + 1 - c), dsts[k].at[p], (x, y, 1 - c))
                for k in range(len(kinds)) for p in range(4)]
    return plan


def _chips_plan(n):
    def plan(srcs, dsts):
        x, y, c = _place()
        return [(srcs[k].at[2 * cx + cy], dsts[k].at[j], (cx, cy, c))
                for j, (cx, cy) in enumerate([(1 - x, y), (x, 1 - y), (1 - x, 1 - y)]) for k in range(n)]
    return plan


def _gather_plan(kinds, layer):
    def plan(srcs, dsts):
        x, y, c = _place()
        me = 4 * x + 2 * y + c
        return [(srcs[k].at[layer], _dest_slice(dsts[k], kinds[k], me), dev)
                for k in range(len(kinds)) for dev in [(x, y, 1 - c), (1 - x, y, c), (x, 1 - y, c), (1 - x, 1 - y, c)]]
    return plan


def _pass_on_plan(kinds):
    def plan(srcs, dsts):
        x, y, c = _place()
        parts = [_dest_slice(dsts[k], kinds[k], 4 * px + 2 * py + pc) for k in range(len(kinds))
                 for px, py, pc in [(1 - x, y, c), (x, 1 - y, c), (1 - x, 1 - y, c), (x, y, 1 - c)]]
        return [(part, part, (x, y, 1 - c)) for part in parts]
    return plan


def _slice_shape(shape, kind):
    if kind == "raw":
        return shape[1:]
    if kind == "col":
        return shape[:-1] + (shape[-1] // N_DEV,)
    return shape[:-2] + (shape[-2] // N_DEV, shape[-1])


def _all_to_all(arrs, src_kinds, dst_kinds, name):
    n = len(arrs)

    def body(*refs):
        ins, outs = refs[:n], refs[n:2 * n]
        send, recv, loc = refs[2 * n:]
        x, y, c = _place()
        me = 4 * x + 2 * y + c
        mine = [pltpu.make_async_copy(_dest_slice(ins[k], src_kinds[k], me), _dest_slice(outs[k], dst_kinds[k], me),
                                      loc.at[k]) for k in range(n)]
        for cp in mine:
            cp.start()
        cps = []
        for r in range(1, N_DEV):
            px, py, pc = (1 - x if r & 4 else x), (1 - y if r & 2 else y), (1 - c if r & 1 else c)
            for k in range(n):
                cps.append(pltpu.make_async_remote_copy(
                    src_ref=_dest_slice(ins[k], src_kinds[k], 4 * px + 2 * py + pc),
                    dst_ref=_dest_slice(outs[k], dst_kinds[k], me), send_sem=send.at[r - 1, k],
                    recv_sem=recv.at[r - 1, k], device_id=(px, py, pc), device_id_type=MESH))
        for cp in cps:
            cp.start()
        for cp in cps:
            cp.wait()
        for cp in mine:
            cp.wait()

    out_shape = [jax.ShapeDtypeStruct(_unsharded_shape(_slice_shape(a.shape, sk), dk), a.dtype)
                 for a, sk, dk in zip(arrs, src_kinds, dst_kinds)]
    return _pcall(body, name=name, in_specs=[_ANY] * n, out_specs=[_ANY] * n, out_shape=out_shape,
                  scratch_shapes=[pltpu.SemaphoreType.DMA((N_DEV - 1, n)), pltpu.SemaphoreType.DMA((N_DEV - 1, n)),
                                  pltpu.SemaphoreType.DMA((n,))])(*arrs)


def _rows_tile(rows, cols=0):
    cap = 512 * 1024
    return _pick(rows, [t for t in (128, 64, 32, 16) if t * cols <= cap])


def _pair_sum(g, r1, core, kind, name):
    _, groups, rows, cols = r1.shape
    tr = _rows_tile(rows, cols)
    nr = rows // tr
    if kind == "raw":
        g_spec = pl.BlockSpec((None, None, None, tr, cols), lambda p, q, i, c: (c[0], p, q, i, 0))
    elif kind == "col":
        g_spec = pl.BlockSpec((None, tr, cols), lambda p, q, i, c: (q, i, 2 * p + c[0]))
    else:
        g_spec = pl.BlockSpec((None, tr, cols), lambda p, q, i, c: (q, (2 * p + c[0]) * nr + i, 0))
    r_spec = pl.BlockSpec((None, None, tr, cols), lambda p, q, i, c: (p, q, i, 0))

    def body(c_ref, g_ref, r_ref, o_ref):
        o_ref[...] = (g_ref[...].astype(F32) + r_ref[...].astype(F32)).astype(o_ref.dtype)

    return _pcall(
        body, name=name,
        grid_spec=pltpu.PrefetchScalarGridSpec(num_scalar_prefetch=1, grid=(4, groups, nr), in_specs=[g_spec, r_spec],
                                               out_specs=r_spec),
        out_shape=jax.ShapeDtypeStruct(r1.shape, r1.dtype),
        compiler_params=_params(("parallel", "parallel", "parallel")))(core, g, r1)


def _adamw(g, w, m, v):
    m = ADAM_B1 * m + (1.0 - ADAM_B1) * g
    v = ADAM_B2 * v + (1.0 - ADAM_B2) * jnp.square(g)
    m_hat = m / (1.0 - ADAM_B1 ** ADAM_STEP)
    v_hat = v / (1.0 - ADAM_B2 ** ADAM_STEP)
    return -ADAM_LR * (m_hat / (jnp.sqrt(v_hat) + ADAM_EPS) + ADAM_WD * w), m, v


def _sum_adamw(parts, w, m, v, name):
    npart, groups, rows, cols = parts.shape
    tr = _rows_tile(rows, cols)

    def body(p_ref, w_ref, m_ref, v_ref, g_ref, d_ref, nm_ref, nv_ref):
        g = p_ref[0].astype(F32)
        for k in range(1, npart):
            g = g + p_ref[k].astype(F32)
        g_ref[...] = g
        d_ref[...], nm_ref[...], nv_ref[...] = _adamw(g, w_ref[...], m_ref[...], v_ref[...])

    blk = pl.BlockSpec((None, tr, cols), lambda q, i: (q, i, 0))
    return _pcall(body, name=name, grid=(groups, rows // tr),
                  in_specs=[pl.BlockSpec((npart, None, tr, cols), lambda q, i: (0, q, i, 0)), blk, blk, blk],
                  out_specs=[blk] * 4, out_shape=[jax.ShapeDtypeStruct((groups, rows, cols), F32)] * 4,
                  compiler_params=_params(("parallel", "parallel")))(parts, w, m, v)


def _layer_sum_specs(own, layer):
    _, groups, rows, cols = own.shape
    tr = _rows_tile(rows, cols)
    own_spec = pl.BlockSpec((None, None, tr, cols), lambda q, i, ch: (ch[0], q, i, 0))
    theirs_spec = pl.BlockSpec((3, None, tr, cols), lambda q, i, ch: (0, q, i, 0))
    stacked = pl.BlockSpec((None, tr, cols), lambda q, i, ch: (layer * groups + q, i, 0))
    return (groups, rows // tr), own_spec, theirs_spec, stacked


def _layer_sum(own_ref, theirs_ref):
    g = own_ref[...].astype(F32)
    for k in range(3):
        g = g + theirs_ref[k].astype(F32)
    return g


def _sum_parts_layer(own, theirs, chip, buf, layer, name):
    grid, own_spec, theirs_spec, stacked = _layer_sum_specs(own, layer)

    def body(ch_ref, own_ref, theirs_ref, buf_ref, o_ref):
        o_ref[...] = _layer_sum(own_ref, theirs_ref)

    return _pcall(
        body, name=name,
        grid_spec=pltpu.PrefetchScalarGridSpec(num_scalar_prefetch=1, grid=grid, in_specs=[own_spec, theirs_spec, _ANY],
                                               out_specs=stacked),
        out_shape=jax.ShapeDtypeStruct(buf.shape, buf.dtype), input_output_aliases={3: 0},
        compiler_params=_params(("parallel", "parallel")))(chip, own, theirs, buf)


def _sum_adamw_layer(own, theirs, chip, w, m, v, outs, layer, deps, name):
    grid, own_spec, theirs_spec, stacked = _layer_sum_specs(own, layer)

    def body(ch_ref, own_ref, theirs_ref, w_ref, m_ref, v_ref, *rest):
        g_ref, d_ref, nm_ref, nv_ref = rest[-4:]
        g = _layer_sum(own_ref, theirs_ref)
        g_ref[...] = g
        d_ref[...], nm_ref[...], nv_ref[...] = _adamw(g, w_ref[...], m_ref[...], v_ref[...])

    return _pcall(
        body, name=name,
        grid_spec=pltpu.PrefetchScalarGridSpec(
            num_scalar_prefetch=1, grid=grid,
            in_specs=[own_spec, theirs_spec, stacked, stacked, stacked] + [_ANY] * (4 + len(deps)), out_specs=[stacked] * 4),
        out_shape=[jax.ShapeDtypeStruct(o.shape, o.dtype) for o in outs],
        input_output_aliases={6 + i: i for i in range(4)},
        compiler_params=_params(("parallel", "parallel")))(chip, own, theirs, w, m, v, *outs, *deps)


def _rot_cols(w):
    return jnp.concatenate([-w[..., ROPE // 2:], w[..., :ROPE // 2]], axis=-1)


def _rot_cols_t(dw):
    return jnp.concatenate([dw[..., ROPE // 2:], -dw[..., :ROPE // 2]], axis=-1)


_IN_SPLITS = np.cumsum([0, Q_RANK, KV_RANK, ROPE, BW, BW, BW, BW, BW, BW, HEADS, 3 * D_MODEL])


def _ext_w_in(w):
    o = _IN_SPLITS
    kpe = w[..., o[2]:o[3]]
    z = lambda n: jnp.zeros(w.shape[:-1] + (n,), w.dtype)
    return jnp.concatenate([w[..., o[10]:o[11]], w[..., o[3]:o[9]], w[..., o[0]:o[2]], kpe, z(LANE - ROPE),
                            _rot_cols(kpe), z(LANE - ROPE), w[..., o[9]:o[10]], z(2 * LANE - HEADS)], axis=-1)


def _unext_w_in(dw):
    kpe = dw[..., OFF_KPE:OFF_KPE + ROPE] + _rot_cols_t(dw[..., OFF_KROT:OFF_KROT + ROPE])
    return jnp.concatenate([dw[..., OFF_CQ:OFF_KPE], kpe, dw[..., OFF_CONV:OFF_CQ], dw[..., OFF_FL:OFF_FL + HEADS],
                            dw[..., OFF_GATE:OFF_CONV]], axis=-1)


def _ext_w_uq(w):
    w3 = w.reshape(w.shape[:-1] + (HEADS, HD + ROPE))
    pe = w3[..., HD:]
    z = jnp.zeros(w3.shape[:-1] + (LANE - ROPE,), w.dtype)
    return jnp.concatenate([w3[..., :HD], pe, z, _rot_cols(pe), z], axis=-1).reshape(w.shape[:-1] + (HEADS * QG,))


def _unext_w_uq(dw):
    d3 = dw.reshape(dw.shape[:-1] + (HEADS, QG))
    pe = d3[..., HD:HD + ROPE] + _rot_cols_t(d3[..., 2 * HD:2 * HD + ROPE])
    return jnp.concatenate([d3[..., :HD], pe], axis=-1).reshape(dw.shape[:-1] + (HEADS * (HD + ROPE),))


_BIG = (("meta", "col", F32), ("w_in", "row", BF16), ("w_uq", "row", BF16), ("w_ukv", "col", BF16),
        ("conv_w", "col", F32), ("w_branch", "col", BF16), ("w_out", "row", BF16), ("w_ffn_in", "col", BF16),
        ("w_ffn_out", "row", BF16))
_SMALL = ("b_forget", "g_q_lat", "g_kv_lat", "g_mix_pre", "g_mix_post", "g_ffn_pre", "g_ffn_post")
_ORDER = ("meta", "w_in", "b_forget", "g_q_lat", "g_kv_lat", "w_uq", "w_ukv", "conv_w", "w_branch", "w_out",
          "w_ffn_in", "w_ffn_out", "g_mix_pre", "g_mix_post", "g_ffn_pre", "g_ffn_post")


def _unshard_cols(g):
    g = jnp.moveaxis(g, 0, -2)
    return g.reshape(g.shape[:-2] + (g.shape[-2] * g.shape[-1],))


def _as3d(a, lead=0):
    return a.reshape(a.shape[:lead] + (-1,) + a.shape[-2:])


def kernel(x, meta, w_in, b_forget, g_q_lat, g_kv_lat, w_uq, w_ukv, conv_w, w_branch, w_out, w_ffn_in, w_ffn_out, g_mix_pre, g_mix_post, g_ffn_pre, g_ffn_post, loss_target, m_meta, m_w_in, m_b_forget, m_g_q_lat, m_g_kv_lat, m_w_uq, m_w_ukv, m_conv_w, m_w_branch, m_w_out, m_w_ffn_in, m_w_ffn_out, m_g_mix_pre, m_g_mix_post, m_g_ffn_pre, m_g_ffn_post, v_meta, v_w_in, v_b_forget, v_g_q_lat, v_g_kv_lat, v_w_uq, v_w_ukv, v_conv_w, v_w_branch, v_w_out, v_w_ffn_in, v_w_ffn_out, v_g_mix_pre, v_g_mix_post, v_g_ffn_pre, v_g_ffn_post):
    given = dict(locals())
    core = lax.axis_index("c").astype(jnp.int32).reshape(1)

    rows_in, rows_uq = _all_to_all([w_in.astype(BF16), w_uq.astype(BF16)], ("row", "row"), ("raw", "raw"),
                                   "rows_of_column_shards")
    shards = {n: given[n].astype(dt) for n, _, dt in _BIG}
    shards["w_in"] = _ext_w_in(_unshard_cols(rows_in))
    shards["w_uq"] = _ext_w_uq(_unshard_cols(rows_uq))
    (meta_full,) = _all_gather([shards["meta"]], ("col",), "gather_meta")
    per_layer = tuple((n, kd) for n, kd, _ in _BIG if n != "meta")
    layer_kinds = tuple(kd for _, kd in per_layer)
    pass_plan = _pass_on_plan(layer_kinds)

    def start_gather(layer, stacked):
        bufs = [lax.empty(_unsharded_shape(a.shape[1:], kd), a.dtype) for a, kd in zip(stacked, layer_kinds)]
        send, recv, stacked, bufs, token = _split_start(_gather_plan(layer_kinds, layer), 4 * len(per_layer), stacked, bufs,
                                                        "gather_start_%d" % layer)
        return dict(send=send, recv=recv, srcs=stacked, dsts=bufs, token=token)

    def land_gather(layer, fly, after):
        stacked, bufs = _split_wait(_gather_plan(layer_kinds, layer), fly["send"], fly["recv"], fly["srcs"], fly["dsts"],
                                    after, "gather_wait_%d" % layer)
        send, recv, _, bufs, token = _split_start(pass_plan, 4 * len(per_layer), [], bufs, "pass_on_start_%d" % layer)
        return stacked, dict(send=send, recv=recv, dsts=bufs, token=token)

    def finish_gather(layer, passing, after):
        _, bufs = _split_wait(pass_plan, passing["send"], passing["recv"], [], passing["dsts"], after,
                              "pass_on_wait_%d" % layer)
        return {n: b for (n, _), b in zip(per_layer, bufs)}

    gathering = start_gather(0, [shards[n] for n, _ in per_layer])

    pos = jnp.arange(LP, dtype=F32)[:, None]
    inv_freq = 1.0 / (ROPE_THETA ** (jnp.arange(0, ROPE, 2, dtype=F32) / ROPE))
    ang = pos * inv_freq[None, :]
    zpad = jnp.zeros((LP, LANE - ROPE), F32)
    cosp = jnp.concatenate([jnp.cos(ang), jnp.cos(ang), zpad], axis=1)
    sinp = jnp.concatenate([jnp.sin(ang), jnp.sin(ang), zpad], axis=1)

    tail = jnp.zeros((LP - L_TOK, D_MODEL), F32)
    h = jnp.concatenate([meta_full, x[0], tail], axis=0)
    ltp = jnp.concatenate([jnp.zeros((N_META, D_MODEL), F32), loss_target[0], tail], axis=0)
    row = jnp.arange(LP)[:, None]
    rmask = ((row >= N_META) & (row < L_TOK)).astype(F32)

    def vec(a, l):
        return a[l][None, :]

    wl = []
    for l in range(DEPTH):
        wl.append(dict(
            b_pad=jnp.concatenate([b_forget[l], jnp.zeros((LANE - HEADS,), F32)])[None, :],
            gq=vec(g_q_lat, l), gkv=vec(g_kv_lat, l), g1=vec(g_mix_pre, l), g2=vec(g_mix_post, l),
            g3=vec(g_ffn_pre, l), g4=vec(g_ffn_post, l)))

    def prep_ins(p_all, w):
        return [R(p_all, Q_RANK, _cb(OFF_CQ, Q_RANK)), R(p_all, KV_RANK, _cb(OFF_CKV, KV_RANK)),
                R(p_all, LANE, _cb(OFF_KPE, LANE)), R(p_all, LANE, _cb(OFF_KROT, LANE)), Pm(w["gq"]), Pm(w["gkv"]),
                R(cosp, LANE), R(sinp, LANE)]

    def merge_ins(p_all, ys):
        return [R(p_all, GW, _cb(OFF_GATE + n * D_MODEL, GW), 1) for n in range(3)] + [R(yv, GW, 0, 1) for yv in ys]

    def b16(total, w, cstep=0):
        return OR(total, w, cstep, BF16)

    (hn,) = _rw(lambda a, g: (_rms(a, g),), [R(h, D_MODEL), Pm(wl[0]["g1"])], [b16(D_MODEL, D_MODEL)], name="rms_in")
    saved, full = [], []
    stacked, passing = land_gather(0, gathering, h)
    for l in range(DEPTH):
        w = wl[l]
        s = dict(h=h, hn=hn)
        wts = finish_gather(l, passing, h if l == 0 else saved[-1]["f"])
        full.append(wts)
        if l + 1 < DEPTH:
            gathering = start_gather(l + 1, stacked)
        p_all = _mm(hn, wts["w_in"], deps=(gathering["token"],) if l + 1 < DEPTH else (), name="proj_in")
        cqn, ckvn, kper = _rw(_mla_prep, prep_ins(p_all, w),
                              [b16(Q_RANK, Q_RANK), b16(KV_RANK, KV_RANK), OR(LANE, LANE)], name="mla_prep")
        q = _mm(cqn, wts["w_uq"], name="proj_q")
        kv = _mm(ckvn, wts["w_ukv"], name="proj_kv")
        o_a = _mla_fwd(q, kv, kper, cosp, sinp)
        o_b = _conv_fwd(p_all, wts["conv_w"])
        cdec = _decay_fwd(p_all, w["b_pad"])
        cq3 = cdec[:, :HEADS].T[:, :, None]
        ck3 = cdec[:, :HEADS].T[:, None, :]
        o_c = _fox_fwd(p_all, cq3, ck3)
        outs = (o_a, o_b, o_c)
        ys = [_mm(outs[n], wts["w_branch"], bidx=(n,), name="proj_branch") for n in range(3)]
        (merged,) = _rw(_merge, merge_ins(p_all, ys), [b16(D_MODEL, GW, 1)], ncol=D_MODEL // GW, name="merge")
        mix = _mm(merged, wts["w_out"], name="proj_out")
        h2, hn2 = _rw(_resid_norm, [R(h, D_MODEL), R(mix, D_MODEL), Pm(w["g2"]), Pm(w["g3"])],
                      [OR(D_MODEL, D_MODEL), b16(D_MODEL, D_MODEL)], name="resid_norm")
        gu = _mm(hn2, wts["w_ffn_in"], name="ffn_in")
        (act,) = _rw(_swiglu, [R(gu, 2 * D_FF)], [b16(D_FF, D_FF)], tm=TM_FF, name="swiglu")
        if l + 1 < DEPTH:
            stacked, passing = land_gather(l + 1, gathering, act)
        f = _mm(act, wts["w_ffn_out"], deps=(passing["token"],) if l + 1 < DEPTH else (), name="ffn_out")
        s.update(p_all=p_all, cqn=cqn, ckvn=ckvn, kper=kper, q=q, kv=kv, outs=outs, cq3=cq3, ck3=ck3, ys=ys,
                 merged=merged, mix=mix, h2=h2, hn2=hn2, gu=gu, act=act, f=f)
        saved.append(s)
        if l + 1 < DEPTH:
            h, hn = _rw(_resid_norm, [R(h2, D_MODEL), R(f, D_MODEL), Pm(w["g4"]), Pm(wl[l + 1]["g1"])],
                        [OR(D_MODEL, D_MODEL), b16(D_MODEL, D_MODEL)], name="resid_norm")

    grads = {n: [None] * DEPTH for n in _SMALL + ("conv_w",)}
    mats = tuple(n for n, _, _ in _BIG if n not in ("meta", "conv_w"))
    mat_kinds = tuple(kd for n, kd, _ in _BIG if n in mats)
    via_rows = ("w_in", "w_uq")
    chip = (2 * lax.axis_index("x") + lax.axis_index("y")).astype(jnp.int32).reshape(1)
    sib_plan, chips_plan = _sibling_plan(mat_kinds), _chips_plan(len(mats))
    updates = {n: [lax.empty(_as3d(given[n]).shape, F32) for _ in range(4)] for n in mats if n not in via_rows}
    row_sums = {n: lax.empty((DEPTH, full[0][n].shape[0] // N_DEV, full[0][n].shape[1]), F32) for n in via_rows}
    landed = []
    flying = None

    def finish(layer, own, theirs, deps):
        for n, o, t in zip(mats, own, theirs):
            if n in via_rows:
                row_sums[n] = _sum_parts_layer(o, t, chip, row_sums[n], layer, "sum_" + n)
            else:
                updates[n] = _sum_adamw_layer(o, t, chip, _as3d(given[n]), _as3d(given["m_" + n]),
                                              _as3d(given["v_" + n]), updates[n], layer, deps, "adamw_" + n)

    def pair_sums(layer, srcs, from_sib):
        sums = [_pair_sum(_as3d(g), _as3d(r, 1), core, kd, "pair_sum_" + n)
                for n, kd, g, r in zip(mats, mat_kinds, srcs, from_sib)]
        return _split_start(chips_plan, 3 * len(mats), sums, [lax.empty((3,) + p.shape[1:], BF16) for p in sums],
                            "scatter_chips_start_%d" % layer)

    s, w = saved[-1], wl[-1]
    dh2, df, dg4, loss_acc = _rw(
        _loss_bwd, [R(s["h2"], D_MODEL), R(s["f"], D_MODEL), Pm(w["g4"]), R(ltp, D_MODEL), R(rmask, 1)],
        [OR(D_MODEL, D_MODEL), b16(D_MODEL, D_MODEL), OA((1, D_MODEL)), OA((1, LANE))], name="loss_bwd")
    loss = lax.psum(loss_acc[0, 0], ("x", "y", "c"))
    grads["g_ffn_post"][DEPTH - 1] = dg4[0]
    for l in reversed(range(DEPTH)):
        s, w = saved[l], wl[l]
        p_all = s["p_all"]
        gl = {}
        wts = full[l]
        dact = _mm(df, wts["w_ffn_out"], tb=True, deps=(flying["token"],) if flying else (), name="d_act")
        gl["w_ffn_out"] = _mm(s["act"], df, ta=True, out_dtype=BF16, name="dw_ffn_out")
        (dgu,) = _rw(_swiglu_bwd, [R(s["gu"], 2 * D_FF), R(dact, D_FF)], [b16(2 * D_FF, 2 * D_FF)], tm=TM_FF,
                     name="swiglu_bwd")
        if flying:
            srcs, from_sib = _split_wait(sib_plan, flying["send"], flying["recv"], flying["srcs"], flying["dsts"], dgu,
                                         "scatter_sibling_wait_%d" % (l + 1))
            send, recv, sums, slots, token = pair_sums(l + 1, srcs, from_sib)
            flying = dict(send=send, recv=recv, srcs=sums, dsts=slots, token=token)
        dhn2 = _mm(dgu, wts["w_ffn_in"], tb=True, deps=(flying["token"],) if flying else (), name="d_hn2")
        gl["w_ffn_in"] = _mm(s["hn2"], dgu, ta=True, out_dtype=BF16, name="dw_ffn_in")
        dh, dmix, dg2, dg3 = _rw(
            _resid_norm_bwd, [R(s["h"], D_MODEL), R(s["mix"], D_MODEL), Pm(w["g2"]), Pm(w["g3"]), R(dh2, D_MODEL),
                              R(dhn2, D_MODEL)],
            [OR(D_MODEL, D_MODEL), b16(D_MODEL, D_MODEL), OA((1, D_MODEL)), OA((1, D_MODEL))], name="resid_norm_bwd")
        grads["g_mix_post"][l], grads["g_ffn_pre"][l] = dg2[0], dg3[0]
        dmerged = _mm(dmix, wts["w_out"], tb=True, name="d_merged")
        gl["w_out"] = _mm(s["merged"], dmix, ta=True, out_dtype=BF16, name="dw_out")
        mb = _rw(_merge_bwd, merge_ins(p_all, s["ys"]) + [R(dmerged, GW, 0, 1)], [b16(D_MODEL, GW, 1)] * 6,
                 ncol=D_MODEL // GW, name="merge_bwd")
        dgate, dys = mb[:3], mb[3:]
        dos = [_mm(dys[n], wts["w_branch"], tb=True, bidx=(n,), name="d_branch") for n in range(3)]
        gl["w_branch"] = lax.empty(wts["w_branch"].shape, BF16)
        for n in range(3):
            gl["w_branch"] = _mm(s["outs"][n], dys[n], ta=True, stack=(gl["w_branch"], (n,)), name="dw_branch")
        dfq, dfk, dfv, dcq3, dck3 = _fox_bwd(p_all, s["cq3"], s["ck3"], dos[2])
        dc = jnp.concatenate([dcq3[:, :, 0].T + dck3[:, 0, :].T, jnp.zeros((LP, LANE - HEADS), F32)], axis=1)
        dfl, db = _decay_bwd(p_all, w["b_pad"], dc)
        grads["b_forget"][l] = db[0, :HEADS]
        dconv, dcw = _conv_bwd(p_all, wts["conv_w"], dos[1])
        grads["conv_w"][l] = dcw
        dq, dkv, dkper = _mla_bwd(s["q"], s["kv"], s["kper"], cosp, sinp, dos[0])
        dcqn = _mm(dq, wts["w_uq"], tb=True, name="d_cqn")
        gl["w_uq"] = _mm(s["cqn"], dq, ta=True, out_dtype=BF16, name="dw_uq")
        dckvn = _mm(dkv, wts["w_ukv"], tb=True, name="d_ckvn")
        gl["w_ukv"] = _mm(s["ckvn"], dkv, ta=True, out_dtype=BF16, name="dw_ukv")
        dcq, dckv, dkpe, dkrot, dgq, dgkv = _rw(
            _mla_prep_bwd, prep_ins(p_all, w) + [R(dcqn, Q_RANK), R(dckvn, KV_RANK), R(dkper, LANE)],
            [b16(Q_RANK, Q_RANK), b16(KV_RANK, KV_RANK), b16(LANE, LANE), b16(LANE, LANE), OA((1, Q_RANK)),
             OA((1, KV_RANK))], name="mla_prep_bwd")
        grads["g_q_lat"][l], grads["g_kv_lat"][l] = dgq[0], dgkv[0]
        dp = jnp.concatenate([*dgate, dconv[0], dconv[1], dconv[2], dfq, dfk.astype(BF16), dfv.astype(BF16), dcq, dckv,
                              dkpe, dkrot, dfl, jnp.zeros((LP, LANE), BF16)], axis=1)
        dhn = _mm(dp, wts["w_in"], tb=True, name="d_hn")
        gl["w_in"] = _mm(s["hn"], dp, ta=True, out_dtype=BF16, name="dw_in")
        if flying:
            sums, slots = _split_wait(chips_plan, flying["send"], flying["recv"], flying["srcs"], flying["dsts"], dhn,
                                      "scatter_chips_wait_%d" % (l + 1))
            landed.append((l + 1, sums, slots))
        parts_l = [gl[n] for n in mats]
        send, recv, srcs, dsts, token = _split_start(
            sib_plan, 4 * len(mats), parts_l,
            [lax.empty((4,) + _slice_shape(g.shape, kd), BF16) for g, kd in zip(parts_l, mat_kinds)],
            "scatter_sibling_start_%d" % l)
        flying = dict(send=send, recv=recv, srcs=srcs, dsts=dsts, token=token)
        if l > 0:
            sp, wp = saved[l - 1], wl[l - 1]
            dh2, df, dg4, dg1 = _rw(
                _resid_norm_bwd, [R(sp["h2"], D_MODEL), R(sp["f"], D_MODEL), Pm(wp["g4"]), Pm(w["g1"]),
                                  R(dh, D_MODEL), R(dhn, D_MODEL)],
                [OR(D_MODEL, D_MODEL), b16(D_MODEL, D_MODEL), OA((1, D_MODEL)), OA((1, D_MODEL))], name="resid_norm_bwd")
            grads["g_ffn_post"][l - 1], grads["g_mix_pre"][l] = dg4[0], dg1[0]
        else:
            dh0, dg1 = _rw(_rms_bwd, [R(s["h"], D_MODEL), Pm(w["g1"]), R(dhn, D_MODEL), R(dh, D_MODEL)],
                           [OR(D_MODEL, D_MODEL), OA((1, D_MODEL))], name="rms_in_bwd")
            grads["g_mix_pre"][0] = dg1[0]
    grad_x = dh0[N_META:L_TOK][None]
    gfull = {n: jnp.stack(grads[n]) for n in grads}
    gfull["meta"] = dh0[:N_META]

    srcs, from_sib = _split_wait(sib_plan, flying["send"], flying["recv"], flying["srcs"], flying["dsts"], dh0,
                                 "scatter_sibling_wait_0")
    send, recv, sums, slots, token = pair_sums(0, srcs, from_sib)
    for layer, own, theirs in landed:
        finish(layer, own, theirs, (token,))
    sums, slots = _split_wait(chips_plan, send, recv, sums, slots, updates[mats[-1]][0], "scatter_chips_wait_0")
    finish(0, sums, slots, ())

    few = ("meta", "conv_w")
    partial = [gfull[n] for n in few]
    from_sib = _swap_sibling(partial, ("col", "col"), "scatter_sibling_few")
    chip_sums = [_pair_sum(_as3d(g), _as3d(r, 1), core, "col", "pair_sum_" + n)
                 for n, g, r in zip(few, partial, from_sib)]
    parts = dict(zip(few, _swap_chips(chip_sums, "scatter_chips_few")))

    def by_dest(a):
        return jnp.moveaxis(a.reshape(a.shape[:-1] + (N_DEV, a.shape[-1] // N_DEV)), -2, 0).astype(BF16)

    cols_in, cols_uq = _all_to_all([by_dest(_unext_w_in(row_sums["w_in"])), by_dest(_unext_w_uq(row_sums["w_uq"]))],
                                   ("raw", "raw"), ("row", "row"), "columns_of_row_sums")
    parts["w_in"], parts["w_uq"] = _as3d(cols_in)[None], _as3d(cols_uq)[None]
    out = {n: [r.reshape(given[n].shape) for r in updates[n]] for n in updates}
    for n in parts:
        res = _sum_adamw(parts[n], _as3d(given[n]), _as3d(given["m_" + n]), _as3d(given["v_" + n]), "adamw_" + n)
        out[n] = [r.reshape(given[n].shape) for r in res]

    def pack(d):
        flat = jnp.concatenate([d[n].reshape(-1) for n in _SMALL])
        return jnp.concatenate([flat, jnp.zeros((-flat.shape[0]) % (8 * LANE), F32)]).reshape(-1, LANE)

    (small_parts,) = _all_gather([pack(gfull)], ("raw",), "gather_small_grads")
    res = _sum_adamw(small_parts[:, None], pack(given)[None], pack({n: given["m_" + n] for n in _SMALL})[None],
                     pack({n: given["v_" + n] for n in _SMALL})[None], "adamw_small")
    off = 0
    for n in _SMALL:
        size = int(np.prod(given[n].shape))
        out[n] = [r.reshape(-1)[off:off + size].reshape(given[n].shape) for r in res]
        off += size

    return (loss, grad_x, *[out[n][0] for n in _ORDER], *[out[n][1] for n in _ORDER], *[out[n][2] for n in _ORDER],
            *[out[n][3] for n in _ORDER])
```

```python
import functools

import numpy as np
import jax
import jax.numpy as jnp
from jax import lax
from jax.experimental import pallas as pl
from jax.experimental.pallas import tpu as pltpu

D_MODEL = 2048
SEQ = 2048
DEPTH = 4
Q_RANK = 512
KV_RANK = 512
D_FF = 5632
N_META = 16
HEADS = 8
HD = 128
ROPE = 64
BW = HEADS * HD
EPS = 1e-6
NEG_INF = -1e30
ROPE_THETA = 10000.0
N_DEV = 8
LANE = 128
L_TOK = N_META + SEQ
LP = -(-L_TOK // LANE) * LANE
D_IN = Q_RANK + KV_RANK + ROPE + 6 * BW + HEADS + 3 * D_MODEL
MLA_SCALE = (HD + ROPE) ** -0.5
FOX_SCALE = HD ** -0.5
ADAM_LR, ADAM_B1, ADAM_B2, ADAM_EPS, ADAM_WD, ADAM_STEP = 0.001, 0.9, 0.999, 1e-08, 0.01, 10
VMEM_LIMIT = 48 * 1024 * 1024

F32 = jnp.float32
BF16 = jnp.bfloat16
MESH = pl.DeviceIdType.MESH

OFF_GATE = 0
OFF_CONV = 3 * D_MODEL
OFF_FOX = OFF_CONV + 3 * BW
OFF_CQ = OFF_FOX + 3 * BW
OFF_CKV = OFF_CQ + Q_RANK
OFF_KPE = OFF_CKV + KV_RANK
OFF_KROT = OFF_KPE + LANE
OFF_FL = OFF_KROT + LANE
W_ALL = OFF_FL + 2 * LANE
GW = 512 if D_MODEL % 512 == 0 else 256
TM_FF = 64
CW = 128
QG = 3 * LANE


def _pick(n, prefs):
    for p in prefs:
        if n % p == 0:
            return p
    return n


TM_ROW = 128
TQ = _pick(LP, (272, 128))


def _cb(off, w):
    assert off % w == 0, (off, w)
    return off // w


def _pcall(body, **kw):
    return pl.pallas_call(body, **kw)


def _params(sem):
    return pltpu.CompilerParams(dimension_semantics=sem, vmem_limit_bytes=VMEM_LIMIT)


def _bf(x):
    return x.astype(BF16)


def _dot(a, b, dims, **kw):
    return lax.dot_general(_bf(a), _bf(b), (dims, ((), ())), preferred_element_type=F32, **kw)


_NN = ((1,), (0,))
_NT = ((1,), (1,))
_TN = ((0,), (0,))
_ANY = pl.BlockSpec(memory_space=pl.ANY)


MM_VMEM_BUDGET = 38 * 1024 * 1024
HBM_BYTES_PER_STEP = 1 << 20


def _divisors(n, prefs):
    return [p for p in prefs if n % p == 0] or [n]


def _mm_tiles(m, n, kd, a_bytes, b_bytes, o_bytes):
    best = None
    for tm in _divisors(m, (2176, 2048, 1088, 1024, 544, 512, 272, 256, 128)):
        for tn in _divisors(n, (2048, 1536, 1024, 768, 512, 384, 256, 128)):
            for tk in _divisors(kd, (2816, 2304, 2176, 2048, 1536, 1408, 1024, 768, 544, 512, 384, 272, 256, 128)):
                nk = kd // tk
                vmem = 2 * (tm * tk * a_bytes + tk * tn * b_bytes + tm * tn * o_bytes) + 2 * tm * tn * 4
                vmem += (tm * tk * 2 if a_bytes == 4 else 0) + (tk * tn * 2 if b_bytes == 4 else 0)
                if vmem > MM_VMEM_BUDGET:
                    continue
                steps = (m // tm) * (n // tn) * nk
                cost = (m * kd * a_bytes * (1 if nk == 1 else n // tn) + kd * n * b_bytes * (m // tm)
                        + steps * HBM_BYTES_PER_STEP)
                if best is None or cost < best[0]:
                    best = (cost, tm, tn, tk)
    assert best is not None, (m, n, kd)
    return best[1:]


def _mm(a, b, *, ta=False, tb=False, bidx=(), stack=None, out_dtype=F32, deps=(), name):
    if ta:
        kd, m = a.shape
    else:
        m, kd = a.shape
    nlead = len(bidx)
    if tb:
        n, kd2 = b.shape[nlead:]
    else:
        kd2, n = b.shape[nlead:]
    assert kd == kd2, (a.shape, b.shape, ta, tb)
    if stack is not None:
        out_dtype = stack[0].dtype
    tm, tn, tk = _mm_tiles(m, n, kd, a.dtype.itemsize, b.dtype.itemsize, jnp.dtype(out_dtype).itemsize)
    nk = kd // tk
    dims = _TN if ta else (_NT if tb else _NN)

    def body(a_ref, b_ref, *rest):
        o_ref, acc_ref = rest[-2:]
        if nk == 1:
            o_ref[...] = _dot(a_ref[...], b_ref[...], dims).astype(o_ref.dtype)
            return
        k = pl.program_id(2)

        @pl.when(k == 0)
        def _():
            acc_ref[...] = jnp.zeros_like(acc_ref)

        acc_ref[...] += _dot(a_ref[...], b_ref[...], dims)

        @pl.when(k == nk - 1)
        def _():
            o_ref[...] = acc_ref[...].astype(o_ref.dtype)

    lead = (None,) * nlead
    a_spec = pl.BlockSpec((tk, tm), lambda i, j, k: (k, i)) if ta else pl.BlockSpec((tm, tk), lambda i, j, k: (i, k))
    if tb:
        b_spec = pl.BlockSpec(lead + (tn, tk), lambda i, j, k: bidx + (j, k))
    else:
        b_spec = pl.BlockSpec(lead + (tk, tn), lambda i, j, k: bidx + (k, j))
    in_specs, args, extra = [a_spec, b_spec], [a, b], {}
    if stack is None:
        out_spec = pl.BlockSpec((tm, tn), lambda i, j, k: (i, j))
        out_shape = jax.ShapeDtypeStruct((m, n), out_dtype)
    else:
        buf, sidx = stack
        assert buf.shape[len(sidx):] == (m, n), (buf.shape, sidx, m, n)
        in_specs.append(_ANY)
        args.append(buf)
        extra = dict(input_output_aliases={2: 0})
        out_spec = pl.BlockSpec((None,) * len(sidx) + (tm, tn), lambda i, j, k: sidx + (i, j))
        out_shape = jax.ShapeDtypeStruct(buf.shape, buf.dtype)
    in_specs += [_ANY] * len(deps)
    args += list(deps)
    return _pcall(
        body, name=name, grid=(m // tm, n // tn, nk), in_specs=in_specs, out_specs=out_spec, out_shape=out_shape,
        scratch_shapes=[pltpu.VMEM((tm, tn) if nk > 1 else (8, LANE), F32)],
        compiler_params=_params(("parallel", "parallel", "arbitrary")), **extra)(*args)


class R:
    def __init__(self, arr, w, cb0=0, cstep=0):
        self.arr, self.w, self.cb0, self.cstep = arr, w, cb0, cstep


class Pm:
    def __init__(self, arr):
        self.arr = arr


class OR:
    def __init__(self, total, w, cstep=0, dtype=F32):
        self.total, self.w, self.cstep, self.dtype = total, w, cstep, dtype


class OA:
    def __init__(self, shape):
        self.shape = shape


def _rw(fn, ins, outs, *, name, ncol=1, tm=None):
    tm = TM_ROW if tm is None else tm
    nrow = LP // tm
    n_in = len(ins)

    def body(*refs):
        j, i = pl.program_id(0), pl.program_id(1)
        res = fn(*[r[...] for r in refs[:n_in]])
        for o, ref, val in zip(outs, refs[n_in:], res):
            if isinstance(o, OR):
                ref[...] = val.astype(ref.dtype)
            else:
                @pl.when((i == 0) & (j == 0))
                def _(ref=ref):
                    ref[...] = jnp.zeros_like(ref)

                ref[...] += val

    in_specs = []
    for s in ins:
        if isinstance(s, R):
            in_specs.append(pl.BlockSpec((tm, s.w), lambda j, i, s=s: (i, s.cb0 + j * s.cstep)))
        else:
            in_specs.append(pl.BlockSpec(s.arr.shape, lambda j, i, nd=s.arr.ndim: (0,) * nd))
    out_specs, out_shape = [], []
    for o in outs:
        if isinstance(o, OR):
            out_specs.append(pl.BlockSpec((tm, o.w), lambda j, i, o=o: (i, j * o.cstep)))
            out_shape.append(jax.ShapeDtypeStruct((LP, o.total), o.dtype))
        else:
            out_specs.append(pl.BlockSpec(o.shape, lambda j, i: (0, 0)))
            out_shape.append(jax.ShapeDtypeStruct(o.shape, F32))
    return _pcall(body, name=name, grid=(ncol, nrow), in_specs=in_specs, out_specs=out_specs, out_shape=out_shape,
                  compiler_params=_params(("arbitrary", "arbitrary")))(*[s.arr for s in ins])


def _rms(x, g):
    return x * lax.rsqrt(jnp.mean(x * x, axis=-1, keepdims=True) + EPS) * g


def _resid_norm(h, z, ga, gb):
    h2 = h + _rms(z, ga)
    return h2, _rms(h2, gb)


def _resid_norm_bwd(h, z, ga, gb, dh2, dhn2):
    _, vjp = jax.vjp(_resid_norm, h, z, ga, gb)
    return vjp((dh2, dhn2))


def _rms_bwd(h, g, dhn, dh_in):
    _, vjp = jax.vjp(_rms, h, g)
    dh, dg = vjp(dhn)
    return dh + dh_in, dg


def _loss_fn(h2, f, g4, lt, rmask):
    h3 = h2 + _rms(f, g4)
    err = jnp.square(h3 - lt)
    return 0.5 * jnp.sum(jnp.mean(err, axis=-1, keepdims=True) * rmask)


def _loss_bwd(h2, f, g4, lt, rmask):
    val, (dh2, df, dg4) = jax.value_and_grad(_loss_fn, argnums=(0, 1, 2))(h2, f, g4, lt, rmask)
    return dh2, df, dg4, jnp.broadcast_to(val, (1, LANE))


def _mla_prep(cq, ckv, kpe, krot, gq, gkv, cosp, sinp):
    return _rms(cq, gq), _rms(ckv, gkv), kpe * cosp + krot * sinp


def _mla_prep_bwd(cq, ckv, kpe, krot, gq, gkv, cosp, sinp, dcqn, dckvn, dkper):
    _, vjp = jax.vjp(lambda a, b, c, d, e, f: _mla_prep(a, b, c, d, e, f, cosp, sinp), cq, ckv, kpe, krot, gq, gkv)
    return vjp((dcqn, dckvn, dkper))


def _merge(g0, g1, g2, y0, y1, y2):
    return (jax.nn.sigmoid(g0) * y0 + jax.nn.sigmoid(g1) * y1 + jax.nn.sigmoid(g2) * y2,)


def _merge_bwd(g0, g1, g2, y0, y1, y2, dm):
    _, vjp = jax.vjp(_merge, g0, g1, g2, y0, y1, y2)
    return vjp((dm,))


def _swiglu(gu):
    g, u = gu[:, :D_FF], gu[:, D_FF:]
    return (g * jax.nn.sigmoid(g) * u,)


def _swiglu_bwd(gu, dact):
    _, vjp = jax.vjp(_swiglu, gu)
    return vjp((dact,))


def _causal_probs(s, iq):
    qpos = iq * TQ + lax.broadcasted_iota(jnp.int32, s.shape, 0)
    kpos = lax.broadcasted_iota(jnp.int32, s.shape, 1)
    s = jnp.where(kpos <= qpos, s, NEG_INF)
    e = jnp.exp(s - jnp.max(s, axis=-1, keepdims=True))
    return e / jnp.sum(e, axis=-1, keepdims=True)


def _softmax_vjp(p, dp):
    return p * (dp - jnp.sum(p * dp, axis=-1, keepdims=True))


def _mla_specs():
    return [pl.BlockSpec((TQ, QG), lambda h, i: (i, h)),
            pl.BlockSpec((LP, 2 * HD), lambda h, i: (0, h)),
            pl.BlockSpec((LP, LANE), lambda h, i: (0, 0)),
            pl.BlockSpec((TQ, LANE), lambda h, i: (i, 0)),
            pl.BlockSpec((TQ, LANE), lambda h, i: (i, 0))]


def _mla_parts(q_ref, kv_ref, cos_ref, sin_ref):
    q, kv = q_ref[...], kv_ref[...]
    qn = q[:, :HD]
    qp = q[:, HD:2 * HD] * cos_ref[...] + q[:, 2 * HD:] * sin_ref[...]
    return qn, qp, kv[:, :HD], kv[:, HD:]


def _mla_fwd(q, kv, kper, cosp, sinp):
    def body(q_ref, kv_ref, kp_ref, cos_ref, sin_ref, o_ref):
        iq = pl.program_id(1)
        q = q_ref[...]
        qn = q[:, :HD]
        qp = q[:, HD:2 * HD] * cos_ref[...] + q[:, 2 * HD:] * sin_ref[...]
        qpos = iq * TQ + lax.broadcasted_iota(jnp.int32, (TQ, TQ), 0)

        def block(kb, carry):
            m, l, acc = carry
            rows = pl.ds(pl.multiple_of(kb * TQ, 16), TQ)
            kv = kv_ref[rows, :]
            s = (_dot(qn, kv[:, :HD], _NT) + _dot(qp, kp_ref[rows, :], _NT)) * MLA_SCALE
            kpos = kb * TQ + lax.broadcasted_iota(jnp.int32, (TQ, TQ), 1)
            s = jnp.where(kpos <= qpos, s, NEG_INF)
            m_new = jnp.maximum(m, jnp.max(s, axis=-1, keepdims=True))
            p = jnp.exp(s - m_new)
            scale = jnp.exp(m - m_new)
            return (m_new, scale * l + jnp.sum(p, axis=-1, keepdims=True), scale * acc + _dot(p, kv[:, HD:], _NN))

        init = (jnp.full((TQ, 1), NEG_INF, F32), jnp.zeros((TQ, 1), F32), jnp.zeros((TQ, HD), F32))
        _, l, acc = lax.fori_loop(0, iq + 1, block, init)
        o_ref[...] = (acc / l).astype(o_ref.dtype)

    return _pcall(body, name="mla_fwd", grid=(HEADS, LP // TQ), in_specs=_mla_specs(),
                  out_specs=pl.BlockSpec((TQ, HD), lambda h, i: (i, h)),
                  out_shape=jax.ShapeDtypeStruct((LP, BW), BF16),
                  compiler_params=_params(("parallel", "parallel")))(q, kv, kper, cosp, sinp)


def _mla_bwd(q, kv, kper, cosp, sinp, do):
    def body(q_ref, kv_ref, kp_ref, cos_ref, sin_ref, do_ref, dq_ref, dkv_ref, dkp_ref):
        h, iq = pl.program_id(0), pl.program_id(1)
        qn, qp, kn, v = _mla_parts(q_ref, kv_ref, cos_ref, sin_ref)
        kp, dout = kp_ref[...], do_ref[...]
        p = _causal_probs((_dot(qn, kn, _NT) + _dot(qp, kp, _NT)) * MLA_SCALE, iq)
        ds = _softmax_vjp(p, _dot(dout, v, _NT)) * MLA_SCALE
        dqp = _dot(ds, kp, _NN)
        dq = jnp.concatenate([_dot(ds, kn, _NN), dqp * cos_ref[...], dqp * sin_ref[...]], axis=1)
        dq_ref[...] = dq.astype(dq_ref.dtype)

        @pl.when(iq == 0)
        def _():
            dkv_ref[...] = jnp.zeros_like(dkv_ref)

        dkv_ref[...] += jnp.concatenate([_dot(ds, qn, _TN), _dot(p, dout, _TN)], axis=1)

        @pl.when((iq == 0) & (h == 0))
        def _():
            dkp_ref[...] = jnp.zeros_like(dkp_ref)

        dkp_ref[...] += _dot(ds, qp, _TN)

    return _pcall(
        body, name="mla_bwd", grid=(HEADS, LP // TQ),
        in_specs=_mla_specs() + [pl.BlockSpec((TQ, HD), lambda h, i: (i, h))],
        out_specs=[pl.BlockSpec((TQ, QG), lambda h, i: (i, h)), pl.BlockSpec((LP, 2 * HD), lambda h, i: (0, h)),
                   pl.BlockSpec((LP, LANE), lambda h, i: (0, 0))],
        out_shape=[jax.ShapeDtypeStruct((LP, HEADS * QG), BF16), jax.ShapeDtypeStruct((LP, 2 * BW), F32),
                   jax.ShapeDtypeStruct((LP, LANE), F32)],
        compiler_params=_params(("arbitrary", "arbitrary")))(q, kv, kper, cosp, sinp, do)


def _fox_specs():
    cq, ck, cv = _cb(OFF_FOX, HD), _cb(OFF_FOX + BW, HD), _cb(OFF_FOX + 2 * BW, HD)
    return [pl.BlockSpec((TQ, HD), lambda h, i: (i, cq + h)),
            pl.BlockSpec((LP, HD), lambda h, i: (0, ck + h)),
            pl.BlockSpec((LP, HD), lambda h, i: (0, cv + h)),
            pl.BlockSpec((1, TQ, 1), lambda h, i: (h, i, 0)),
            pl.BlockSpec((1, 1, LP), lambda h, i: (h, 0, 0))]


def _fox_probs(q_ref, k_ref, cq_ref, ck_ref, iq):
    s = _dot(q_ref[...], k_ref[...], _NT) * FOX_SCALE + (cq_ref[0] - ck_ref[0])
    return _causal_probs(s, iq)


def _fox_fwd(p_all, cq3, ck3):
    def body(q_ref, k_ref, v_ref, cq_ref, ck_ref, o_ref):
        p = _fox_probs(q_ref, k_ref, cq_ref, ck_ref, pl.program_id(1))
        o_ref[...] = _dot(p, v_ref[...], _NN).astype(o_ref.dtype)

    return _pcall(body, name="fox_fwd", grid=(HEADS, LP // TQ), in_specs=_fox_specs(),
                  out_specs=pl.BlockSpec((TQ, HD), lambda h, i: (i, h)),
                  out_shape=jax.ShapeDtypeStruct((LP, BW), BF16),
                  compiler_params=_params(("parallel", "parallel")))(p_all, p_all, p_all, cq3, ck3)


def _fox_bwd(p_all, cq3, ck3, do):
    def body(q_ref, k_ref, v_ref, cq_ref, ck_ref, do_ref, dq_ref, dk_ref, dv_ref, dcq_ref, dck_ref):
        iq = pl.program_id(1)
        p = _fox_probs(q_ref, k_ref, cq_ref, ck_ref, iq)
        dout = do_ref[...]
        ds = _softmax_vjp(p, _dot(dout, v_ref[...], _NT))
        dss = ds * FOX_SCALE
        dq_ref[...] = _dot(dss, k_ref[...], _NN).astype(dq_ref.dtype)
        dcq_ref[0] = jnp.sum(ds, axis=1, keepdims=True)

        @pl.when(iq == 0)
        def _():
            dk_ref[...] = jnp.zeros_like(dk_ref)
            dv_ref[...] = jnp.zeros_like(dv_ref)
            dck_ref[...] = jnp.zeros_like(dck_ref)

        dk_ref[...] += _dot(dss, q_ref[...], _TN)
        dv_ref[...] += _dot(p, dout, _TN)
        dck_ref[0] -= jnp.sum(ds, axis=0, keepdims=True)

    head_rows = pl.BlockSpec((TQ, HD), lambda h, i: (i, h))
    head_all = pl.BlockSpec((LP, HD), lambda h, i: (0, h))
    return _pcall(
        body, name="fox_bwd", grid=(HEADS, LP // TQ), in_specs=_fox_specs() + [head_rows],
        out_specs=[head_rows, head_all, head_all, pl.BlockSpec((1, TQ, 1), lambda h, i: (h, i, 0)),
                   pl.BlockSpec((1, 1, LP), lambda h, i: (h, 0, 0))],
        out_shape=[jax.ShapeDtypeStruct((LP, BW), BF16)] + [jax.ShapeDtypeStruct((LP, BW), F32)] * 2
        + [jax.ShapeDtypeStruct((HEADS, LP, 1), F32), jax.ShapeDtypeStruct((HEADS, 1, LP), F32)],
        compiler_params=_params(("arbitrary", "arbitrary")))(p_all, p_all, p_all, cq3, ck3, do)


def _log_sigmoid(x):
    return jnp.minimum(x, 0.0) - jnp.log(1.0 + jnp.exp(-jnp.abs(x)))


def _decay_fwd(p_all, b_pad):
    tc = TM_ROW

    def body(fl_ref, b_ref, c_ref):
        lf = _log_sigmoid(fl_ref[...] + b_ref[...])
        r = pl.program_id(0) * tc + lax.broadcasted_iota(jnp.int32, (tc, LP), 0)
        s = lax.broadcasted_iota(jnp.int32, (tc, LP), 1)
        c_ref[...] = jnp.dot((s <= r).astype(F32), lf, precision=lax.Precision.HIGHEST, preferred_element_type=F32)

    return _pcall(body, name="decay_fwd", grid=(LP // tc,),
                  in_specs=[pl.BlockSpec((LP, LANE), lambda i: (0, _cb(OFF_FL, LANE))),
                            pl.BlockSpec((1, LANE), lambda i: (0, 0))],
                  out_specs=pl.BlockSpec((tc, LANE), lambda i: (i, 0)),
                  out_shape=jax.ShapeDtypeStruct((LP, LANE), F32), compiler_params=_params(("parallel",)))(p_all, b_pad)


def _decay_bwd(p_all, b_pad, dc):
    tc = TM_ROW

    def body(fl_ref, b_ref, dc_ref, dfl_ref, db_ref):
        i = pl.program_id(0)
        r = i * tc + lax.broadcasted_iota(jnp.int32, (tc, LP), 0)
        t = lax.broadcasted_iota(jnp.int32, (tc, LP), 1)
        dlf = jnp.dot((t >= r).astype(F32), dc_ref[...], precision=lax.Precision.HIGHEST, preferred_element_type=F32)
        dfl = dlf * jax.nn.sigmoid(-(fl_ref[...] + b_ref[...]))
        dfl_ref[...] = dfl.astype(dfl_ref.dtype)

        @pl.when(i == 0)
        def _():
            db_ref[...] = jnp.zeros_like(db_ref)

        db_ref[...] += jnp.sum(dfl, axis=0, keepdims=True)

    return _pcall(body, name="decay_bwd", grid=(LP // tc,),
                  in_specs=[pl.BlockSpec((tc, LANE), lambda i: (i, _cb(OFF_FL, LANE))),
                            pl.BlockSpec((1, LANE), lambda i: (0, 0)), pl.BlockSpec((LP, LANE), lambda i: (0, 0))],
                  out_specs=[pl.BlockSpec((tc, LANE), lambda i: (i, 0)), pl.BlockSpec((1, LANE), lambda i: (0, 0))],
                  out_shape=[jax.ShapeDtypeStruct((LP, LANE), BF16), jax.ShapeDtypeStruct((1, LANE), F32)],
                  compiler_params=_params(("arbitrary",)))(p_all, b_pad, dc)


def _conv_specs():
    c0 = _cb(OFF_CONV, CW)
    step = BW // CW
    return [pl.BlockSpec((LP, CW), lambda j: (0, c0 + j)), pl.BlockSpec((LP, CW), lambda j: (0, c0 + step + j)),
            pl.BlockSpec((LP, CW), lambda j: (0, c0 + 2 * step + j)),
            pl.BlockSpec((3, CW), lambda j: (0, j))]


def _shift_down(x, k, t):
    return jnp.where(t >= k, pltpu.roll(x, k, 0), 0.0)


def _shift_up(x, k, t):
    return jnp.where(t < LP - k, pltpu.roll(x, LP - k, 0), 0.0)


def _conv_fwd(p_all, cw):
    def body(b_ref, c_ref, x_ref, w_ref, o_ref):
        t = lax.broadcasted_iota(jnp.int32, (LP, CW), 0)
        uu = c_ref[...] * x_ref[...]
        w = w_ref[...]
        u = w[2:3] * uu + w[1:2] * _shift_down(uu, 1, t) + w[0:1] * _shift_down(uu, 2, t)
        o_ref[...] = (b_ref[...] * u).astype(o_ref.dtype)

    return _pcall(body, name="conv_fwd", grid=(BW // CW,), in_specs=_conv_specs(),
                  out_specs=pl.BlockSpec((LP, CW), lambda j: (0, j)), out_shape=jax.ShapeDtypeStruct((LP, BW), BF16),
                  compiler_params=_params(("parallel",)))(p_all, p_all, p_all, cw)


def _conv_bwd(p_all, cw, do):
    def body(b_ref, c_ref, x_ref, w_ref, do_ref, d_ref, dw_ref):
        t = lax.broadcasted_iota(jnp.int32, (LP, CW), 0)
        cc, xx, w, dout = c_ref[...], x_ref[...], w_ref[...], do_ref[...]
        uu = cc * xx
        s1, s2 = _shift_down(uu, 1, t), _shift_down(uu, 2, t)
        u = w[2:3] * uu + w[1:2] * s1 + w[0:1] * s2
        du = dout * b_ref[...]
        duu = w[2:3] * du + w[1:2] * _shift_up(du, 1, t) + w[0:1] * _shift_up(du, 2, t)
        d_ref[0] = (dout * u).astype(d_ref.dtype)
        d_ref[1] = (duu * xx).astype(d_ref.dtype)
        d_ref[2] = (duu * cc).astype(d_ref.dtype)
        dw_ref[0:1, :] = jnp.sum(du * s2, axis=0, keepdims=True)
        dw_ref[1:2, :] = jnp.sum(du * s1, axis=0, keepdims=True)
        dw_ref[2:3, :] = jnp.sum(du * uu, axis=0, keepdims=True)

    return _pcall(body, name="conv_bwd", grid=(BW // CW,),
                  in_specs=_conv_specs() + [pl.BlockSpec((LP, CW), lambda j: (0, j))],
                  out_specs=[pl.BlockSpec((3, LP, CW), lambda j: (0, 0, j)), pl.BlockSpec((3, CW), lambda j: (0, j))],
                  out_shape=[jax.ShapeDtypeStruct((3, LP, BW), BF16), jax.ShapeDtypeStruct((3, BW), F32)],
                  compiler_params=_params(("parallel",)))(p_all, p_all, p_all, cw, do)


def _place():
    return lax.axis_index("x"), lax.axis_index("y"), lax.axis_index("c")


def _dest_slice(ref, kind, d):
    if kind == "raw":
        return ref.at[d]
    nd = len(ref.shape)
    if kind == "col":
        w = ref.shape[-1] // N_DEV
        return ref.at[(slice(None),) * (nd - 1) + (pl.ds(d * w, w),)]
    r = ref.shape[-2] // N_DEV
    return ref.at[(slice(None),) * (nd - 2) + (pl.ds(d * r, r), slice(None))]


def _unsharded_shape(shape, kind, lead=N_DEV):
    if kind == "raw":
        return (lead,) + shape
    if kind == "col":
        return shape[:-1] + (N_DEV * shape[-1],)
    return shape[:-2] + (N_DEV * shape[-2], shape[-1])


def _all_gather(arrs, kinds, name):
    n = len(arrs)

    def body(*refs):
        ins, outs = refs[:n], refs[n:2 * n]
        send, recv, loc = refs[2 * n:]
        x, y, c = _place()
        sib = (x, y, 1 - c)
        chips = [(1 - x, y), (x, 1 - y), (1 - x, 1 - y)]

        def idx(px, py, pc):
            return 4 * px + 2 * py + pc

        def copy(j, k, block, to, src=None):
            dst = _dest_slice(outs[k], kinds[k], idx(*block))
            return pltpu.make_async_remote_copy(src_ref=dst if src is None else src, dst_ref=dst, send_sem=send.at[j, k],
                                                recv_sem=recv.at[j, k], device_id=to, device_id_type=MESH)

        me = (x, y, c)
        mine = [pltpu.make_async_copy(ins[k], _dest_slice(outs[k], kinds[k], idx(*me)), loc.at[k]) for k in range(n)]
        for cp in mine:
            cp.start()
        first = [copy(0, k, me, sib, src=ins[k]) for k in range(n)]
        first += [copy(1 + j, k, me, (*chip, c), src=ins[k]) for j, chip in enumerate(chips) for k in range(n)]
        for cp in first:
            cp.start()
        passed = []
        for j, chip in enumerate(chips):
            for k in range(n):
                copy(1 + j, k, (*chip, c), me, src=ins[k]).wait_recv()
                cp = copy(4 + j, k, (*chip, c), sib)
                cp.start()
                passed.append(cp)
        for k in range(n):
            copy(0, k, sib, me, src=ins[k]).wait_recv()
        for j, chip in enumerate(chips):
            for k in range(n):
                copy(4 + j, k, (*chip, 1 - c), me).wait_recv()
        for cp in first + passed:
            cp.wait_send()
        for cp in mine:
            cp.wait()

    return _pcall(body, name=name, in_specs=[_ANY] * n, out_specs=[_ANY] * n,
                  out_shape=[jax.ShapeDtypeStruct(_unsharded_shape(a.shape, kd), a.dtype) for a, kd in zip(arrs, kinds)],
                  scratch_shapes=[pltpu.SemaphoreType.DMA((7, n)), pltpu.SemaphoreType.DMA((7, n)),
                                  pltpu.SemaphoreType.DMA((n,))])(*arrs)


def _shard_shape(a, kind):
    if kind == "raw":
        return a.shape[2:]
    if kind == "col":
        return a.shape[:-1] + (a.shape[-1] // N_DEV,)
    return a.shape[:-2] + (a.shape[-2] // N_DEV, a.shape[-1])


def _swap_sibling(arrs, kinds, name):
    n = len(arrs)
    npieces = sum(1 if kd == "raw" else 4 for kd in kinds)

    def body(*refs):
        ins, outs = refs[:n], refs[n:2 * n]
        send, recv = refs[2 * n:]
        x, y, c = _place()
        pieces = []
        for k in range(n):
            if kinds[k] == "raw":
                pieces.append((ins[k].at[1 - c], outs[k]))
            else:
                pieces += [(_dest_slice(ins[k], kinds[k], 2 * p + 1 - c), outs[k].at[p]) for p in range(4)]
        cps = [pltpu.make_async_remote_copy(src_ref=src, dst_ref=dst, send_sem=send.at[i], recv_sem=recv.at[i],
                                            device_id=(x, y, 1 - c), device_id_type=MESH)
               for i, (src, dst) in enumerate(pieces)]
        for cp in cps:
            cp.start()
        for cp in cps:
            cp.wait()

    return _pcall(body, name=name, in_specs=[_ANY] * n, out_specs=[_ANY] * n,
                  out_shape=[jax.ShapeDtypeStruct((4,) + _shard_shape(a, kd), a.dtype) for a, kd in zip(arrs, kinds)],
                  scratch_shapes=[pltpu.SemaphoreType.DMA((npieces,)), pltpu.SemaphoreType.DMA((npieces,))])(*arrs)


def _swap_chips(arrs, name):
    n = len(arrs)

    def body(*refs):
        ins, outs = refs[:n], refs[n:2 * n]
        send, recv, loc = refs[2 * n:]
        x, y, c = _place()
        mychip = 2 * x + y
        mine = [pltpu.make_async_copy(ins[k].at[mychip], outs[k].at[mychip], loc.at[k]) for k in range(n)]
        for cp in mine:
            cp.start()
        cps = []
        for j, (cx, cy) in enumerate([(1 - x, y), (x, 1 - y), (1 - x, 1 - y)]):
            for k in range(n):
                cps.append(pltpu.make_async_remote_copy(
                    src_ref=ins[k].at[2 * cx + cy], dst_ref=outs[k].at[mychip], send_sem=send.at[j, k],
                    recv_sem=recv.at[j, k], device_id=(cx, cy, c), device_id_type=MESH))
        for cp in cps:
            cp.start()
        for cp in cps:
            cp.wait()
        for cp in mine:
            cp.wait()

    return _pcall(body, name=name, in_specs=[_ANY] * n, out_specs=[_ANY] * n,
                  out_shape=[jax.ShapeDtypeStruct(a.shape, a.dtype) for a in arrs],
                  scratch_shapes=[pltpu.SemaphoreType.DMA((3, n)), pltpu.SemaphoreType.DMA((3, n)),
                                  pltpu.SemaphoreType.DMA((n,))])(*arrs)


_HBM = pl.BlockSpec(memory_space=pltpu.HBM)
_SEM = pl.BlockSpec(memory_space=pltpu.SEMAPHORE)
_SIDE_EFFECT = pltpu.SideEffectType.DATAFLOW_SIDE_EFFECTING


def _split_copies(plan, refs, ns, nd):
    send, recv = refs[ns + nd], refs[ns + nd + 1]
    return [pltpu.make_async_remote_copy(src_ref=src, dst_ref=dst, send_sem=send.at[i], recv_sem=recv.at[i],
                                         device_id=dev, device_id_type=MESH)
            for i, (src, dst, dev) in enumerate(plan(refs[:ns], refs[ns:ns + nd]))]


def _split_start(plan, ncopies, srcs, dsts, name):
    ns, nd = len(srcs), len(dsts)

    def body(*refs):
        copies = _split_copies(plan, refs, ns, nd)
        assert len(copies) == ncopies
        for cp in copies:
            cp.start()
        refs[-1][...] = jnp.zeros_like(refs[-1])

    bufs = [pltpu.with_memory_space_constraint(a, pltpu.HBM) for a in list(srcs) + list(dsts)]
    res = _pcall(
        body, name=name, in_specs=[_HBM] * (ns + nd),
        out_specs=[_SEM, _SEM] + [_HBM] * (ns + nd) + [pl.BlockSpec(memory_space=pltpu.VMEM)],
        out_shape=[pltpu.SemaphoreType.DMA((ncopies,)), pltpu.SemaphoreType.DMA((ncopies,))]
        + [pltpu.HBM(a.shape, a.dtype) for a in bufs] + [jax.ShapeDtypeStruct((8, LANE), F32)],
        input_output_aliases={i: 2 + i for i in range(ns + nd)},
        compiler_params=pltpu.CompilerParams(has_side_effects=_SIDE_EFFECT))(*bufs)
    return res[0], res[1], list(res[2:2 + ns]), list(res[2 + ns:2 + ns + nd]), res[-1]


def _split_wait(plan, send, recv, srcs, dsts, after, name):
    ns, nd = len(srcs), len(dsts)

    def body(*refs):
        for cp in _split_copies(plan, refs, ns, nd):
            cp.wait_send()
            cp.wait_recv()

    res = _pcall(
        body, name=name, in_specs=[_HBM] * (ns + nd) + [_SEM, _SEM, _ANY], out_specs=[_HBM] * (ns + nd),
        out_shape=[pltpu.HBM(a.shape, a.dtype) for a in list(srcs) + list(dsts)],
        input_output_aliases={i: i for i in range(ns + nd)},
        compiler_params=pltpu.CompilerParams(has_side_effects=_SIDE_EFFECT))(*srcs, *dsts, send, recv, after)
    return list(res[:ns]), list(res[ns:])


def _sibling_plan(kinds):
    def plan(srcs, dsts):
        x, y, c = _place()
        return [(_dest_slice(srcs[k], kinds[k], 2 * p + 1 - c), dsts[k].at[p], (x, y, 1 - c))
                for k in range(len(kinds)) for p in range(4)]
    return plan


def _chips_plan(n):
    def plan(srcs, dsts):
        x, y, c = _place()
        return [(srcs[k].at[2 * cx + cy], dsts[k].at[j], (cx, cy, c))
                for j, (cx, cy) in enumerate([(1 - x, y), (x, 1 - y), (1 - x, 1 - y)]) for k in range(n)]
    return plan


def _gather_plan(kinds, layer):
    def plan(srcs, dsts):
        x, y, c = _place()
        me = 4 * x + 2 * y + c
        return [(srcs[k].at[layer], _dest_slice(dsts[k], kinds[k], me), dev)
                for k in range(len(kinds)) for dev in [(x, y, 1 - c), (1 - x, y, c), (x, 1 - y, c), (1 - x, 1 - y, c)]]
    return plan


def _pass_on_plan(kinds):
    def plan(srcs, dsts):
        x, y, c = _place()
        parts = [_dest_slice(dsts[k], kinds[k], 4 * px + 2 * py + pc) for k in range(len(kinds))
                 for px, py, pc in [(1 - x, y, c), (x, 1 - y, c), (1 - x, 1 - y, c), (x, y, 1 - c)]]
        return [(part, part, (x, y, 1 - c)) for part in parts]
    return plan


def _slice_shape(shape, kind):
    if kind == "raw":
        return shape[1:]
    if kind == "col":
        return shape[:-1] + (shape[-1] // N_DEV,)
    return shape[:-2] + (shape[-2] // N_DEV, shape[-1])


def _all_to_all(arrs, src_kinds, dst_kinds, name):
    n = len(arrs)

    def body(*refs):
        ins, outs = refs[:n], refs[n:2 * n]
        send, recv, loc = refs[2 * n:]
        x, y, c = _place()
        me = 4 * x + 2 * y + c
        mine = [pltpu.make_async_copy(_dest_slice(ins[k], src_kinds[k], me), _dest_slice(outs[k], dst_kinds[k], me),
                                      loc.at[k]) for k in range(n)]
        for cp in mine:
            cp.start()
        cps = []
        for r in range(1, N_DEV):
            px, py, pc = (1 - x if r & 4 else x), (1 - y if r & 2 else y), (1 - c if r & 1 else c)
            for k in range(n):
                cps.append(pltpu.make_async_remote_copy(
                    src_ref=_dest_slice(ins[k], src_kinds[k], 4 * px + 2 * py + pc),
                    dst_ref=_dest_slice(outs[k], dst_kinds[k], me), send_sem=send.at[r - 1, k],
                    recv_sem=recv.at[r - 1, k], device_id=(px, py, pc), device_id_type=MESH))
        for cp in cps:
            cp.start()
        for cp in cps:
            cp.wait()
        for cp in mine:
            cp.wait()

    out_shape = [jax.ShapeDtypeStruct(_unsharded_shape(_slice_shape(a.shape, sk), dk), a.dtype)
                 for a, sk, dk in zip(arrs, src_kinds, dst_kinds)]
    return _pcall(body, name=name, in_specs=[_ANY] * n, out_specs=[_ANY] * n, out_shape=out_shape,
                  scratch_shapes=[pltpu.SemaphoreType.DMA((N_DEV - 1, n)), pltpu.SemaphoreType.DMA((N_DEV - 1, n)),
                                  pltpu.SemaphoreType.DMA((n,))])(*arrs)


def _rows_tile(rows, cols=0):
    cap = 512 * 1024
    return _pick(rows, [t for t in (128, 64, 32, 16) if t * cols <= cap])


def _pair_sum(g, r1, core, kind, name):
    _, groups, rows, cols = r1.shape
    tr = _rows_tile(rows, cols)
    nr = rows // tr
    if kind == "raw":
        g_spec = pl.BlockSpec((None, None, None, tr, cols), lambda p, q, i, c: (c[0], p, q, i, 0))
    elif kind == "col":
        g_spec = pl.BlockSpec((None, tr, cols), lambda p, q, i, c: (q, i, 2 * p + c[0]))
    else:
        g_spec = pl.BlockSpec((None, tr, cols), lambda p, q, i, c: (q, (2 * p + c[0]) * nr + i, 0))
    r_spec = pl.BlockSpec((None, None, tr, cols), lambda p, q, i, c: (p, q, i, 0))

    def body(c_ref, g_ref, r_ref, o_ref):
        o_ref[...] = (g_ref[...].astype(F32) + r_ref[...].astype(F32)).astype(o_ref.dtype)

    return _pcall(
        body, name=name,
        grid_spec=pltpu.PrefetchScalarGridSpec(num_scalar_prefetch=1, grid=(4, groups, nr), in_specs=[g_spec, r_spec],
                                               out_specs=r_spec),
        out_shape=jax.ShapeDtypeStruct(r1.shape, r1.dtype),
        compiler_params=_params(("parallel", "parallel", "parallel")))(core, g, r1)


def _adamw(g, w, m, v):
    m = ADAM_B1 * m + (1.0 - ADAM_B1) * g
    v = ADAM_B2 * v + (1.0 - ADAM_B2) * jnp.square(g)
    m_hat = m / (1.0 - ADAM_B1 ** ADAM_STEP)
    v_hat = v / (1.0 - ADAM_B2 ** ADAM_STEP)
    return -ADAM_LR * (m_hat / (jnp.sqrt(v_hat) + ADAM_EPS) + ADAM_WD * w), m, v


def _sum_adamw(parts, w, m, v, name):
    npart, groups, rows, cols = parts.shape
    tr = _rows_tile(rows, cols)

    def body(p_ref, w_ref, m_ref, v_ref, g_ref, d_ref, nm_ref, nv_ref):
        g = p_ref[0].astype(F32)
        for k in range(1, npart):
            g = g + p_ref[k].astype(F32)
        g_ref[...] = g
        d_ref[...], nm_ref[...], nv_ref[...] = _adamw(g, w_ref[...], m_ref[...], v_ref[...])

    blk = pl.BlockSpec((None, tr, cols), lambda q, i: (q, i, 0))
    return _pcall(body, name=name, grid=(groups, rows // tr),
                  in_specs=[pl.BlockSpec((npart, None, tr, cols), lambda q, i: (0, q, i, 0)), blk, blk, blk],
                  out_specs=[blk] * 4, out_shape=[jax.ShapeDtypeStruct((groups, rows, cols), F32)] * 4,
                  compiler_params=_params(("parallel", "parallel")))(parts, w, m, v)


def _layer_sum_specs(own, layer):
    _, groups, rows, cols = own.shape
    tr = _rows_tile(rows, cols)
    own_spec = pl.BlockSpec((None, None, tr, cols), lambda q, i, ch: (ch[0], q, i, 0))
    theirs_spec = pl.BlockSpec((3, None, tr, cols), lambda q, i, ch: (0, q, i, 0))
    stacked = pl.BlockSpec((None, tr, cols), lambda q, i, ch: (layer * groups + q, i, 0))
    return (groups, rows // tr), own_spec, theirs_spec, stacked


def _layer_sum(own_ref, theirs_ref):
    g = own_ref[...].astype(F32)
    for k in range(3):
        g = g + theirs_ref[k].astype(F32)
    return g


def _sum_parts_layer(own, theirs, chip, buf, layer, name):
    grid, own_spec, theirs_spec, stacked = _layer_sum_specs(own, layer)

    def body(ch_ref, own_ref, theirs_ref, buf_ref, o_ref):
        o_ref[...] = _layer_sum(own_ref, theirs_ref)

    return _pcall(
        body, name=name,
        grid_spec=pltpu.PrefetchScalarGridSpec(num_scalar_prefetch=1, grid=grid, in_specs=[own_spec, theirs_spec, _ANY],
                                               out_specs=stacked),
        out_shape=jax.ShapeDtypeStruct(buf.shape, buf.dtype), input_output_aliases={3: 0},
        compiler_params=_params(("parallel", "parallel")))(chip, own, theirs, buf)


def _sum_adamw_layer(own, theirs, chip, w, m, v, outs, layer, deps, name):
    grid, own_spec, theirs_spec, stacked = _layer_sum_specs(own, layer)

    def body(ch_ref, own_ref, theirs_ref, w_ref, m_ref, v_ref, *rest):
        g_ref, d_ref, nm_ref, nv_ref = rest[-4:]
        g = _layer_sum(own_ref, theirs_ref)
        g_ref[...] = g
        d_ref[...], nm_ref[...], nv_ref[...] = _adamw(g, w_ref[...], m_ref[...], v_ref[...])

    return _pcall(
        body, name=name,
        grid_spec=pltpu.PrefetchScalarGridSpec(
            num_scalar_prefetch=1, grid=grid,
            in_specs=[own_spec, theirs_spec, stacked, stacked, stacked] + [_ANY] * (4 + len(deps)), out_specs=[stacked] * 4),
        out_shape=[jax.ShapeDtypeStruct(o.shape, o.dtype) for o in outs],
        input_output_aliases={6 + i: i for i in range(4)},
        compiler_params=_params(("parallel", "parallel")))(chip, own, theirs, w, m, v, *outs, *deps)


def _rot_cols(w):
    return jnp.concatenate([-w[..., ROPE // 2:], w[..., :ROPE // 2]], axis=-1)


def _rot_cols_t(dw):
    return jnp.concatenate([dw[..., ROPE // 2:], -dw[..., :ROPE // 2]], axis=-1)


_IN_SPLITS = np.cumsum([0, Q_RANK, KV_RANK, ROPE, BW, BW, BW, BW, BW, BW, HEADS, 3 * D_MODEL])


def _ext_w_in(w):
    o = _IN_SPLITS
    kpe = w[..., o[2]:o[3]]
    z = lambda n: jnp.zeros(w.shape[:-1] + (n,), w.dtype)
    return jnp.concatenate([w[..., o[10]:o[11]], w[..., o[3]:o[9]], w[..., o[0]:o[2]], kpe, z(LANE - ROPE),
                            _rot_cols(kpe), z(LANE - ROPE), w[..., o[9]:o[10]], z(2 * LANE - HEADS)], axis=-1)


def _unext_w_in(dw):
    kpe = dw[..., OFF_KPE:OFF_KPE + ROPE] + _rot_cols_t(dw[..., OFF_KROT:OFF_KROT + ROPE])
    return jnp.concatenate([dw[..., OFF_CQ:OFF_KPE], kpe, dw[..., OFF_CONV:OFF_CQ], dw[..., OFF_FL:OFF_FL + HEADS],
                            dw[..., OFF_GATE:OFF_CONV]], axis=-1)


def _ext_w_uq(w):
    w3 = w.reshape(w.shape[:-1] + (HEADS, HD + ROPE))
    pe = w3[..., HD:]
    z = jnp.zeros(w3.shape[:-1] + (LANE - ROPE,), w.dtype)
    return jnp.concatenate([w3[..., :HD], pe, z, _rot_cols(pe), z], axis=-1).reshape(w.shape[:-1] + (HEADS * QG,))


def _unext_w_uq(dw):
    d3 = dw.reshape(dw.shape[:-1] + (HEADS, QG))
    pe = d3[..., HD:HD + ROPE] + _rot_cols_t(d3[..., 2 * HD:2 * HD + ROPE])
    return jnp.concatenate([d3[..., :HD], pe], axis=-1).reshape(dw.shape[:-1] + (HEADS * (HD + ROPE),))


_BIG = (("meta", "col", F32), ("w_in", "row", BF16), ("w_uq", "row", BF16), ("w_ukv", "col", BF16),
        ("conv_w", "col", F32), ("w_branch", "col", BF16), ("w_out", "row", BF16), ("w_ffn_in", "col", BF16),
        ("w_ffn_out", "row", BF16))
_SMALL = ("b_forget", "g_q_lat", "g_kv_lat", "g_mix_pre", "g_mix_post", "g_ffn_pre", "g_ffn_post")
_ORDER = ("meta", "w_in", "b_forget", "g_q_lat", "g_kv_lat", "w_uq", "w_ukv", "conv_w", "w_branch", "w_out",
          "w_ffn_in", "w_ffn_out", "g_mix_pre", "g_mix_post", "g_ffn_pre", "g_ffn_post")


def _unshard_cols(g):
    g = jnp.moveaxis(g, 0, -2)
    return g.reshape(g.shape[:-2] + (g.shape[-2] * g.shape[-1],))


def _as3d(a, lead=0):
    return a.reshape(a.shape[:lead] + (-1,) + a.shape[-2:])


def kernel(x, meta, w_in, b_forget, g_q_lat, g_kv_lat, w_uq, w_ukv, conv_w, w_branch, w_out, w_ffn_in, w_ffn_out, g_mix_pre, g_mix_post, g_ffn_pre, g_ffn_post, loss_target, m_meta, m_w_in, m_b_forget, m_g_q_lat, m_g_kv_lat, m_w_uq, m_w_ukv, m_conv_w, m_w_branch, m_w_out, m_w_ffn_in, m_w_ffn_out, m_g_mix_pre, m_g_mix_post, m_g_ffn_pre, m_g_ffn_post, v_meta, v_w_in, v_b_forget, v_g_q_lat, v_g_kv_lat, v_w_uq, v_w_ukv, v_conv_w, v_w_branch, v_w_out, v_w_ffn_in, v_w_ffn_out, v_g_mix_pre, v_g_mix_post, v_g_ffn_pre, v_g_ffn_post):
    given = dict(locals())
    core = lax.axis_index("c").astype(jnp.int32).reshape(1)

    rows_in, rows_uq = _all_to_all([w_in.astype(BF16), w_uq.astype(BF16)], ("row", "row"), ("raw", "raw"),
                                   "rows_of_column_shards")
    shards = {n: given[n].astype(dt) for n, _, dt in _BIG}
    shards["w_in"] = _ext_w_in(_unshard_cols(rows_in))
    shards["w_uq"] = _ext_w_uq(_unshard_cols(rows_uq))
    (meta_full,) = _all_gather([shards["meta"]], ("col",), "gather_meta")
    per_layer = tuple((n, kd) for n, kd, _ in _BIG if n != "meta")
    layer_kinds = tuple(kd for _, kd in per_layer)
    pass_plan = _pass_on_plan(layer_kinds)

    def start_gather(layer, stacked):
        bufs = [lax.empty(_unsharded_shape(a.shape[1:], kd), a.dtype) for a, kd in zip(stacked, layer_kinds)]
        send, recv, stacked, bufs, token = _split_start(_gather_plan(layer_kinds, layer), 4 * len(per_layer), stacked, bufs,
                                                        "gather_start_%d" % layer)
        return dict(send=send, recv=recv, srcs=stacked, dsts=bufs, token=token)

    def land_gather(layer, fly, after):
        stacked, bufs = _split_wait(_gather_plan(layer_kinds, layer), fly["send"], fly["recv"], fly["srcs"], fly["dsts"],
                                    after, "gather_wait_%d" % layer)
        send, recv, _, bufs, token = _split_start(pass_plan, 4 * len(per_layer), [], bufs, "pass_on_start_%d" % layer)
        return stacked, dict(send=send, recv=recv, dsts=bufs, token=token)

    def finish_gather(layer, passing, after):
        _, bufs = _split_wait(pass_plan, passing["send"], passing["recv"], [], passing["dsts"], after,
                              "pass_on_wait_%d" % layer)
        return {n: b for (n, _), b in zip(per_layer, bufs)}

    gathering = start_gather(0, [shards[n] for n, _ in per_layer])

    pos = jnp.arange(LP, dtype=F32)[:, None]
    inv_freq = 1.0 / (ROPE_THETA ** (jnp.arange(0, ROPE, 2, dtype=F32) / ROPE))
    ang = pos * inv_freq[None, :]
    zpad = jnp.zeros((LP, LANE - ROPE), F32)
    cosp = jnp.concatenate([jnp.cos(ang), jnp.cos(ang), zpad], axis=1)
    sinp = jnp.concatenate([jnp.sin(ang), jnp.sin(ang), zpad], axis=1)

    tail = jnp.zeros((LP - L_TOK, D_MODEL), F32)
    h = jnp.concatenate([meta_full, x[0], tail], axis=0)
    ltp = jnp.concatenate([jnp.zeros((N_META, D_MODEL), F32), loss_target[0], tail], axis=0)
    row = jnp.arange(LP)[:, None]
    rmask = ((row >= N_META) & (row < L_TOK)).astype(F32)

    def vec(a, l):
        return a[l][None, :]

    wl = []
    for l in range(DEPTH):
        wl.append(dict(
            b_pad=jnp.concatenate([b_forget[l], jnp.zeros((LANE - HEADS,), F32)])[None, :],
            gq=vec(g_q_lat, l), gkv=vec(g_kv_lat, l), g1=vec(g_mix_pre, l), g2=vec(g_mix_post, l),
            g3=vec(g_ffn_pre, l), g4=vec(g_ffn_post, l)))

    def prep_ins(p_all, w):
        return [R(p_all, Q_RANK, _cb(OFF_CQ, Q_RANK)), R(p_all, KV_RANK, _cb(OFF_CKV, KV_RANK)),
                R(p_all, LANE, _cb(OFF_KPE, LANE)), R(p_all, LANE, _cb(OFF_KROT, LANE)), Pm(w["gq"]), Pm(w["gkv"]),
                R(cosp, LANE), R(sinp, LANE)]

    def merge_ins(p_all, ys):
        return [R(p_all, GW, _cb(OFF_GATE + n * D_MODEL, GW), 1) for n in range(3)] + [R(yv, GW, 0, 1) for yv in ys]

    def b16(total, w, cstep=0):
        return OR(total, w, cstep, BF16)

    (hn,) = _rw(lambda a, g: (_rms(a, g),), [R(h, D_MODEL), Pm(wl[0]["g1"])], [b16(D_MODEL, D_MODEL)], name="rms_in")
    saved, full = [], []
    stacked, passing = land_gather(0, gathering, h)
    for l in range(DEPTH):
        w = wl[l]
        s = dict(h=h, hn=hn)
        wts = finish_gather(l, passing, h if l == 0 else saved[-1]["f"])
        full.append(wts)
        if l + 1 < DEPTH:
            gathering = start_gather(l + 1, stacked)
        p_all = _mm(hn, wts["w_in"], deps=(gathering["token"],) if l + 1 < DEPTH else (), name="proj_in")
        cqn, ckvn, kper = _rw(_mla_prep, prep_ins(p_all, w),
                              [b16(Q_RANK, Q_RANK), b16(KV_RANK, KV_RANK), OR(LANE, LANE)], name="mla_prep")
        q = _mm(cqn, wts["w_uq"], name="proj_q")
        kv = _mm(ckvn, wts["w_ukv"], name="proj_kv")
        o_a = _mla_fwd(q, kv, kper, cosp, sinp)
        o_b = _conv_fwd(p_all, wts["conv_w"])
        cdec = _decay_fwd(p_all, w["b_pad"])
        cq3 = cdec[:, :HEADS].T[:, :, None]
        ck3 = cdec[:, :HEADS].T[:, None, :]
        o_c = _fox_fwd(p_all, cq3, ck3)
        outs = (o_a, o_b, o_c)
        ys = [_mm(outs[n], wts["w_branch"], bidx=(n,), name="proj_branch") for n in range(3)]
        (merged,) = _rw(_merge, merge_ins(p_all, ys), [b16(D_MODEL, GW, 1)], ncol=D_MODEL // GW, name="merge")
        mix = _mm(merged, wts["w_out"], name="proj_out")
        h2, hn2 = _rw(_resid_norm, [R(h, D_MODEL), R(mix, D_MODEL), Pm(w["g2"]), Pm(w["g3"])],
                      [OR(D_MODEL, D_MODEL), b16(D_MODEL, D_MODEL)], name="resid_norm")
        gu = _mm(hn2, wts["w_ffn_in"], name="ffn_in")
        (act,) = _rw(_swiglu, [R(gu, 2 * D_FF)], [b16(D_FF, D_FF)], tm=TM_FF, name="swiglu")
        if l + 1 < DEPTH:
            stacked, passing = land_gather(l + 1, gathering, act)
        f = _mm(act, wts["w_ffn_out"], deps=(passing["token"],) if l + 1 < DEPTH else (), name="ffn_out")
        s.update(p_all=p_all, cqn=cqn, ckvn=ckvn, kper=kper, q=q, kv=kv, outs=outs, cq3=cq3, ck3=ck3, ys=ys,
                 merged=merged, mix=mix, h2=h2, hn2=hn2, gu=gu, act=act, f=f)
        saved.append(s)
        if l + 1 < DEPTH:
            h, hn = _rw(_resid_norm, [R(h2, D_MODEL), R(f, D_MODEL), Pm(w["g4"]), Pm(wl[l + 1]["g1"])],
                        [OR(D_MODEL, D_MODEL), b16(D_MODEL, D_MODEL)], name="resid_norm")

    grads = {n: [None] * DEPTH for n in _SMALL + ("conv_w",)}
    mats = tuple(n for n, _, _ in _BIG if n not in ("meta", "conv_w"))
    mat_kinds = tuple(kd for n, kd, _ in _BIG if n in mats)
    via_rows = ("w_in", "w_uq")
    chip = (2 * lax.axis_index("x") + lax.axis_index("y")).astype(jnp.int32).reshape(1)
    sib_plan, chips_plan = _sibling_plan(mat_kinds), _chips_plan(len(mats))
    updates = {n: [lax.empty(_as3d(given[n]).shape, F32) for _ in range(4)] for n in mats if n not in via_rows}
    row_sums = {n: lax.empty((DEPTH, full[0][n].shape[0] // N_DEV, full[0][n].shape[1]), F32) for n in via_rows}
    landed = []
    flying = None

    def finish(layer, own, theirs, deps):
        for n, o, t in zip(mats, own, theirs):
            if n in via_rows:
                row_sums[n] = _sum_parts_layer(o, t, chip, row_sums[n], layer, "sum_" + n)
            else:
                updates[n] = _sum_adamw_layer(o, t, chip, _as3d(given[n]), _as3d(given["m_" + n]),
                                              _as3d(given["v_" + n]), updates[n], layer, deps, "adamw_" + n)

    def pair_sums(layer, srcs, from_sib):
        sums = [_pair_sum(_as3d(g), _as3d(r, 1), core, kd, "pair_sum_" + n)
                for n, kd, g, r in zip(mats, mat_kinds, srcs, from_sib)]
        return _split_start(chips_plan, 3 * len(mats), sums, [lax.empty((3,) + p.shape[1:], BF16) for p in sums],
                            "scatter_chips_start_%d" % layer)

    s, w = saved[-1], wl[-1]
    dh2, df, dg4, loss_acc = _rw(
        _loss_bwd, [R(s["h2"], D_MODEL), R(s["f"], D_MODEL), Pm(w["g4"]), R(ltp, D_MODEL), R(rmask, 1)],
        [OR(D_MODEL, D_MODEL), b16(D_MODEL, D_MODEL), OA((1, D_MODEL)), OA((1, LANE))], name="loss_bwd")
    loss = lax.psum(loss_acc[0, 0], ("x", "y", "c"))
    grads["g_ffn_post"][DEPTH - 1] = dg4[0]
    for l in reversed(range(DEPTH)):
        s, w = saved[l], wl[l]
        p_all = s["p_all"]
        gl = {}
        wts = full[l]
        dact = _mm(df, wts["w_ffn_out"], tb=True, deps=(flying["token"],) if flying else (), name="d_act")
        gl["w_ffn_out"] = _mm(s["act"], df, ta=True, out_dtype=BF16, name="dw_ffn_out")
        (dgu,) = _rw(_swiglu_bwd, [R(s["gu"], 2 * D_FF), R(dact, D_FF)], [b16(2 * D_FF, 2 * D_FF)], tm=TM_FF,
                     name="swiglu_bwd")
        if flying:
            srcs, from_sib = _split_wait(sib_plan, flying["send"], flying["recv"], flying["srcs"], flying["dsts"], dgu,
                                         "scatter_sibling_wait_%d" % (l + 1))
            send, recv, sums, slots, token = pair_sums(l + 1, srcs, from_sib)
            flying = dict(send=send, recv=recv, srcs=sums, dsts=slots, token=token)
        dhn2 = _mm(dgu, wts["w_ffn_in"], tb=True, deps=(flying["token"],) if flying else (), name="d_hn2")
        gl["w_ffn_in"] = _mm(s["hn2"], dgu, ta=True, out_dtype=BF16, name="dw_ffn_in")
        dh, dmix, dg2, dg3 = _rw(
            _resid_norm_bwd, [R(s["h"], D_MODEL), R(s["mix"], D_MODEL), Pm(w["g2"]), Pm(w["g3"]), R(dh2, D_MODEL),
                              R(dhn2, D_MODEL)],
            [OR(D_MODEL, D_MODEL), b16(D_MODEL, D_MODEL), OA((1, D_MODEL)), OA((1, D_MODEL))], name="resid_norm_bwd")
        grads["g_mix_post"][l], grads["g_ffn_pre"][l] = dg2[0], dg3[0]
        dmerged = _mm(dmix, wts["w_out"], tb=True, name="d_merged")
        gl["w_out"] = _mm(s["merged"], dmix, ta=True, out_dtype=BF16, name="dw_out")
        mb = _rw(_merge_bwd, merge_ins(p_all, s["ys"]) + [R(dmerged, GW, 0, 1)], [b16(D_MODEL, GW, 1)] * 6,
                 ncol=D_MODEL // GW, name="merge_bwd")
        dgate, dys = mb[:3], mb[3:]
        dos = [_mm(dys[n], wts["w_branch"], tb=True, bidx=(n,), name="d_branch") for n in range(3)]
        gl["w_branch"] = lax.empty(wts["w_branch"].shape, BF16)
        for n in range(3):
            gl["w_branch"] = _mm(s["outs"][n], dys[n], ta=True, stack=(gl["w_branch"], (n,)), name="dw_branch")
        dfq, dfk, dfv, dcq3, dck3 = _fox_bwd(p_all, s["cq3"], s["ck3"], dos[2])
        dc = jnp.concatenate([dcq3[:, :, 0].T + dck3[:, 0, :].T, jnp.zeros((LP, LANE - HEADS), F32)], axis=1)
        dfl, db = _decay_bwd(p_all, w["b_pad"], dc)
        grads["b_forget"][l] = db[0, :HEADS]
        dconv, dcw = _conv_bwd(p_all, wts["conv_w"], dos[1])
        grads["conv_w"][l] = dcw
        dq, dkv, dkper = _mla_bwd(s["q"], s["kv"], s["kper"], cosp, sinp, dos[0])
        dcqn = _mm(dq, wts["w_uq"], tb=True, name="d_cqn")
        gl["w_uq"] = _mm(s["cqn"], dq, ta=True, out_dtype=BF16, name="dw_uq")
        dckvn = _mm(dkv, wts["w_ukv"], tb=True, name="d_ckvn")
        gl["w_ukv"] = _mm(s["ckvn"], dkv, ta=True, out_dtype=BF16, name="dw_ukv")
        dcq, dckv, dkpe, dkrot, dgq, dgkv = _rw(
            _mla_prep_bwd, prep_ins(p_all, w) + [R(dcqn, Q_RANK), R(dckvn, KV_RANK), R(dkper, LANE)],
            [b16(Q_RANK, Q_RANK), b16(KV_RANK, KV_RANK), b16(LANE, LANE), b16(LANE, LANE), OA((1, Q_RANK)),
             OA((1, KV_RANK))], name="mla_prep_bwd")
        grads["g_q_lat"][l], grads["g_kv_lat"][l] = dgq[0], dgkv[0]
        dp = jnp.concatenate([*dgate, dconv[0], dconv[1], dconv[2], dfq, dfk.astype(BF16), dfv.astype(BF16), dcq, dckv,
                              dkpe, dkrot, dfl, jnp.zeros((LP, LANE), BF16)], axis=1)
        dhn = _mm(dp, wts["w_in"], tb=True, name="d_hn")
        gl["w_in"] = _mm(s["hn"], dp, ta=True, out_dtype=BF16, name="dw_in")
        if flying:
            sums, slots = _split_wait(chips_plan, flying["send"], flying["recv"], flying["srcs"], flying["dsts"], dhn,
                                      "scatter_chips_wait_%d" % (l + 1))
            landed.append((l + 1, sums, slots))
        parts_l = [gl[n] for n in mats]
        send, recv, srcs, dsts, token = _split_start(
            sib_plan, 4 * len(mats), parts_l,
            [lax.empty((4,) + _slice_shape(g.shape, kd), BF16) for g, kd in zip(parts_l, mat_kinds)],
            "scatter_sibling_start_%d" % l)
        flying = dict(send=send, recv=recv, srcs=srcs, dsts=dsts, token=token)
        if l > 0:
            sp, wp = saved[l - 1], wl[l - 1]
            dh2, df, dg4, dg1 = _rw(
                _resid_norm_bwd, [R(sp["h2"], D_MODEL), R(sp["f"], D_MODEL), Pm(wp["g4"]), Pm(w["g1"]),
                                  R(dh, D_MODEL), R(dhn, D_MODEL)],
                [OR(D_MODEL, D_MODEL), b16(D_MODEL, D_MODEL), OA((1, D_MODEL)), OA((1, D_MODEL))], name="resid_norm_bwd")
            grads["g_ffn_post"][l - 1], grads["g_mix_pre"][l] = dg4[0], dg1[0]
        else:
            dh0, dg1 = _rw(_rms_bwd, [R(s["h"], D_MODEL), Pm(w["g1"]), R(dhn, D_MODEL), R(dh, D_MODEL)],
                           [OR(D_MODEL, D_MODEL), OA((1, D_MODEL))], name="rms_in_bwd")
            grads["g_mix_pre"][0] = dg1[0]
    grad_x = dh0[N_META:L_TOK][None]
    gfull = {n: jnp.stack(grads[n]) for n in grads}
    gfull["meta"] = dh0[:N_META]

    srcs, from_sib = _split_wait(sib_plan, flying["send"], flying["recv"], flying["srcs"], flying["dsts"], dh0,
                                 "scatter_sibling_wait_0")
    send, recv, sums, slots, token = pair_sums(0, srcs, from_sib)
    for layer, own, theirs in landed:
        finish(layer, own, theirs, (token,))
    sums, slots = _split_wait(chips_plan, send, recv, sums, slots, updates[mats[-1]][0], "scatter_chips_wait_0")
    finish(0, sums, slots, ())

    few = ("meta", "conv_w")
    partial = [gfull[n] for n in few]
    from_sib = _swap_sibling(partial, ("col", "col"), "scatter_sibling_few")
    chip_sums = [_pair_sum(_as3d(g), _as3d(r, 1), core, "col", "pair_sum_" + n)
                 for n, g, r in zip(few, partial, from_sib)]
    parts = dict(zip(few, _swap_chips(chip_sums, "scatter_chips_few")))

    def by_dest(a):
        return jnp.moveaxis(a.reshape(a.shape[:-1] + (N_DEV, a.shape[-1] // N_DEV)), -2, 0).astype(BF16)

    cols_in, cols_uq = _all_to_all([by_dest(_unext_w_in(row_sums["w_in"])), by_dest(_unext_w_uq(row_sums["w_uq"]))],
                                   ("raw", "raw"), ("row", "row"), "columns_of_row_sums")
    parts["w_in"], parts["w_uq"] = _as3d(cols_in)[None], _as3d(cols_uq)[None]
    out = {n: [r.reshape(given[n].shape) for r in updates[n]] for n in updates}
    for n in parts:
        res = _sum_adamw(parts[n], _as3d(given[n]), _as3d(given["m_" + n]), _as3d(given["v_" + n]), "adamw_" + n)
        out[n] = [r.reshape(given[n].shape) for r in res]

    def pack(d):
        flat = jnp.concatenate([d[n].reshape(-1) for n in _SMALL])
        return jnp.concatenate([flat, jnp.zeros((-flat.shape[0]) % (8 * LANE), F32)]).reshape(-1, LANE)

    (small_parts,) = _all_gather([pack(gfull)], ("raw",), "gather_small_grads")
    res = _sum_adamw(small_parts[:, None], pack(given)[None], pack({n: given["m_" + n] for n in _SMALL})[None],
                     pack({n: given["v_" + n] for n in _SMALL})[None], "adamw_small")
    off = 0
    for n in _SMALL:
        size = int(np.prod(given[n].shape))
        out[n] = [r.reshape(-1)[off:off + size].reshape(given[n].shape) for r in res]
        off += size

    return (loss, grad_x, *[out[n][0] for n in _ORDER], *[out[n][1] for n in _ORDER], *[out[n][2] for n in _ORDER],
            *[out[n][3] for n in _ORDER])
```
